```python
import math, functools
import jax, jax.numpy as jnp
from jax import lax
import numpy as np

D_MODEL = 4096
BATCH = 4
SEQ = 2048
DEPTH = 1
DEC_BATCH = 32
DEC_SEQ = 1
PAST_LEN = 8192
PAGE_SIZE = 128

HEAD_DIM = 128
SB_HEADS = (D_MODEL // 2) // HEAD_DIM
SB_KV_HEADS = SB_HEADS // 4
SB_GROUP = SB_HEADS // SB_KV_HEADS
DIFF_HEADS = (D_MODEL // 2) // (2 * HEAD_DIM)
DIFF_KV_HEADS = DIFF_HEADS // 4
DIFF_GROUP = DIFF_HEADS // DIFF_KV_HEADS
SB_WIDTH = SB_HEADS * HEAD_DIM
DIFF_WIDTH = DIFF_HEADS * 2 * HEAD_DIM
MIX_WIDTH = SB_WIDTH + DIFF_WIDTH
SB_Q_COLS = SB_HEADS * HEAD_DIM
SB_K_COLS = SB_KV_HEADS * HEAD_DIM
SB_V_COLS = SB_KV_HEADS * HEAD_DIM
DIFF_Q_COLS = DIFF_HEADS * 2 * HEAD_DIM
DIFF_K_COLS = DIFF_KV_HEADS * 2 * HEAD_DIM
DIFF_V_COLS = DIFF_KV_HEADS * 2 * HEAD_DIM
IN_COLS = SB_Q_COLS + SB_K_COLS + SB_V_COLS + DIFF_Q_COLS + DIFF_K_COLS + DIFF_V_COLS
N_BUCKETS = 32
MAX_EXACT = N_BUCKETS // 2
MAX_DISTANCE = 128
N_EXPERTS = 32
TOP_K = 4
D_FF = D_MODEL
SWIGLU_LIMIT = 7.0
SWIGLU_ALPHA = 1.702
Q_BLOCK = 128
EXPERT_BLOCK = 128
EPS = 1e-5

kernel_name = 'stickbreak_diffattn_moe_hybrid_step'


def rms_norm(x, gain):
    xf = x.astype(jnp.float32)
    y = xf * lax.rsqrt(jnp.mean(xf * xf, axis=-1, keepdims=True) + EPS)
    return (y * gain.astype(jnp.float32)).astype(x.dtype)


def t5_bucket(q_pos, k_pos):
    n = jnp.maximum(q_pos[:, None] - k_pos[None, :], 0)
    nf = jnp.maximum(n, MAX_EXACT).astype(jnp.float32)
    large = MAX_EXACT + (jnp.log(nf / MAX_EXACT) / math.log(MAX_DISTANCE / MAX_EXACT)
                         * (N_BUCKETS - MAX_EXACT)).astype(jnp.int32)
    large = jnp.minimum(large, N_BUCKETS - 1)
    return jnp.where(n < MAX_EXACT, n, large)


def stick_breaking_attend(q, k, v, q_pos, k_pos):
    z = jnp.einsum('bqhgd,bkhd->bhgqk', q.astype(jnp.float32), k.astype(jnp.float32)) * HEAD_DIM ** -0.5
    visible = k_pos[None, :] < q_pos[:, None]
    log_keep = jnp.where(visible, -jax.nn.softplus(z), 0.0)
    between = lax.cumsum(log_keep, axis=z.ndim - 1, reverse=True) - log_keep
    weights = jnp.where(visible, jnp.exp(jax.nn.log_sigmoid(z) + between), 0.0)
    o = jnp.einsum('bhgqk,bkhd->bqhgd', weights, v.astype(jnp.float32))
    return o.astype(v.dtype)


def diff_attend(q, k, v, q_pos, k_pos, lam, rel_bias):
    s = jnp.einsum('bqhgcd,bkhcd->bhgcqk', q.astype(jnp.float32), k.astype(jnp.float32)) * HEAD_DIM ** -0.5
    bias = rel_bias.astype(jnp.float32)[t5_bucket(q_pos, k_pos)]
    bias = jnp.transpose(bias, (2, 0, 1)).reshape(DIFF_KV_HEADS, DIFF_GROUP, q_pos.shape[0], k_pos.shape[0])
    causal = k_pos[None, :] <= q_pos[:, None]
    s = jnp.where(causal, s + bias[:, :, None], -jnp.inf)
    p = jax.nn.softmax(s, axis=-1)
    attn = p[:, :, :, 0] - lam * p[:, :, :, 1]
    o = jnp.einsum('bhgqk,bkhe->bqhge', attn, v.astype(jnp.float32))
    return o.astype(v.dtype)


def sweep_query_blocks(attend, q, k, v, q_pos, k_pos):
    n_q = q.shape[1]
    if n_q <= Q_BLOCK:
        return attend(q, k, v, q_pos, k_pos)
    offset = k.shape[1] - n_q
    outs = []
    for i in range(-(-n_q // Q_BLOCK)):
        lo, hi = i * Q_BLOCK, min((i + 1) * Q_BLOCK, n_q)
        end = offset + hi
        outs.append(attend(q[:, lo:hi], k[:, :end], v[:, :end], q_pos[lo:hi], k_pos[:end]))
    return jnp.concatenate(outs, axis=1)


def expert_ffn(xb, wg, bg, wu, bu, wd, bd):
    gate = xb @ wg + bg
    up = xb @ wu + bu
    gate = jnp.minimum(gate, SWIGLU_LIMIT)
    up = jnp.clip(up, -SWIGLU_LIMIT, SWIGLU_LIMIT)
    glu = gate * jax.nn.sigmoid(SWIGLU_ALPHA * gate)
    return ((up + 1.0) * glu) @ wd + bd


def moe_ffn(h, w_router, b_router, w_gate, b_gate, w_up, b_up, w_down, b_down):
    n_tok, d = h.shape
    logits = h.astype(jnp.float32) @ w_router.astype(jnp.float32) + b_router.astype(jnp.float32)
    top_val, top_idx = lax.top_k(logits, TOP_K)
    gates = jax.nn.softmax(top_val, axis=-1)
    flat_expert = top_idx.reshape(-1)
    flat_token = jnp.repeat(jnp.arange(n_tok, dtype=jnp.int32), TOP_K)
    flat_gate = gates.reshape(-1)
    order = jnp.argsort(flat_expert)
    sorted_expert = flat_expert[order]
    counts = jnp.bincount(flat_expert, length=N_EXPERTS)
    padded = (counts + EXPERT_BLOCK - 1) // EXPERT_BLOCK * EXPERT_BLOCK
    padded_end = jnp.cumsum(padded)
    padded_start = padded_end - padded
    group_start = jnp.cumsum(counts) - counts
    rank = jnp.arange(n_tok * TOP_K, dtype=jnp.int32) - group_start[sorted_expert]
    row = padded_start[sorted_expert] + rank
    n_blocks = -(-(n_tok * TOP_K + N_EXPERTS * (EXPERT_BLOCK - 1)) // EXPERT_BLOCK)
    n_rows = n_blocks * EXPERT_BLOCK
    row_token = jnp.full((n_rows,), n_tok, jnp.int32).at[row].set(flat_token[order])
    row_gate = jnp.zeros((n_rows,), jnp.float32).at[row].set(flat_gate[order])
    block_start = jnp.arange(n_blocks, dtype=padded_end.dtype) * EXPERT_BLOCK
    block_expert = jnp.clip(jnp.searchsorted(padded_end, block_start, side='right'), 0, N_EXPERTS - 1)
    h_pad = jnp.concatenate([h, jnp.zeros((1, d), h.dtype)], axis=0)
    x_rows = h_pad[row_token].reshape(n_blocks, EXPERT_BLOCK, d)

    def run_block(args):
        xb, e = args
        return expert_ffn(xb, w_gate[e], b_gate[e], w_up[e], b_up[e], w_down[e], b_down[e])

    y_rows = lax.map(run_block, (x_rows, block_expert)).reshape(n_rows, d)
    y = jax.ops.segment_sum(y_rows.astype(jnp.float32) * row_gate[:, None], row_token,
                            num_segments=n_tok + 1)[:n_tok]
    return y.astype(h.dtype)


def gather_pages(pool, page_table):
    g = pool[page_table]
    return g.reshape((page_table.shape[0], page_table.shape[1] * PAGE_SIZE) + pool.shape[2:])


def hybrid_layer(x, q_pos, past, attn_norm, w_in, diff_q_norm, diff_k_norm, lam_q1, lam_k1, lam_q2, lam_k2,
                 rel_bias, sb_out_norm, diff_subln, w_out, ffn_norm, w_router, b_router,
                 w_gate, b_gate, w_up, b_up, w_down, b_down, layer_idx):
    b, t, d = x.shape
    h = rms_norm(x, attn_norm)
    proj = h @ w_in
    splits = [SB_Q_COLS, SB_Q_COLS + SB_K_COLS, SB_Q_COLS + SB_K_COLS + SB_V_COLS,
              SB_Q_COLS + SB_K_COLS + SB_V_COLS + DIFF_Q_COLS,
              SB_Q_COLS + SB_K_COLS + SB_V_COLS + DIFF_Q_COLS + DIFF_K_COLS]
    sb_q, sb_k, sb_v, d_q, d_k, d_v = jnp.split(proj, splits, axis=-1)
    sb_q = sb_q.reshape(b, t, SB_KV_HEADS, SB_GROUP, HEAD_DIM)
    sb_k = sb_k.reshape(b, t, SB_KV_HEADS, HEAD_DIM)
    sb_v = sb_v.reshape(b, t, SB_KV_HEADS, HEAD_DIM)
    d_q = rms_norm(d_q.reshape(b, t, DIFF_KV_HEADS, DIFF_GROUP, 2, HEAD_DIM), diff_q_norm)
    d_k = rms_norm(d_k.reshape(b, t, DIFF_KV_HEADS, 2, HEAD_DIM), diff_k_norm)
    d_v = d_v.reshape(b, t, DIFF_KV_HEADS, 2 * HEAD_DIM)

    if past is None:
        all_sb_k, all_sb_v, all_d_k, all_d_v = sb_k, sb_v, d_k, d_v
        k_pos = q_pos
    else:
        p_sb_k, p_sb_v, p_d_k, p_d_v = past
        all_sb_k = jnp.concatenate([p_sb_k, sb_k], axis=1)
        all_sb_v = jnp.concatenate([p_sb_v, sb_v], axis=1)
        all_d_k = jnp.concatenate([p_d_k, d_k], axis=1)
        all_d_v = jnp.concatenate([p_d_v, d_v], axis=1)
        k_pos = jnp.concatenate([jnp.arange(p_sb_k.shape[1], dtype=jnp.int32), q_pos])

    sb_o = sweep_query_blocks(stick_breaking_attend, sb_q, all_sb_k, all_sb_v, q_pos, k_pos)
    sb_o = rms_norm(sb_o.reshape(b, t, SB_HEADS, HEAD_DIM), sb_out_norm)

    lambda_init = 0.8 - 0.6 * math.exp(-0.3 * layer_idx)
    lam = (jnp.exp(jnp.sum(lam_q1.astype(jnp.float32) * lam_k1.astype(jnp.float32)))
           - jnp.exp(jnp.sum(lam_q2.astype(jnp.float32) * lam_k2.astype(jnp.float32))) + lambda_init)
    attend = functools.partial(diff_attend, lam=lam, rel_bias=rel_bias)
    d_o = sweep_query_blocks(attend, d_q, all_d_k, all_d_v, q_pos, k_pos)
    d_o = rms_norm(d_o.reshape(b, t, DIFF_HEADS, 2 * HEAD_DIM), diff_subln) * (1.0 - lambda_init)

    mix = jnp.concatenate([sb_o.reshape(b, t, SB_WIDTH), d_o.reshape(b, t, DIFF_WIDTH)], axis=-1)
    x = x + mix @ w_out
    h2 = rms_norm(x, ffn_norm).reshape(b * t, d)
    x = x + moe_ffn(h2, w_router, b_router, w_gate, b_gate, w_up, b_up, w_down, b_down).reshape(b, t, d)
    return x, (sb_k, sb_v, d_k, d_v)


def setup_inputs(seed: int = 0) -> dict:
    key = jax.random.key(seed)
    ks = jax.random.split(key, 32)
    n_pages = PAST_LEN // PAGE_SIZE
    used = DEC_BATCH * n_pages
    n_pool = used + max(1, used // 4)

    def normal(k, shape, scale):
        return scale * jax.random.normal(k, shape, jnp.float32)

    def gain(k, shape):
        return 1.0 + 0.02 * jax.random.normal(k, shape, jnp.float32)

    page_table = jax.random.permutation(ks[6], n_pool)[:used].reshape(DEC_BATCH, n_pages).astype(jnp.int32)
    return {
        'x_prompt': normal(ks[0], (BATCH, SEQ, D_MODEL), 1.0),
        'x_sample': normal(ks[1], (DEC_BATCH, DEC_SEQ, D_MODEL), 1.0),
        'cache_sb_k': normal(ks[2], (DEPTH, n_pool, PAGE_SIZE, SB_KV_HEADS, HEAD_DIM), 1.0),
        'cache_sb_v': normal(ks[3], (DEPTH, n_pool, PAGE_SIZE, SB_KV_HEADS, HEAD_DIM), 1.0),
        'cache_diff_k': normal(ks[4], (DEPTH, n_pool, PAGE_SIZE, DIFF_KV_HEADS, 2, HEAD_DIM), 1.0),
        'cache_diff_v': normal(ks[5], (DEPTH, n_pool, PAGE_SIZE, DIFF_KV_HEADS, 2 * HEAD_DIM), 1.0),
        'page_table': page_table,
        'attn_norm': gain(ks[7], (DEPTH, D_MODEL)),
        'w_in': normal(ks[8], (DEPTH, D_MODEL, IN_COLS), D_MODEL ** -0.5),
        'diff_q_norm': gain(ks[9], (DEPTH, HEAD_DIM)),
        'diff_k_norm': gain(ks[10], (DEPTH, HEAD_DIM)),
        'diff_lambda_q1': normal(ks[11], (DEPTH, HEAD_DIM), 0.1),
        'diff_lambda_k1': normal(ks[12], (DEPTH, HEAD_DIM), 0.1),
        'diff_lambda_q2': normal(ks[13], (DEPTH, HEAD_DIM), 0.1),
        'diff_lambda_k2': normal(ks[14], (DEPTH, HEAD_DIM), 0.1),
        'rel_bias': normal(ks[15], (N_BUCKETS, DIFF_HEADS), 0.5),
        'sb_out_norm': gain(ks[16], (DEPTH, SB_HEADS, HEAD_DIM)),
        'diff_subln': gain(ks[17], (DEPTH, 2 * HEAD_DIM)),
        'w_out': normal(ks[18], (DEPTH, MIX_WIDTH, D_MODEL), MIX_WIDTH ** -0.5),
        'ffn_norm': gain(ks[19], (DEPTH, D_MODEL)),
        'w_router': normal(ks[20], (DEPTH, D_MODEL, N_EXPERTS), D_MODEL ** -0.5),
        'b_router': normal(ks[21], (DEPTH, N_EXPERTS), 0.01),
        'w_gate': normal(ks[22], (DEPTH, N_EXPERTS, D_MODEL, D_FF), D_MODEL ** -0.5),
        'b_gate': normal(ks[23], (DEPTH, N_EXPERTS, D_FF), 0.02),
        'w_up': normal(ks[24], (DEPTH, N_EXPERTS, D_MODEL, D_FF), D_MODEL ** -0.5),
        'b_up': normal(ks[25], (DEPTH, N_EXPERTS, D_FF), 0.02),
        'w_down': normal(ks[26], (DEPTH, N_EXPERTS, D_FF, D_MODEL), D_FF ** -0.5),
        'b_down': normal(ks[27], (DEPTH, N_EXPERTS, D_MODEL), 0.02),
    }


def reference(x_prompt, x_sample, cache_sb_k, cache_sb_v, cache_diff_k, cache_diff_v, page_table,
              attn_norm, w_in, diff_q_norm, diff_k_norm, diff_lambda_q1, diff_lambda_k1, diff_lambda_q2,
              diff_lambda_k2, rel_bias, sb_out_norm, diff_subln, w_out, ffn_norm, w_router, b_router,
              w_gate, b_gate, w_up, b_up, w_down, b_down):
    past_len = page_table.shape[1] * PAGE_SIZE
    pos_prompt = jnp.arange(x_prompt.shape[1], dtype=jnp.int32)
    pos_sample = past_len + jnp.arange(x_sample.shape[1], dtype=jnp.int32)
    y_p, y_s = x_prompt, x_sample
    rows_p, rows_s = [], []
    for l in range(DEPTH):
        lw = (attn_norm[l], w_in[l], diff_q_norm[l], diff_k_norm[l], diff_lambda_q1[l], diff_lambda_k1[l],
              diff_lambda_q2[l], diff_lambda_k2[l], rel_bias, sb_out_norm[l], diff_subln[l], w_out[l],
              ffn_norm[l], w_router[l], b_router[l], w_gate[l], b_gate[l], w_up[l], b_up[l],
              w_down[l], b_down[l])
        y_p, new_p = hybrid_layer(y_p, pos_prompt, None, *lw, layer_idx=l)
        past = (gather_pages(cache_sb_k[l], page_table), gather_pages(cache_sb_v[l], page_table),
                gather_pages(cache_diff_k[l], page_table), gather_pages(cache_diff_v[l], page_table))
        y_s, new_s = hybrid_layer(y_s, pos_sample, past, *lw, layer_idx=l)
        rows_p.append(new_p)
        rows_s.append(new_s)
    return (y_p, y_s,
            jnp.stack([r[0] for r in rows_p]), jnp.stack([r[1] for r in rows_p]),
            jnp.stack([r[2] for r in rows_p]), jnp.stack([r[3] for r in rows_p]),
            jnp.stack([r[0] for r in rows_s]), jnp.stack([r[1] for r in rows_s]),
            jnp.stack([r[2] for r in rows_s]), jnp.stack([r[3] for r in rows_s]))
```

```python
import functools
import math

import numpy as np
import jax
import jax.numpy as jnp
from jax import lax
from jax.experimental import pallas as pl
from jax.experimental.pallas import tpu as pltpu

F32 = jnp.float32
BF16 = jnp.bfloat16
HIGHEST = lax.Precision.HIGHEST

HEAD_DIM = 128
GROUP = 4
N_BUCKETS = 32
MAX_EXACT = N_BUCKETS // 2
MAX_DISTANCE = 128
TOP_K = 4
SWIGLU_LIMIT = 7.0
SWIGLU_ALPHA = 1.702
EPS = 1e-5
NEG_BIG = -1e30

V7X_VMEM_LIMIT_BYTES = 56 * 1024 * 1024
LANES = 128
ATTN_BLOCK = 128
PAGES_PER_STEP = 8
MOE_SUB = 256
MOE_CHUNK = 5 * MOE_SUB
MOE_TN = 256
COMBINE_TOKENS = 64


def _cparams(n_axes):
    return pltpu.CompilerParams(dimension_semantics=("arbitrary",) * n_axes,
                                vmem_limit_bytes=V7X_VMEM_LIMIT_BYTES)


def _nt_dot(a, b, precision=None):
    return lax.dot_general(a, b, (((1,), (1,)), ((), ())), precision=precision,
                           preferred_element_type=F32)


def _rms(x, gain):
    ms = jnp.mean(x * x, axis=-1, keepdims=True)
    return x * lax.rsqrt(ms + EPS) * gain


def _bucket_thresholds():
    n = np.arange(MAX_DISTANCE + 1)
    nf = np.maximum(n, MAX_EXACT).astype(np.float32)
    large = MAX_EXACT + (np.log(nf / np.float32(MAX_EXACT)) / np.float32(math.log(MAX_DISTANCE / MAX_EXACT))
                         * np.float32(N_BUCKETS - MAX_EXACT)).astype(np.int32)
    large = np.minimum(large, N_BUCKETS - 1)
    bucket = np.where(n < MAX_EXACT, n, large)
    return [int(np.argmax(bucket >= k)) for k in range(1, N_BUCKETS)]


_BUCKET_THR = _bucket_thresholds()


def _rmsnorm_kernel(x_ref, g_ref, o_ref):
    o_ref[...] = _rms(x_ref[...], g_ref[...]).astype(o_ref.dtype)


def _rmsnorm(x, gain, out_dtype, tm):
    t, d = x.shape
    return pl.pallas_call(
        _rmsnorm_kernel,
        out_shape=jax.ShapeDtypeStruct((t, d), out_dtype),
        grid=(t // tm,),
        in_specs=[pl.BlockSpec((tm, d), lambda i: (i, 0)), pl.BlockSpec((1, d), lambda i: (0, 0))],
        out_specs=pl.BlockSpec((tm, d), lambda i: (i, 0)),
        compiler_params=_cparams(1), name="rmsnorm",
    )(x, gain.reshape(1, d))


def _matmul_kernel(a_ref, w_ref, e_ref, o_ref, *scratch, high, mode, norm_lo, norm_hi, tn):
    j = pl.program_id(0)
    i = pl.program_id(1)
    if high:
        acc = jnp.dot(a_ref[...], w_ref[...], precision=HIGHEST, preferred_element_type=F32)
    else:
        wb_ref, = scratch

        @pl.when(i == 0)
        def _():
            wb_ref[...] = w_ref[...].astype(BF16)

        acc = jnp.dot(a_ref[...], wb_ref[...], preferred_element_type=F32)
    if mode == "residual":
        o_ref[...] = e_ref[...] + acc
    else:
        in_range = jnp.logical_and(j >= norm_lo, j < norm_hi)

        @pl.when(in_range)
        def _():
            for c in range(tn // LANES):
                sl = slice(c * LANES, (c + 1) * LANES)
                o_ref[:, sl] = _rms(acc[:, sl], e_ref[:, sl])

        @pl.when(jnp.logical_not(in_range))
        def _():
            o_ref[...] = acc


def _matmul(a, w, extra, *, high, mode, tm, tn, norm_lo=0, norm_hi=0):
    m, k = a.shape
    n = w.shape[1]
    if mode == "residual":
        e_spec = pl.BlockSpec((tm, tn), lambda j, i: (i, j))
    else:
        e_spec = pl.BlockSpec((1, tn), lambda j, i: (0, j))
    kern = functools.partial(_matmul_kernel, high=high, mode=mode, norm_lo=norm_lo, norm_hi=norm_hi, tn=tn)
    return pl.pallas_call(
        kern,
        out_shape=jax.ShapeDtypeStruct((m, n), F32),
        grid=(n // tn, m // tm),
        in_specs=[pl.BlockSpec((tm, k), lambda j, i: (i, 0)),
                  pl.BlockSpec((k, tn), lambda j, i: (0, j)),
                  e_spec],
        out_specs=pl.BlockSpec((tm, tn), lambda j, i: (i, j)),
        scratch_shapes=[] if high else [pltpu.VMEM((k, tn), BF16)],
        compiler_params=_cparams(2), name="matmul_" + mode + ("_hp" if high else ""),
    )(a, w, extra)


def _softplus(z):
    return jnp.maximum(z, 0.0) + jnp.log(1.0 + jnp.exp(-jnp.abs(z)))


def _sb_prompt_kernel(q_ref, k_ref, v_ref, g_ref, o_ref, acc_ref, *, bq, scale):
    qi = pl.program_id(2)
    rows = GROUP * bq
    q = jnp.concatenate([q_ref[0, :, g * HEAD_DIM:(g + 1) * HEAD_DIM] for g in range(GROUP)], axis=0)
    qs = (q * scale).astype(BF16)
    row_i = lax.broadcasted_iota(jnp.int32, (bq, bq), 0)
    col_i = lax.broadcasted_iota(jnp.int32, (bq, bq), 1)
    later = jnp.where(row_i > col_i, 1.0, 0.0).astype(BF16)
    visible = jnp.concatenate([col_i < row_i] * GROUP, axis=0)

    def block(kb, carry, masked):
        start = pl.multiple_of(kb * bq, bq)
        k = k_ref[0, pl.ds(start, bq), :].astype(BF16)
        v = v_ref[0, pl.ds(start, bq), :].astype(BF16)
        z = _nt_dot(qs, k)
        sp = _softplus(z)
        log_keep = jnp.where(visible, -sp, 0.0) if masked else -sp
        hi = log_keep.astype(BF16)
        lo = (log_keep - hi.astype(F32)).astype(BF16)
        between = (jnp.dot(hi, later, preferred_element_type=F32)
                   + jnp.dot(lo, later, preferred_element_type=F32)) + carry
        w = jnp.exp(z - sp + between)
        if masked:
            w = jnp.where(visible, w, 0.0)
        acc_ref[...] += jnp.dot(w.astype(BF16), v, preferred_element_type=F32)
        return carry + jnp.sum(log_keep, axis=-1, keepdims=True)

    acc_ref[...] = jnp.zeros_like(acc_ref)
    carry = block(qi, jnp.zeros((rows, 1), F32), True)
    lax.fori_loop(0, qi, lambda t, c: block(qi - 1 - t, c, False), carry)

    acc = acc_ref[...]
    for g in range(GROUP):
        sl = slice(g * HEAD_DIM, (g + 1) * HEAD_DIM)
        o_ref[0, :, sl] = _rms(acc[g * bq:(g + 1) * bq, :], g_ref[:, sl]).astype(o_ref.dtype)


def _sb_prompt(proj3, gain, *, kv_heads, q_off, k_off, v_off):
    b, t, _ = proj3.shape
    bq = ATTN_BLOCK
    qw = GROUP * HEAD_DIM
    width = kv_heads * qw
    assert q_off % qw == 0 and k_off % HEAD_DIM == 0 and v_off % HEAD_DIM == 0 and t % bq == 0
    kern = functools.partial(_sb_prompt_kernel, bq=bq, scale=HEAD_DIM ** -0.5)
    return pl.pallas_call(
        kern,
        out_shape=jax.ShapeDtypeStruct((b, t, width), BF16),
        grid=(b, kv_heads, t // bq),
        in_specs=[pl.BlockSpec((1, bq, qw), lambda bi, h, qi: (bi, qi, q_off // qw + h)),
                  pl.BlockSpec((1, t, HEAD_DIM), lambda bi, h, qi: (bi, 0, k_off // HEAD_DIM + h)),
                  pl.BlockSpec((1, t, HEAD_DIM), lambda bi, h, qi: (bi, 0, v_off // HEAD_DIM + h)),
                  pl.BlockSpec((1, qw), lambda bi, h, qi: (0, h))],
        out_specs=pl.BlockSpec((1, bq, qw), lambda bi, h, qi: (bi, qi, h)),
        scratch_shapes=[pltpu.VMEM((GROUP * bq, HEAD_DIM), F32)],
        compiler_params=_cparams(3), name="sb_prompt",
    )(proj3, proj3, proj3, gain.reshape(1, width))


def _bias_from_distance(n, table):
    bias = table(0)
    for k in range(1, N_BUCKETS):
        bias = jnp.where(n >= _BUCKET_THR[k - 1], table(k), bias)
    return bias


def _diff_prompt_kernel(rb_ref, lam_ref, q_ref, k_ref, v_ref, sub_ref, o_ref,
                        tiles_ref, m_ref, l_ref, acc_ref, *, bq, scale, kv_heads, out_scale):
    bi = pl.program_id(0)
    h = pl.program_id(1)
    qi = pl.program_id(2)
    rows = GROUP * bq
    vw = 2 * HEAD_DIM
    row_i = lax.broadcasted_iota(jnp.int32, (bq, bq), 0)
    col_i = lax.broadcasted_iota(jnp.int32, (bq, bq), 1)

    @pl.when(jnp.logical_and(jnp.logical_and(bi == 0, h == 0), qi == 0))
    def _():
        for kind in range(2):
            n = jnp.maximum(row_i - col_i + kind * bq, 0)
            for hh in range(kv_heads):
                for g in range(GROUP):
                    head = hh * GROUP + g
                    tiles_ref[kind, hh, g * bq:(g + 1) * bq, :] = _bias_from_distance(
                        n, lambda k, head=head: rb_ref[k, head])

    far_bias = jnp.concatenate(
        [jnp.full((bq, 1), rb_ref[N_BUCKETS - 1, h * GROUP + g], F32) for g in range(GROUP)], axis=0)
    causal = jnp.concatenate([col_i <= row_i] * GROUP, axis=0)
    qs = []
    for c in range(2):
        qc = jnp.concatenate([q_ref[0, :, (2 * g + c) * HEAD_DIM:(2 * g + c + 1) * HEAD_DIM]
                              for g in range(GROUP)], axis=0)
        qs.append((qc * scale).astype(BF16))

    m_ref[...] = jnp.full_like(m_ref, NEG_BIG)
    l_ref[...] = jnp.zeros_like(l_ref)
    acc_ref[...] = jnp.zeros_like(acc_ref)

    def step(kb, bias, masked):
        start = pl.multiple_of(kb * bq, bq)
        ks = k_ref[0, pl.ds(start, bq), :]
        v = v_ref[0, pl.ds(start, bq), :].astype(BF16)
        for c in range(2):
            k = ks[:, c * HEAD_DIM:(c + 1) * HEAD_DIM].astype(BF16)
            s = _nt_dot(qs[c], k) + bias
            if masked:
                s = jnp.where(causal, s, NEG_BIG)
            m_old = m_ref[c]
            m_new = jnp.maximum(m_old, jnp.max(s, axis=-1, keepdims=True))
            alpha = jnp.exp(m_old - m_new)
            p = jnp.exp(s - m_new)
            l_ref[c] = alpha * l_ref[c] + jnp.sum(p, axis=-1, keepdims=True)
            acc_ref[c] = alpha * acc_ref[c] + jnp.dot(p.astype(BF16), v, preferred_element_type=F32)
            m_ref[c] = m_new

    step(qi, tiles_ref[0, h], True)

    @pl.when(qi >= 1)
    def _():
        step(qi - 1, tiles_ref[1, h], False)

    def far(kb, carry):
        step(kb, far_bias, False)
        return carry

    lax.fori_loop(0, qi - 1, far, 0)

    lam = lam_ref[0]
    o = acc_ref[0] / l_ref[0] - lam * (acc_ref[1] / l_ref[1])
    for g in range(GROUP):
        o_ref[0, :, g * vw:(g + 1) * vw] = (_rms(o[g * bq:(g + 1) * bq, :], sub_ref[...]) * out_scale
                                            ).astype(o_ref.dtype)


def _diff_prompt(proj3, rel_bias, lam, subln, *, kv_heads, q_off, k_off, v_off, out_scale):
    b, t, _ = proj3.shape
    bq = ATTN_BLOCK
    qw = GROUP * 2 * HEAD_DIM
    vw = 2 * HEAD_DIM
    width = kv_heads * GROUP * vw
    assert q_off % qw == 0 and k_off % vw == 0 and v_off % vw == 0 and t % bq == 0
    assert bq + 1 >= _BUCKET_THR[-1]
    kern = functools.partial(_diff_prompt_kernel, bq=bq, scale=HEAD_DIM ** -0.5, kv_heads=kv_heads,
                             out_scale=out_scale)
    smem = pl.BlockSpec(memory_space=pltpu.SMEM)
    return pl.pallas_call(
        kern,
        out_shape=jax.ShapeDtypeStruct((b, t, width), BF16),
        grid=(b, kv_heads, t // bq),
        in_specs=[smem, smem,
                  pl.BlockSpec((1, bq, qw), lambda bi, h, qi: (bi, qi, q_off // qw + h)),
                  pl.BlockSpec((1, t, vw), lambda bi, h, qi: (bi, 0, k_off // vw + h)),
                  pl.BlockSpec((1, t, vw), lambda bi, h, qi: (bi, 0, v_off // vw + h)),
                  pl.BlockSpec((1, vw), lambda bi, h, qi: (0, 0))],
        out_specs=pl.BlockSpec((1, bq, GROUP * vw), lambda bi, h, qi: (bi, qi, h)),
        scratch_shapes=[pltpu.VMEM((2, kv_heads, GROUP * bq, bq), F32),
                        pltpu.VMEM((2, GROUP * bq, 1), F32),
                        pltpu.VMEM((2, GROUP * bq, 1), F32),
                        pltpu.VMEM((2, GROUP * bq, vw), F32)],
        compiler_params=_cparams(3), name="diff_prompt",
    )(rel_bias, lam.reshape(1), proj3, proj3, proj3, subln.reshape(1, vw))


def _block_diag(q, n_blocks, rows_per_block=None, block_of_row=None):
    r = q.shape[0]
    row = lax.broadcasted_iota(jnp.int32, (r, HEAD_DIM), 0)
    blk = block_of_row(row)
    return jnp.concatenate([jnp.where(blk == hb, q, 0.0) for hb in range(n_blocks)], axis=1)


def _page_specs(n_pages, n_steps, width, reverse):
    specs = []
    for n in range(PAGES_PER_STEP):
        def imap(b, s, pt, n=n):
            step = (n_steps - 1 - s) if reverse else s
            return (pt[b * n_pages + step * PAGES_PER_STEP + n], 0, 0)
        specs.append(pl.BlockSpec((1, LANES, width), imap))
    return specs


def _sb_decode_kernel(pt_ref, q_ref, *refs, kv_heads, scale):
    k_refs = refs[:PAGES_PER_STEP]
    v_refs = refs[PAGES_PER_STEP:2 * PAGES_PER_STEP]
    g_ref, o_ref, acc_ref, carry_ref = refs[2 * PAGES_PER_STEP:]
    s = pl.program_id(1)
    heads = kv_heads * GROUP
    width = PAGES_PER_STEP * LANES

    @pl.when(s == 0)
    def _():
        acc_ref[...] = jnp.zeros_like(acc_ref)
        carry_ref[...] = jnp.zeros_like(carry_ref)

    q_bd = _block_diag(q_ref[0] * scale, kv_heads, block_of_row=lambda r: r // GROUP)
    z = jnp.concatenate([_nt_dot(q_bd, k_refs[p][0], HIGHEST) for p in range(PAGES_PER_STEP)], axis=1)
    sp = _softplus(z)
    log_keep = -sp
    lane = lax.broadcasted_iota(jnp.int32, (heads, width), 1) % LANES
    suffix = log_keep
    sh = 1
    while sh < LANES:
        suffix = suffix + jnp.where(lane + sh < LANES, pltpu.roll(suffix, width - sh, 1), 0.0)
        sh *= 2
    run = carry_ref[...]
    offsets = [None] * PAGES_PER_STEP
    for p in reversed(range(PAGES_PER_STEP)):
        offsets[p] = run
        run = run + jnp.broadcast_to(suffix[:, p * LANES:p * LANES + 1], (heads, LANES))
    carry_ref[...] = run
    between = suffix - log_keep + jnp.concatenate(offsets, axis=1)
    w = jnp.exp(z - sp + between)
    o = acc_ref[...]
    for p in range(PAGES_PER_STEP):
        o = o + jnp.dot(w[:, p * LANES:(p + 1) * LANES], v_refs[p][0], precision=HIGHEST,
                        preferred_element_type=F32)
    acc_ref[...] = o

    @pl.when(s == pl.num_programs(1) - 1)
    def _():
        row = lax.broadcasted_iota(jnp.int32, (heads, HEAD_DIM), 0)
        out = jnp.zeros((heads, HEAD_DIM), F32)
        for hb in range(kv_heads):
            out = out + jnp.where(row // GROUP == hb, o[:, hb * HEAD_DIM:(hb + 1) * HEAD_DIM], 0.0)
        o_ref[0] = _rms(out, g_ref[...])


def _sb_decode(q, cache_k, cache_v, page_table, gain, *, kv_heads):
    db, heads, _ = q.shape
    n_pages = page_table.shape[1]
    assert n_pages % PAGES_PER_STEP == 0
    n_steps = n_pages // PAGES_PER_STEP
    width = kv_heads * HEAD_DIM
    kern = functools.partial(_sb_decode_kernel, kv_heads=kv_heads, scale=HEAD_DIM ** -0.5)
    pages = _page_specs(n_pages, n_steps, width, reverse=True)
    grid_spec = pltpu.PrefetchScalarGridSpec(
        num_scalar_prefetch=1, grid=(db, n_steps),
        in_specs=[pl.BlockSpec((1, heads, HEAD_DIM), lambda b, s, pt: (b, 0, 0))] + pages + pages
                 + [pl.BlockSpec((heads, HEAD_DIM), lambda b, s, pt: (0, 0))],
        out_specs=pl.BlockSpec((1, heads, HEAD_DIM), lambda b, s, pt: (b, 0, 0)),
        scratch_shapes=[pltpu.VMEM((heads, width), F32), pltpu.VMEM((heads, LANES), F32)])
    return pl.pallas_call(
        kern, out_shape=jax.ShapeDtypeStruct((db, heads, HEAD_DIM), F32), grid_spec=grid_spec,
        compiler_params=_cparams(2), name="sb_decode",
    )(page_table.reshape(-1), q, *([cache_k] * PAGES_PER_STEP), *([cache_v] * PAGES_PER_STEP), gain)


def _diff_decode_kernel(pt_ref, lam_ref, q_ref, kn_ref, vn_ref, rbt_ref, *refs, kv_heads, scale, out_scale):
    k_refs = refs[:PAGES_PER_STEP]
    v_refs = refs[PAGES_PER_STEP:2 * PAGES_PER_STEP]
    sub_ref, o_ref, m_ref, l_ref, acc_ref = refs[2 * PAGES_PER_STEP:]
    s = pl.program_id(1)
    last = pl.num_programs(1) - 1
    heads = kv_heads * GROUP
    rows = 2 * heads
    width = PAGES_PER_STEP * LANES
    vw = 2 * HEAD_DIM

    @pl.when(s == 0)
    def _():
        m_ref[...] = jnp.full_like(m_ref, NEG_BIG)
        l_ref[...] = jnp.zeros_like(l_ref)
        acc_ref[...] = jnp.zeros_like(acc_ref)

    q_bd = _block_diag(q_ref[0] * scale, 2 * kv_heads,
                       block_of_row=lambda r: 2 * ((r % heads) // GROUP) + r // heads)
    rbt = rbt_ref[...]
    far = jnp.broadcast_to(rbt[:, N_BUCKETS - 1:N_BUCKETS], (rows, LANES))

    def attend(bias_last_page):
        logits = []
        for p in range(PAGES_PER_STEP):
            bias = bias_last_page if (p == PAGES_PER_STEP - 1 and bias_last_page is not None) else far
            logits.append(_nt_dot(q_bd, k_refs[p][0], HIGHEST) + bias)
        sc = jnp.concatenate(logits, axis=1)
        m_old = m_ref[...]
        m_new = jnp.maximum(m_old, jnp.broadcast_to(jnp.max(sc, axis=-1, keepdims=True), m_old.shape))
        alpha = jnp.exp(m_old - m_new)
        p_all = jnp.exp(sc - jnp.concatenate([m_new] * PAGES_PER_STEP, axis=1))
        l_ref[...] = alpha * l_ref[...] + jnp.broadcast_to(jnp.sum(p_all, axis=-1, keepdims=True), m_old.shape)
        o = jnp.concatenate([alpha] * (acc_ref.shape[1] // LANES), axis=1) * acc_ref[...]
        for p in range(PAGES_PER_STEP):
            o = o + jnp.dot(p_all[:, p * LANES:(p + 1) * LANES], v_refs[p][0], precision=HIGHEST,
                            preferred_element_type=F32)
        acc_ref[...] = o
        m_ref[...] = m_new

    @pl.when(s != last)
    def _():
        attend(None)

    @pl.when(s == last)
    def _():
        n = LANES - lax.broadcasted_iota(jnp.int32, (rows, LANES), 1)
        bias_last = _bias_from_distance(
            n, lambda k: jnp.broadcast_to(rbt[:, k:k + 1], (rows, LANES)))
        attend(bias_last)
        s_self = jnp.sum(q_bd * kn_ref[0], axis=-1, keepdims=True) + rbt[:, 0:1]
        s_self = jnp.broadcast_to(s_self, (rows, LANES))
        m_old = m_ref[...]
        m_new = jnp.maximum(m_old, s_self)
        alpha = jnp.exp(m_old - m_new)
        p_self = jnp.exp(s_self - m_new)
        l_fin = alpha * l_ref[...] + p_self
        n_rep = acc_ref.shape[1] // LANES
        acc = (jnp.concatenate([alpha] * n_rep, axis=1) * acc_ref[...]
               + jnp.concatenate([p_self] * n_rep, axis=1) * vn_ref[0])
        acc = acc / jnp.concatenate([l_fin] * n_rep, axis=1)
        row = lax.broadcasted_iota(jnp.int32, (rows, vw), 0)
        sel = jnp.zeros((rows, vw), F32)
        for hb in range(kv_heads):
            sel = sel + jnp.where((row % heads) // GROUP == hb, acc[:, hb * vw:(hb + 1) * vw], 0.0)
        out = sel[:heads] - lam_ref[0] * sel[heads:]
        o_ref[0] = _rms(out, sub_ref[...]) * out_scale


def _diff_decode(q, k_new, v_new, cache_k, cache_v, page_table, rbt, lam, subln, *, kv_heads, out_scale):
    db, rows, _ = q.shape
    heads = rows // 2
    n_pages = page_table.shape[1]
    assert n_pages % PAGES_PER_STEP == 0 and LANES >= MAX_DISTANCE
    n_steps = n_pages // PAGES_PER_STEP
    width = kv_heads * 2 * HEAD_DIM
    vw = 2 * HEAD_DIM
    kern = functools.partial(_diff_decode_kernel, kv_heads=kv_heads, scale=HEAD_DIM ** -0.5, out_scale=out_scale)
    pages = _page_specs(n_pages, n_steps, width, reverse=False)
    grid_spec = pltpu.PrefetchScalarGridSpec(
        num_scalar_prefetch=1, grid=(db, n_steps),
        in_specs=[pl.BlockSpec(memory_space=pltpu.SMEM),
                  pl.BlockSpec((1, rows, HEAD_DIM), lambda b, s, pt: (b, 0, 0)),
                  pl.BlockSpec((1, 1, width), lambda b, s, pt: (b, 0, 0)),
                  pl.BlockSpec((1, 1, width), lambda b, s, pt: (b, 0, 0)),
                  pl.BlockSpec((rows, N_BUCKETS), lambda b, s, pt: (0, 0))] + pages + pages
                 + [pl.BlockSpec((1, vw), lambda b, s, pt: (0, 0))],
        out_specs=pl.BlockSpec((1, heads, vw), lambda b, s, pt: (b, 0, 0)),
        scratch_shapes=[pltpu.VMEM((rows, LANES), F32), pltpu.VMEM((rows, LANES), F32),
                        pltpu.VMEM((rows, width), F32)])
    return pl.pallas_call(
        kern, out_shape=jax.ShapeDtypeStruct((db, heads, vw), F32), grid_spec=grid_spec,
        compiler_params=_cparams(2), name="diff_decode",
    )(page_table.reshape(-1), lam.reshape(1), q, k_new, v_new, rbt,
      *([cache_k] * PAGES_PER_STEP), *([cache_v] * PAGES_PER_STEP), subln.reshape(1, vw))


def _router_kernel(x_ref, g_ref, wr_ref, br_ref, h_ref, idx_ref, gate_ref):
    h = _rms(x_ref[...], g_ref[...])
    h_ref[...] = h
    logits = _nt_dot(wr_ref[...], h, HIGHEST) + br_ref[...]
    n_exp = logits.shape[0]
    expert = lax.broadcasted_iota(jnp.int32, logits.shape, 0)
    vals, idxs = [], []
    for _ in range(TOP_K):
        top = jnp.max(logits, axis=0, keepdims=True)
        idx = jnp.min(jnp.where(logits == top, expert, n_exp), axis=0, keepdims=True)
        vals.append(top)
        idxs.append(idx)
        logits = jnp.where(expert == idx, -jnp.inf, logits)
    top_val = jnp.concatenate(vals, axis=0)
    e = jnp.exp(top_val - top_val[0:1])
    gate_ref[...] = e / jnp.sum(e, axis=0, keepdims=True)
    idx_ref[...] = jnp.concatenate(idxs, axis=0)


def _router(x, gain, w_router, b_router, tm):
    t, d = x.shape
    n_exp = w_router.shape[1]
    return pl.pallas_call(
        _router_kernel,
        out_shape=(jax.ShapeDtypeStruct((t, d), F32), jax.ShapeDtypeStruct((TOP_K, t), jnp.int32),
                   jax.ShapeDtypeStruct((TOP_K, t), F32)),
        grid=(t // tm,),
        in_specs=[pl.BlockSpec((tm, d), lambda i: (i, 0)), pl.BlockSpec((1, d), lambda i: (0, 0)),
                  pl.BlockSpec((n_exp, d), lambda i: (0, 0)), pl.BlockSpec((n_exp, 1), lambda i: (0, 0))],
        out_specs=(pl.BlockSpec((tm, d), lambda i: (i, 0)), pl.BlockSpec((TOP_K, tm), lambda i: (0, i)),
                   pl.BlockSpec((TOP_K, tm), lambda i: (0, i))),
        compiler_params=_cparams(1), name="router",
    )(x, gain.reshape(1, d), w_router.T, b_router.reshape(n_exp, 1))


def _moe_plan(top_idx, n_exp, zero_token):
    n = top_idx.size
    n_chunks = n_exp + n // MOE_CHUNK + 1
    sub_per_chunk = MOE_CHUNK // MOE_SUB
    n_gather = n_exp + n // MOE_SUB + 1
    flat_e = top_idx.reshape(-1).astype(jnp.int32)
    counts = jnp.bincount(flat_e, length=n_exp).astype(jnp.int32)
    order = jnp.argsort(flat_e, stable=True).astype(jnp.int32)
    sorted_e = flat_e[order]
    group_start = jnp.cumsum(counts) - counts
    rank = jnp.arange(n, dtype=jnp.int32) - group_start[sorted_e]
    chunks_e = (counts + MOE_CHUNK - 1) // MOE_CHUNK
    chunk_end_e = jnp.cumsum(chunks_e)
    chunk_start_e = chunk_end_e - chunks_e
    dest_sorted = chunk_start_e[sorted_e] * MOE_CHUNK + rank
    dest = jnp.zeros((n,), jnp.int32).at[order].set(dest_sorted)
    n_used = chunk_end_e[-1]
    cidx = jnp.arange(n_chunks, dtype=jnp.int32)
    c_exp = jnp.clip(jnp.searchsorted(chunk_end_e, cidx, side="right"), 0, n_exp - 1).astype(jnp.int32)
    c_cnt = jnp.clip(counts[c_exp] - (cidx - chunk_start_e[c_exp]) * MOE_CHUNK, 0, MOE_CHUNK)
    used = cidx < n_used
    c_exp = jnp.where(used, c_exp, c_exp[n_used - 1])
    c_cnt = jnp.where(used, c_cnt, 0).astype(jnp.int32)
    c_blk = jnp.where(used, cidx, n_used - 1).astype(jnp.int32)
    row_token = jnp.full((n_chunks * MOE_CHUNK,), zero_token, jnp.int32).at[dest_sorted].set(order // TOP_K)
    subs_c = (c_cnt + MOE_SUB - 1) // MOE_SUB
    sub_end = jnp.cumsum(subs_c)
    n_active = sub_end[-1]
    gidx = jnp.arange(n_gather, dtype=jnp.int32)
    g_chunk = jnp.clip(jnp.searchsorted(sub_end, gidx, side="right"), 0, n_chunks - 1).astype(jnp.int32)
    g_dst = g_chunk * sub_per_chunk + (gidx - (sub_end - subs_c)[g_chunk])
    g_dst = jnp.where(gidx < n_active, g_dst, g_dst[n_active - 1]).astype(jnp.int32)
    g_tok = row_token.reshape(-1, MOE_SUB)[g_dst].reshape(-1)
    return dict(n_chunks=n_chunks, dest=dest, c_exp=c_exp, c_cnt=c_cnt, c_blk=c_blk,
                n_used=n_used.reshape(1).astype(jnp.int32), g_dst=g_dst, g_tok=g_tok)


def _row_copy(src_hbm, row, dst_vmem, slot, sem):
    return pltpu.make_async_copy(src_hbm.at[pl.ds(row, 1), :], dst_vmem.at[pl.ds(slot, 1), :], sem)


def _moe_gather_kernel(tok_ref, dst_ref, h_hbm, o_ref, buf, sem):
    base = pl.program_id(0) * MOE_SUB

    def issue(r, carry):
        _row_copy(h_hbm, tok_ref[base + r], buf, r, sem).start()
        return carry

    def drain(r, carry):
        _row_copy(h_hbm, 0, buf, r, sem).wait()
        return carry

    lax.fori_loop(0, MOE_SUB, issue, 0)
    lax.fori_loop(0, MOE_SUB, drain, 0)
    o_ref[...] = buf[...].astype(o_ref.dtype)


def _moe_gather(h_all, plan):
    d = h_all.shape[1]
    n_gather = plan["g_dst"].shape[0]
    grid_spec = pltpu.PrefetchScalarGridSpec(
        num_scalar_prefetch=2, grid=(n_gather,),
        in_specs=[pl.BlockSpec(memory_space=pl.ANY)],
        out_specs=pl.BlockSpec((MOE_SUB, d), lambda i, tok, dst: (dst[i], 0)),
        scratch_shapes=[pltpu.VMEM((MOE_SUB, d), F32), pltpu.SemaphoreType.DMA])
    return pl.pallas_call(
        _moe_gather_kernel, out_shape=jax.ShapeDtypeStruct((plan["n_chunks"] * MOE_CHUNK, d), BF16),
        grid_spec=grid_spec, compiler_params=_cparams(1), name="moe_gather",
    )(plan["g_tok"], plan["g_dst"], h_all)


def _moe_matmul_kernel(ce_ref, cnt_ref, blk_ref, nu_ref, x_ref, *refs, gated):
    if gated:
        wg_ref, bg_ref, wu_ref, bu_ref, o_ref, wgb_ref, wub_ref = refs
    else:
        wg_ref, bg_ref, o_ref, wgb_ref = refs
    c = pl.program_id(0)
    cnt = cnt_ref[c]

    @pl.when(cnt > 0)
    def _():
        wgb_ref[...] = wg_ref[0].astype(BF16)
        if gated:
            wub_ref[...] = wu_ref[0].astype(BF16)

        def sub_block(r, carry):
            rows = pl.ds(pl.multiple_of(r * MOE_SUB, MOE_SUB), MOE_SUB)
            xs = x_ref[rows, :]
            y = jnp.dot(xs, wgb_ref[...], preferred_element_type=F32) + bg_ref[0]
            if gated:
                up = jnp.dot(xs, wub_ref[...], preferred_element_type=F32) + bu_ref[0]
                gate = jnp.minimum(y, SWIGLU_LIMIT)
                up = jnp.clip(up, -SWIGLU_LIMIT, SWIGLU_LIMIT)
                y = (up + 1.0) * (gate * jax.nn.sigmoid(SWIGLU_ALPHA * gate))
            o_ref[rows, :] = y.astype(o_ref.dtype)
            return carry

        lax.fori_loop(0, (cnt + MOE_SUB - 1) // MOE_SUB, sub_block, 0)


def _moe_matmul(x_rows, plan, weights, biases, out_dtype):
    gated = len(weights) == 2
    n_exp, k, n = weights[0].shape
    n_chunks = plan["n_chunks"]
    nj = n // MOE_TN

    def col(c, j, nu):
        return jnp.where(c < nu[0], j, nj - 1)

    x_spec = pl.BlockSpec((MOE_CHUNK, k), lambda c, j, ce, cnt, blk, nu: (blk[c], 0))
    w_spec = pl.BlockSpec((1, k, MOE_TN), lambda c, j, ce, cnt, blk, nu: (ce[c], 0, col(c, j, nu)))
    b_spec = pl.BlockSpec((1, 1, MOE_TN), lambda c, j, ce, cnt, blk, nu: (ce[c], 0, col(c, j, nu)))
    operands, in_specs = [x_rows], [x_spec]
    for w, b in zip(weights, biases):
        operands += [w, b.reshape(n_exp, 1, n)]
        in_specs += [w_spec, b_spec]
    grid_spec = pltpu.PrefetchScalarGridSpec(
        num_scalar_prefetch=4, grid=(n_chunks, nj), in_specs=in_specs,
        out_specs=pl.BlockSpec((MOE_CHUNK, MOE_TN), lambda c, j, ce, cnt, blk, nu: (blk[c], col(c, j, nu))),
        scratch_shapes=[pltpu.VMEM((k, MOE_TN), BF16)] * len(weights))
    return pl.pallas_call(
        functools.partial(_moe_matmul_kernel, gated=gated),
        out_shape=jax.ShapeDtypeStruct((n_chunks * MOE_CHUNK, n), out_dtype), grid_spec=grid_spec,
        compiler_params=_cparams(2), name="moe_up" if gated else "moe_down",
    )(plan["c_exp"], plan["c_cnt"], plan["c_blk"], plan["n_used"], *operands)


def _moe_combine_kernel(dest_ref, y_hbm, x_ref, g_ref, o_ref, buf, sem, *, tb):
    base = pl.program_id(0) * tb * TOP_K

    def issue(r, carry):
        for k in range(TOP_K):
            _row_copy(y_hbm, dest_ref[base + r * TOP_K + k], buf.at[k], r, sem).start()
        return carry

    def drain(r, carry):
        for k in range(TOP_K):
            _row_copy(y_hbm, 0, buf.at[k], r, sem).wait()
        return carry

    lax.fori_loop(0, tb, issue, 0)
    lax.fori_loop(0, tb, drain, 0)
    gates = g_ref[...]
    moe = gates[:, 0:1] * buf[0]
    for k in range(1, TOP_K):
        moe = moe + gates[:, k:k + 1] * buf[k]
    o_ref[...] = x_ref[...] + moe


def _moe_combine(y_rows, dest, x, gates, tb):
    t, d = x.shape
    grid_spec = pltpu.PrefetchScalarGridSpec(
        num_scalar_prefetch=1, grid=(t // tb,),
        in_specs=[pl.BlockSpec(memory_space=pl.ANY),
                  pl.BlockSpec((tb, d), lambda i, dest: (i, 0)),
                  pl.BlockSpec((tb, TOP_K), lambda i, dest: (i, 0))],
        out_specs=pl.BlockSpec((tb, d), lambda i, dest: (i, 0)),
        scratch_shapes=[pltpu.VMEM((TOP_K, tb, d), F32), pltpu.SemaphoreType.DMA])
    return pl.pallas_call(
        functools.partial(_moe_combine_kernel, tb=tb), out_shape=jax.ShapeDtypeStruct((t, d), F32),
        grid_spec=grid_spec, compiler_params=_cparams(1), name="moe_combine",
    )(dest, y_rows, x, gates)


def _pick_tile(n, candidates):
    for c in candidates:
        if n % c == 0:
            return c
    return n


def kernel(x_prompt, x_sample, cache_sb_k, cache_sb_v, cache_diff_k, cache_diff_v, page_table, attn_norm, w_in, diff_q_norm, diff_k_norm, diff_lambda_q1, diff_lambda_k1, diff_lambda_q2, diff_lambda_k2, rel_bias, sb_out_norm, diff_subln, w_out, ffn_norm, w_router, b_router, w_gate, b_gate, w_up, b_up, w_down, b_down):
    depth = attn_norm.shape[0]
    assert depth == 1, "single-layer trunk"
    bsz, seq, d = x_prompt.shape
    db, dseq, _ = x_sample.shape
    assert dseq == 1
    n_exp = w_router.shape[2]
    half = d // 2
    sb_heads = half // HEAD_DIM
    sb_kv = sb_heads // GROUP
    diff_heads = half // (2 * HEAD_DIM)
    diff_kv = diff_heads // GROUP
    sbq, sbk = sb_heads * HEAD_DIM, sb_kv * HEAD_DIM
    dq, dk = diff_heads * 2 * HEAD_DIM, diff_kv * 2 * HEAD_DIM
    off_sbk, off_sbv = sbq, sbq + sbk
    off_dq = sbq + 2 * sbk
    off_dk = off_dq + dq
    off_dv = off_dk + dk
    in_cols = off_dv + dk
    layer = 0
    lambda_init = 0.8 - 0.6 * math.exp(-0.3 * layer)
    lam = (jnp.exp(jnp.sum(diff_lambda_q1[layer] * diff_lambda_k1[layer]))
           - jnp.exp(jnp.sum(diff_lambda_q2[layer] * diff_lambda_k2[layer])) + lambda_init).astype(F32)

    tn_in = 512
    assert off_dq % tn_in == 0 and off_dv % tn_in == 0
    qk_gain = jnp.concatenate([jnp.ones((off_dq,), F32), jnp.tile(diff_q_norm[layer], dq // HEAD_DIM),
                               jnp.tile(diff_k_norm[layer], dk // HEAD_DIM), jnp.ones((dk,), F32)]).reshape(1, in_cols)
    norm_tiles = dict(norm_lo=off_dq // tn_in, norm_hi=off_dv // tn_in)
    n_tok = bsz * seq

    xp = x_prompt.reshape(n_tok, d)
    h_p = _rmsnorm(xp, attn_norm[layer], BF16, _pick_tile(n_tok, (256, 128, 8)))
    tm_p = _pick_tile(n_tok, (1024, 512, 256, 128, 8))
    proj_p = _matmul(h_p, w_in[layer], qk_gain, high=False, mode="qknorm", tm=tm_p, tn=tn_in, **norm_tiles)
    proj3 = proj_p.reshape(bsz, seq, in_cols)
    mix_sb = _sb_prompt(proj3, sb_out_norm[layer].reshape(-1), kv_heads=sb_kv,
                        q_off=0, k_off=off_sbk, v_off=off_sbv)
    mix_d = _diff_prompt(proj3, rel_bias, lam, diff_subln[layer], kv_heads=diff_kv,
                         q_off=off_dq, k_off=off_dk, v_off=off_dv, out_scale=1.0 - lambda_init)
    mix_p = jnp.concatenate([mix_sb, mix_d], axis=-1).reshape(n_tok, d)
    x2_p = _matmul(mix_p, w_out[layer], xp, high=False, mode="residual", tm=tm_p, tn=512)

    xs = x_sample.reshape(db, d)
    h_s = _rmsnorm(xs, attn_norm[layer], F32, db)
    proj_s = _matmul(h_s, w_in[layer], qk_gain, high=True, mode="qknorm", tm=db, tn=tn_in, **norm_tiles)
    pool = cache_sb_k.shape[1]
    page = cache_sb_k.shape[2]
    assert page == LANES
    sb_o = _sb_decode(proj_s[:, :sbq].reshape(db, sb_heads, HEAD_DIM),
                      cache_sb_k[layer].reshape(pool, page, sbk), cache_sb_v[layer].reshape(pool, page, sbk),
                      page_table, sb_out_norm[layer], kv_heads=sb_kv)
    q_d = proj_s[:, off_dq:off_dk].reshape(db, diff_heads, 2, HEAD_DIM).transpose(0, 2, 1, 3)
    rbt = jnp.concatenate([rel_bias.T, rel_bias.T], axis=0).astype(F32)
    d_o = _diff_decode(q_d.reshape(db, 2 * diff_heads, HEAD_DIM),
                       proj_s[:, off_dk:off_dv].reshape(db, 1, dk), proj_s[:, off_dv:].reshape(db, 1, dk),
                       cache_diff_k[layer].reshape(pool, page, dk), cache_diff_v[layer].reshape(pool, page, dk),
                       page_table, rbt, lam, diff_subln[layer], kv_heads=diff_kv, out_scale=1.0 - lambda_init)
    mix_s = jnp.concatenate([sb_o.reshape(db, sbq), d_o.reshape(db, dq)], axis=-1)
    x2_s = _matmul(mix_s, w_out[layer], xs, high=True, mode="residual", tm=db, tn=512)

    h2_p, idx_p, gate_p = _router(x2_p, ffn_norm[layer], w_router[layer], b_router[layer],
                                  _pick_tile(n_tok, (256, 128)))
    h2_s, idx_s, gate_s = _router(x2_s, ffn_norm[layer], w_router[layer], b_router[layer], db)
    n_all = n_tok + db
    h_all = jnp.concatenate([h2_p, h2_s, jnp.zeros((8, d), F32)], axis=0)
    top_idx = jnp.concatenate([idx_p, idx_s], axis=1).T
    plan = _moe_plan(top_idx, n_exp, zero_token=n_all)
    x_rows = _moe_gather(h_all, plan)
    act = _moe_matmul(x_rows, plan, (w_gate[layer], w_up[layer]), (b_gate[layer], b_up[layer]), BF16)
    y_rows = _moe_matmul(act, plan, (w_down[layer],), (b_down[layer],), F32)
    dest = plan["dest"]
    y_p = _moe_combine(y_rows, dest[:n_tok * TOP_K], x2_p, gate_p.T, _pick_tile(n_tok, (COMBINE_TOKENS, 8)))
    y_s = _moe_combine(y_rows, dest[n_tok * TOP_K:], x2_s, gate_s.T, db)

    def rows(p, lead, lo, hi, shape):
        return p[:, lo:hi].reshape((depth,) + lead + shape)

    lead_p, lead_s = (bsz, seq), (db, dseq)
    return (y_p.reshape(bsz, seq, d), y_s.reshape(db, dseq, d),
            rows(proj_p, lead_p, off_sbk, off_sbv, (sb_kv, HEAD_DIM)),
            rows(proj_p, lead_p, off_sbv, off_dq, (sb_kv, HEAD_DIM)),
            rows(proj_p, lead_p, off_dk, off_dv, (diff_kv, 2, HEAD_DIM)),
            rows(proj_p, lead_p, off_dv, in_cols, (diff_kv, 2 * HEAD_DIM)),
            rows(proj_s, lead_s, off_sbk, off_sbv, (sb_kv, HEAD_DIM)),
            rows(proj_s, lead_s, off_sbv, off_dq, (sb_kv, HEAD_DIM)),
            rows(proj_s, lead_s, off_dk, off_dv, (diff_kv, 2, HEAD_DIM)),
            rows(proj_s, lead_s, off_dv, in_cols, (diff_kv, 2 * HEAD_DIM)))
```

```python
import functools
import math

import numpy as np
import jax
import jax.numpy as jnp
from jax import lax
from jax.experimental import pallas as pl
from jax.experimental.pallas import tpu as pltpu

F32 = jnp.float32
BF16 = jnp.bfloat16

HEAD_DIM = 128
GROUP = 4
N_BUCKETS = 32
MAX_EXACT = N_BUCKETS // 2
MAX_DISTANCE = 128
TOP_K = 4
SWIGLU_LIMIT = 7.0
SWIGLU_ALPHA = 1.702
EPS = 1e-5
NEG_BIG = -1e30

V7X_VMEM_LIMIT_BYTES = 56 * 1024 * 1024
LANES = 128
ATTN_BLOCK = 128
PAGES_PER_STEP = 8
DIFF_PAGES_PER_STEP = 16
MOE_SUB = 256
MOE_TAIL = 128
MOE_CHUNK = 5 * MOE_SUB
MOE_TN = 256
COMBINE_TOKENS = 64


def _cparams(n_axes):
    return pltpu.CompilerParams(dimension_semantics=("arbitrary",) * n_axes,
                                vmem_limit_bytes=V7X_VMEM_LIMIT_BYTES)


def _nt_dot(a, b, precision=None):
    return lax.dot_general(a, b, (((1,), (1,)), ((), ())), precision=precision,
                           preferred_element_type=F32)


def _rms(x, gain):
    ms = jnp.mean(x * x, axis=-1, keepdims=True)
    return x * lax.rsqrt(ms + EPS) * gain


def _bucket_thresholds():
    n = np.arange(MAX_DISTANCE + 1)
    nf = np.maximum(n, MAX_EXACT).astype(np.float32)
    large = MAX_EXACT + (np.log(nf / np.float32(MAX_EXACT)) / np.float32(math.log(MAX_DISTANCE / MAX_EXACT))
                         * np.float32(N_BUCKETS - MAX_EXACT)).astype(np.int32)
    large = np.minimum(large, N_BUCKETS - 1)
    bucket = np.where(n < MAX_EXACT, n, large)
    return [int(np.argmax(bucket >= k)) for k in range(1, N_BUCKETS)]


_BUCKET_THR = _bucket_thresholds()


def _rmsnorm_kernel(x_ref, g_ref, o_ref):
    o_ref[...] = _rms(x_ref[...], g_ref[...]).astype(o_ref.dtype)


def _rmsnorm(x, gain, out_dtype, tm):
    t, d = x.shape
    return pl.pallas_call(
        _rmsnorm_kernel,
        out_shape=jax.ShapeDtypeStruct((t, d), out_dtype),
        grid=(t // tm,),
        in_specs=[pl.BlockSpec((tm, d), lambda i: (i, 0)), pl.BlockSpec((1, d), lambda i: (0, 0))],
        out_specs=pl.BlockSpec((tm, d), lambda i: (i, 0)),
        compiler_params=_cparams(1), name="rmsnorm",
    )(x, gain.reshape(1, d))


def _matmul_kernel(a_ref, w_ref, e_ref, o_ref, wb_ref, *, mode, norm_lo, norm_hi, tn):
    j = pl.program_id(0)
    i = pl.program_id(1)

    @pl.when(i == 0)
    def _():
        wb_ref[...] = w_ref[...].astype(BF16)

    acc = jnp.dot(a_ref[...], wb_ref[...], preferred_element_type=F32)
    if mode == "residual":
        o_ref[...] = e_ref[...] + acc
    else:
        in_range = jnp.logical_and(j >= norm_lo, j < norm_hi)

        @pl.when(in_range)
        def _():
            for c in range(tn // LANES):
                sl = slice(c * LANES, (c + 1) * LANES)
                o_ref[:, sl] = _rms(acc[:, sl], e_ref[:, sl])

        @pl.when(jnp.logical_not(in_range))
        def _():
            o_ref[...] = acc


def _matmul(a, w, extra, *, mode, tm, tn, norm_lo=0, norm_hi=0):
    m, k = a.shape
    n = w.shape[1]
    if mode == "residual":
        e_spec = pl.BlockSpec((tm, tn), lambda j, i: (i, j))
    else:
        e_spec = pl.BlockSpec((1, tn), lambda j, i: (0, j))
    kern = functools.partial(_matmul_kernel, mode=mode, norm_lo=norm_lo, norm_hi=norm_hi, tn=tn)
    return pl.pallas_call(
        kern,
        out_shape=jax.ShapeDtypeStruct((m, n), F32),
        grid=(n // tn, m // tm),
        in_specs=[pl.BlockSpec((tm, k), lambda j, i: (i, 0)),
                  pl.BlockSpec((k, tn), lambda j, i: (0, j)),
                  e_spec],
        out_specs=pl.BlockSpec((tm, tn), lambda j, i: (i, j)),
        scratch_shapes=[pltpu.VMEM((k, tn), BF16)],
        compiler_params=_cparams(2), name="matmul_" + mode,
    )(a, w, extra)


def _softplus(z):
    return jnp.maximum(z, 0.0) + jnp.log(1.0 + jnp.exp(-jnp.abs(z)))


def _sb_prompt_kernel(q_ref, k_ref, v_ref, g_ref, o_ref, acc_ref, *, bq, scale):
    qi = pl.program_id(2)
    rows = GROUP * bq
    q = jnp.concatenate([q_ref[0, :, g * HEAD_DIM:(g + 1) * HEAD_DIM] for g in range(GROUP)], axis=0)
    qs = (q * scale).astype(BF16)
    row_i = lax.broadcasted_iota(jnp.int32, (bq, bq), 0)
    col_i = lax.broadcasted_iota(jnp.int32, (bq, bq), 1)
    later = jnp.where(row_i > col_i, 1.0, 0.0).astype(BF16)
    visible = jnp.concatenate([col_i < row_i] * GROUP, axis=0)

    later2 = jnp.concatenate([later, later], axis=0)

    def weights(z, carry, masked):
        sp = _softplus(z)
        log_keep = jnp.where(visible, -sp, 0.0) if masked else -sp
        hi, lo = _split_bf16(log_keep)
        between = jnp.dot(jnp.concatenate([hi, lo], axis=1), later2, preferred_element_type=F32) + carry
        w = jnp.exp(z - sp + between)
        if masked:
            w = jnp.where(visible, w, 0.0)
        return w.astype(BF16), carry + jnp.sum(log_keep, axis=-1, keepdims=True)

    def single(kb, carry, masked):
        start = pl.multiple_of(kb * bq, bq)
        k = k_ref[0, pl.ds(start, bq), :].astype(BF16)
        v = v_ref[0, pl.ds(start, bq), :].astype(BF16)
        w, carry = weights(_nt_dot(qs, k), carry, masked)
        acc_ref[...] += jnp.dot(w, v, preferred_element_type=F32)
        return carry

    n_pairs = qi // 2

    def pair(t, carry):
        start = pl.multiple_of((2 * (n_pairs - 1 - t)) * bq, bq)
        k = k_ref[0, pl.ds(start, 2 * bq), :].astype(BF16)
        v = v_ref[0, pl.ds(start, 2 * bq), :].astype(BF16)
        z = _nt_dot(qs, k)
        w_late, carry = weights(z[:, bq:], carry, False)
        w_early, carry = weights(z[:, :bq], carry, False)
        acc_ref[...] += jnp.dot(jnp.concatenate([w_early, w_late], axis=1), v, preferred_element_type=F32)
        return carry

    acc_ref[...] = jnp.zeros_like(acc_ref)
    carry = single(qi, jnp.zeros((rows, 1), F32), True)
    carry = lax.cond(qi % 2 == 1, lambda c: single(qi - 1, c, False), lambda c: c, carry)
    lax.fori_loop(0, n_pairs, pair, carry)

    acc = acc_ref[...]
    for g in range(GROUP):
        sl = slice(g * HEAD_DIM, (g + 1) * HEAD_DIM)
        o_ref[0, :, sl] = _rms(acc[g * bq:(g + 1) * bq, :], g_ref[:, sl]).astype(o_ref.dtype)


def _sb_prompt(proj3, gain, *, kv_heads, q_off, k_off, v_off):
    b, t, _ = proj3.shape
    bq = ATTN_BLOCK
    qw = GROUP * HEAD_DIM
    width = kv_heads * qw
    assert q_off % qw == 0 and k_off % HEAD_DIM == 0 and v_off % HEAD_DIM == 0 and t % bq == 0
    kern = functools.partial(_sb_prompt_kernel, bq=bq, scale=HEAD_DIM ** -0.5)
    return pl.pallas_call(
        kern,
        out_shape=jax.ShapeDtypeStruct((b, t, width), BF16),
        grid=(b, kv_heads, t // bq),
        in_specs=[pl.BlockSpec((1, bq, qw), lambda bi, h, qi: (bi, qi, q_off // qw + h)),
                  pl.BlockSpec((1, t, HEAD_DIM), lambda bi, h, qi: (bi, 0, k_off // HEAD_DIM + h)),
                  pl.BlockSpec((1, t, HEAD_DIM), lambda bi, h, qi: (bi, 0, v_off // HEAD_DIM + h)),
                  pl.BlockSpec((1, qw), lambda bi, h, qi: (0, h))],
        out_specs=pl.BlockSpec((1, bq, qw), lambda bi, h, qi: (bi, qi, h)),
        scratch_shapes=[pltpu.VMEM((GROUP * bq, HEAD_DIM), F32)],
        compiler_params=_cparams(3), name="sb_prompt",
    )(proj3, proj3, proj3, gain.reshape(1, width))


def _bias_from_distance(n, table):
    bias = table(0)
    for k in range(1, N_BUCKETS):
        bias = jnp.where(n >= _BUCKET_THR[k - 1], table(k), bias)
    return bias


def _diff_prompt_kernel(rb_ref, lam_ref, q_ref, k_ref, v_ref, sub_ref, o_ref,
                        tiles_ref, s_ref, m_ref, l_ref, acc_ref, *, bq, scale, kv_heads, out_scale):
    bi = pl.program_id(0)
    h = pl.program_id(1)
    qi = pl.program_id(2)
    rows = GROUP * bq
    vw = 2 * HEAD_DIM
    row_i = lax.broadcasted_iota(jnp.int32, (bq, bq), 0)
    col_i = lax.broadcasted_iota(jnp.int32, (bq, bq), 1)

    @pl.when(jnp.logical_and(jnp.logical_and(bi == 0, h == 0), qi == 0))
    def _():
        for kind in range(2):
            n = jnp.maximum(row_i - col_i + kind * bq, 0)
            for hh in range(kv_heads):
                for g in range(GROUP):
                    head = hh * GROUP + g
                    tiles_ref[kind, hh, g * bq:(g + 1) * bq, :] = _bias_from_distance(
                        n, lambda k, head=head: rb_ref[k, head])

    far_bias = jnp.concatenate(
        [jnp.full((bq, 1), rb_ref[N_BUCKETS - 1, h * GROUP + g], F32) for g in range(GROUP)], axis=0)
    causal = jnp.concatenate([col_i <= row_i] * GROUP, axis=0)
    qs = []
    for c in range(2):
        qc = jnp.concatenate([q_ref[0, :, (2 * g + c) * HEAD_DIM:(2 * g + c + 1) * HEAD_DIM]
                              for g in range(GROUP)], axis=0)
        qs.append((qc * scale).astype(BF16))

    def logits(c, start, width):
        k = k_ref[0, pl.ds(pl.multiple_of(start, bq), width), c * HEAD_DIM:(c + 1) * HEAD_DIM].astype(BF16)
        return _nt_dot(qs[c], k)

    n_far = jnp.maximum(qi - 1, 0)
    n_far_pairs = n_far // 2
    m_ref[...] = jnp.full_like(m_ref, NEG_BIG)

    def far_pair(j, carry):
        for c in range(2):
            s = logits(c, j * 2 * bq, 2 * bq)
            m_ref[c] = jnp.maximum(m_ref[c], jnp.maximum(s[:, :bq], s[:, bq:]))
            s_ref[c, 2 * j] = s[:, :bq] + far_bias
            s_ref[c, 2 * j + 1] = s[:, bq:] + far_bias
        return carry

    lax.fori_loop(0, n_far_pairs, far_pair, 0)

    @pl.when(n_far % 2 == 1)
    def _():
        for c in range(2):
            s = logits(c, (n_far - 1) * bq, bq)
            m_ref[c] = jnp.maximum(m_ref[c], s)
            s_ref[c, n_far - 1] = s + far_bias

    for c in range(2):
        m_ref[c] = m_ref[c] + far_bias

    @pl.when(qi >= 1)
    def _():
        for c in range(2):
            s = logits(c, (qi - 1) * bq, bq) + tiles_ref[1, h]
            m_ref[c] = jnp.maximum(m_ref[c], s)
            s_ref[c, qi - 1] = s

    for c in range(2):
        s = jnp.where(causal, logits(c, qi * bq, bq) + tiles_ref[0, h], NEG_BIG)
        s_ref[c, qi] = s
        m_ref[c] = jnp.broadcast_to(jnp.max(jnp.maximum(m_ref[c], s), axis=-1, keepdims=True), (rows, bq))

    @pl.when(qi % 2 == 0)
    def _():
        for c in range(2):
            s_ref[c, qi + 1] = jnp.full((rows, bq), NEG_BIG, F32)

    l_ref[...] = jnp.zeros_like(l_ref)
    acc_ref[...] = jnp.zeros_like(acc_ref)

    def accumulate(j, carry):
        v = v_ref[0, pl.ds(pl.multiple_of(j * 2 * bq, 2 * bq), 2 * bq), :].astype(BF16)
        for c in range(2):
            p0 = jnp.exp(s_ref[c, 2 * j] - m_ref[c])
            p1 = jnp.exp(s_ref[c, 2 * j + 1] - m_ref[c])
            l_ref[c] += p0 + p1
            acc_ref[c] += jnp.dot(jnp.concatenate([p0, p1], axis=1).astype(BF16), v, preferred_element_type=F32)
        return carry

    lax.fori_loop(0, qi // 2 + 1, accumulate, 0)

    lam = lam_ref[0]
    l0 = jnp.sum(l_ref[0], axis=-1, keepdims=True)
    l1 = jnp.sum(l_ref[1], axis=-1, keepdims=True)
    o = acc_ref[0] / l0 - lam * (acc_ref[1] / l1)
    for g in range(GROUP):
        o_ref[0, :, g * vw:(g + 1) * vw] = (_rms(o[g * bq:(g + 1) * bq, :], sub_ref[...]) * out_scale
                                            ).astype(o_ref.dtype)


def _diff_prompt(proj3, rel_bias, lam, subln, *, kv_heads, q_off, k_off, v_off, out_scale):
    b, t, _ = proj3.shape
    bq = ATTN_BLOCK
    qw = GROUP * 2 * HEAD_DIM
    vw = 2 * HEAD_DIM
    width = kv_heads * GROUP * vw
    assert q_off % qw == 0 and k_off % vw == 0 and v_off % vw == 0 and t % (2 * bq) == 0
    assert bq + 1 >= _BUCKET_THR[-1]
    kern = functools.partial(_diff_prompt_kernel, bq=bq, scale=HEAD_DIM ** -0.5, kv_heads=kv_heads,
                             out_scale=out_scale)
    smem = pl.BlockSpec(memory_space=pltpu.SMEM)
    return pl.pallas_call(
        kern,
        out_shape=jax.ShapeDtypeStruct((b, t, width), BF16),
        grid=(b, kv_heads, t // bq),
        in_specs=[smem, smem,
                  pl.BlockSpec((1, bq, qw), lambda bi, h, qi: (bi, qi, q_off // qw + h)),
                  pl.BlockSpec((1, t, vw), lambda bi, h, qi: (bi, 0, k_off // vw + h)),
                  pl.BlockSpec((1, t, vw), lambda bi, h, qi: (bi, 0, v_off // vw + h)),
                  pl.BlockSpec((1, vw), lambda bi, h, qi: (0, 0))],
        out_specs=pl.BlockSpec((1, bq, GROUP * vw), lambda bi, h, qi: (bi, qi, h)),
        scratch_shapes=[pltpu.VMEM((2, kv_heads, GROUP * bq, bq), F32),
                        pltpu.VMEM((2, t // bq, GROUP * bq, bq), F32),
                        pltpu.VMEM((2, GROUP * bq, bq), F32),
                        pltpu.VMEM((2, GROUP * bq, bq), F32),
                        pltpu.VMEM((2, GROUP * bq, vw), F32)],
        compiler_params=_cparams(3), name="diff_prompt",
    )(rel_bias, lam.reshape(1), proj3, proj3, proj3, subln.reshape(1, vw))


def _block_diag(q, n_blocks, block_of_row):
    r = q.shape[0]
    row = lax.broadcasted_iota(jnp.int32, (r, HEAD_DIM), 0)
    blk = block_of_row(row)
    return jnp.concatenate([jnp.where(blk == hb, q, 0.0) for hb in range(n_blocks)], axis=1)


def _split_bf16(x):
    hi = x.astype(BF16)
    lo = (x - hi.astype(F32)).astype(BF16)
    return hi, lo


def _page_lanes(ref, subs):
    n = ref.shape[1] // LANES
    return jnp.concatenate([ref[0, pl.ds(sub, LANES, stride=n), :] for sub in subs], axis=1).astype(BF16)


def _page_specs(n_pages, pages_per_step, width, step_of):
    specs = []
    for n in range(pages_per_step):
        def imap(b, s, pt, n=n):
            return (pt[b * n_pages + step_of(s) * pages_per_step + n], 0, 0)
        specs.append(pl.BlockSpec((1, width, LANES), imap))
    return specs


def _sb_decode_kernel(pt_ref, q_ref, *refs, kv_heads, scale):
    k_refs = refs[:PAGES_PER_STEP]
    v_refs = refs[PAGES_PER_STEP:2 * PAGES_PER_STEP]
    g_ref, o_ref, acc_ref, carry_ref = refs[2 * PAGES_PER_STEP:]
    s = pl.program_id(1)
    heads = kv_heads * GROUP
    width = PAGES_PER_STEP * LANES

    @pl.when(s == 0)
    def _():
        acc_ref[...] = jnp.zeros_like(acc_ref)
        carry_ref[...] = jnp.zeros_like(carry_ref)

    q_bd = _block_diag(q_ref[0], kv_heads, block_of_row=lambda r: r // GROUP).astype(BF16)
    subs = tuple(range(kv_heads))
    z = jnp.concatenate([_nt_dot(q_bd, _page_lanes(k_refs[p], subs)) for p in range(PAGES_PER_STEP)],
                        axis=1) * scale
    sp = _softplus(z)
    log_keep = -sp
    lane = lax.broadcasted_iota(jnp.int32, (heads, width), 1) % LANES
    suffix = log_keep
    sh = 1
    while sh < LANES:
        suffix = suffix + jnp.where(lane + sh < LANES, pltpu.roll(suffix, width - sh, 1), 0.0)
        sh *= 2
    run = carry_ref[...]
    offsets = [None] * PAGES_PER_STEP
    for p in reversed(range(PAGES_PER_STEP)):
        offsets[p] = run
        run = run + jnp.broadcast_to(suffix[:, p * LANES:p * LANES + 1], (heads, LANES))
    carry_ref[...] = run
    between = suffix - log_keep + jnp.concatenate(offsets, axis=1)
    w = jnp.exp(z - sp + between)
    o = acc_ref[...]
    for p in range(PAGES_PER_STEP):
        o = o + jnp.dot(w[:, p * LANES:(p + 1) * LANES].astype(BF16), _page_lanes(v_refs[p], subs),
                        preferred_element_type=F32)
    acc_ref[...] = o

    @pl.when(s == pl.num_programs(1) - 1)
    def _():
        row = lax.broadcasted_iota(jnp.int32, (heads, HEAD_DIM), 0)
        out = jnp.zeros((heads, HEAD_DIM), F32)
        for hb in range(kv_heads):
            out = out + jnp.where(row // GROUP == hb, o[:, hb * HEAD_DIM:(hb + 1) * HEAD_DIM], 0.0)
        o_ref[0] = _rms(out, g_ref[...])


def _sb_decode(q, cache_k, cache_v, page_table, gain, *, kv_heads):
    db, heads, _ = q.shape
    n_pages = page_table.shape[1]
    assert n_pages % PAGES_PER_STEP == 0
    n_steps = n_pages // PAGES_PER_STEP
    width = kv_heads * HEAD_DIM
    kern = functools.partial(_sb_decode_kernel, kv_heads=kv_heads, scale=HEAD_DIM ** -0.5)
    pages = _page_specs(n_pages, PAGES_PER_STEP, width, lambda s: n_steps - 1 - s)
    grid_spec = pltpu.PrefetchScalarGridSpec(
        num_scalar_prefetch=1, grid=(db, n_steps),
        in_specs=[pl.BlockSpec((1, heads, HEAD_DIM), lambda b, s, pt: (b, 0, 0))] + pages + pages
                 + [pl.BlockSpec((heads, HEAD_DIM), lambda b, s, pt: (0, 0))],
        out_specs=pl.BlockSpec((1, heads, HEAD_DIM), lambda b, s, pt: (b, 0, 0)),
        scratch_shapes=[pltpu.VMEM((heads, width), F32), pltpu.VMEM((heads, LANES), F32)])
    return pl.pallas_call(
        kern, out_shape=jax.ShapeDtypeStruct((db, heads, HEAD_DIM), F32), grid_spec=grid_spec,
        compiler_params=_cparams(2), name="sb_decode",
    )(page_table.reshape(-1), q, *([cache_k] * PAGES_PER_STEP), *([cache_v] * PAGES_PER_STEP), gain)


def _diff_decode_kernel(pt_ref, lam_ref, q_ref, kn_ref, vn_ref, rbt_ref, *refs,
                        kv_heads, scale, out_scale, n_steps):
    k_refs = refs[:DIFF_PAGES_PER_STEP]
    v_refs = refs[DIFF_PAGES_PER_STEP:2 * DIFF_PAGES_PER_STEP]
    sub_ref, o_ref, s_ref, m_ref, acc_ref = refs[2 * DIFF_PAGES_PER_STEP:]
    s = pl.program_id(1)
    heads = kv_heads * GROUP
    rows = 2 * heads
    vw = 2 * HEAD_DIM
    lam = lam_ref[0]
    rbt = rbt_ref[...]
    q_bd = _block_diag(q_ref[0], 2 * kv_heads,
                       block_of_row=lambda r: 2 * ((r % heads) // GROUP) + r // heads).astype(BF16)
    k_subs = tuple(range(2 * kv_heads))
    v_subs = tuple(half * kv_heads + hb for hb in range(kv_heads) for half in range(2))

    @pl.when(s == 0)
    def _():
        m_ref[...] = jnp.full_like(m_ref, NEG_BIG)

    @pl.when(s < n_steps)
    def _():
        far = jnp.broadcast_to(rbt[:, N_BUCKETS - 1:N_BUCKETS], (rows, LANES))
        n = LANES - lax.broadcasted_iota(jnp.int32, (rows, LANES), 1)
        near = _bias_from_distance(n, lambda k: jnp.broadcast_to(rbt[:, k:k + 1], (rows, LANES)))
        tiles = []
        for p in range(DIFF_PAGES_PER_STEP):
            bias = jnp.where(s == n_steps - 1, near, far) if p == DIFF_PAGES_PER_STEP - 1 else far
            tiles.append(_nt_dot(q_bd, _page_lanes(k_refs[p], k_subs)) * scale + bias)
        s_ref[s] = jnp.concatenate(tiles, axis=1)
        m_ref[...] = jnp.maximum(m_ref[...], functools.reduce(jnp.maximum, tiles))

    @pl.when(s == n_steps)
    def _():
        k_self = kn_ref[0].astype(BF16).astype(F32)
        s_self = jnp.sum(q_bd.astype(F32) * k_self, axis=-1, keepdims=True) * scale + rbt[:, 0:1]
        m = jnp.maximum(jnp.max(m_ref[...], axis=-1, keepdims=True), s_self)
        e_self = jnp.exp(s_self - m)
        denom = e_self
        for st in range(n_steps):
            e = jnp.exp(s_ref[st] - m)
            s_ref[st] = e
            denom = denom + jnp.sum(e, axis=-1, keepdims=True)
        for st in range(n_steps):
            p = s_ref[st] / denom
            s_ref[st, :heads] = p[:heads] - lam * p[heads:]
        p_self = e_self / denom
        a_self = (p_self[:heads] - lam * p_self[heads:]).astype(BF16).astype(F32)
        acc_ref[...] = a_self * vn_ref[0].astype(BF16).astype(F32)

    @pl.when(s >= n_steps)
    def _():
        attn = s_ref[s - n_steps, :heads].astype(BF16)
        o = acc_ref[...]
        for p in range(DIFF_PAGES_PER_STEP):
            o = o + jnp.dot(attn[:, p * LANES:(p + 1) * LANES], _page_lanes(v_refs[p], v_subs),
                            preferred_element_type=F32)
        acc_ref[...] = o

        @pl.when(s == 2 * n_steps - 1)
        def _():
            row = lax.broadcasted_iota(jnp.int32, (heads, vw), 0)
            out = jnp.zeros((heads, vw), F32)
            for hb in range(kv_heads):
                out = out + jnp.where(row // GROUP == hb, o[:, hb * vw:(hb + 1) * vw], 0.0)
            o_ref[0] = _rms(out, sub_ref[...]) * out_scale


def _diff_decode(q, k_new, v_new, cache_k, cache_v, page_table, rbt, lam, subln, *, kv_heads, out_scale):
    db, rows, _ = q.shape
    heads = rows // 2
    n_pages = page_table.shape[1]
    pps = DIFF_PAGES_PER_STEP
    assert n_pages % pps == 0 and LANES >= MAX_DISTANCE
    n_steps = n_pages // pps
    width = kv_heads * 2 * HEAD_DIM
    vw = 2 * HEAD_DIM
    kern = functools.partial(_diff_decode_kernel, kv_heads=kv_heads, scale=HEAD_DIM ** -0.5, out_scale=out_scale,
                             n_steps=n_steps)
    k_pages = _page_specs(n_pages, pps, width, lambda s: jnp.minimum(s, n_steps - 1))
    v_pages = _page_specs(n_pages, pps, width, lambda s: jnp.maximum(s - n_steps, 0))
    grid_spec = pltpu.PrefetchScalarGridSpec(
        num_scalar_prefetch=1, grid=(db, 2 * n_steps),
        in_specs=[pl.BlockSpec(memory_space=pltpu.SMEM),
                  pl.BlockSpec((1, rows, HEAD_DIM), lambda b, s, pt: (b, 0, 0)),
                  pl.BlockSpec((1, 1, width), lambda b, s, pt: (b, 0, 0)),
                  pl.BlockSpec((1, 1, width), lambda b, s, pt: (b, 0, 0)),
                  pl.BlockSpec((rows, N_BUCKETS), lambda b, s, pt: (0, 0))] + k_pages + v_pages
                 + [pl.BlockSpec((1, vw), lambda b, s, pt: (0, 0))],
        out_specs=pl.BlockSpec((1, heads, vw), lambda b, s, pt: (b, 0, 0)),
        scratch_shapes=[pltpu.VMEM((n_steps, rows, pps * LANES), F32), pltpu.VMEM((rows, LANES), F32),
                        pltpu.VMEM((heads, width), F32)])
    return pl.pallas_call(
        kern, out_shape=jax.ShapeDtypeStruct((db, heads, vw), F32), grid_spec=grid_spec,
        compiler_params=_cparams(2), name="diff_decode",
    )(page_table.reshape(-1), lam.reshape(1), q, k_new, v_new, rbt,
      *([cache_k] * pps), *([cache_v] * pps), subln.reshape(1, vw))


def _router_kernel(x_ref, g_ref, wr_ref, br_ref, h_ref, idx_ref, gate_ref):
    h = _rms(x_ref[...], g_ref[...])
    h_ref[...] = h
    logits = _nt_dot(wr_ref[...].astype(BF16), h.astype(BF16)) + br_ref[...]
    n_exp = logits.shape[0]
    expert = lax.broadcasted_iota(jnp.int32, logits.shape, 0)
    vals, idxs = [], []
    for _ in range(TOP_K):
        top = jnp.max(logits, axis=0, keepdims=True)
        idx = jnp.min(jnp.where(logits == top, expert, n_exp), axis=0, keepdims=True)
        vals.append(top)
        idxs.append(idx)
        logits = jnp.where(expert == idx, -jnp.inf, logits)
    top_val = jnp.concatenate(vals, axis=0)
    e = jnp.exp(top_val - top_val[0:1])
    gate_ref[...] = e / jnp.sum(e, axis=0, keepdims=True)
    idx_ref[...] = jnp.concatenate(idxs, axis=0)


def _router(x, gain, w_router, b_router, tm):
    t, d = x.shape
    n_exp = w_router.shape[1]
    return pl.pallas_call(
        _router_kernel,
        out_shape=(jax.ShapeDtypeStruct((t, d), F32), jax.ShapeDtypeStruct((TOP_K, t), jnp.int32),
                   jax.ShapeDtypeStruct((TOP_K, t), F32)),
        grid=(t // tm,),
        in_specs=[pl.BlockSpec((tm, d), lambda i: (i, 0)), pl.BlockSpec((1, d), lambda i: (0, 0)),
                  pl.BlockSpec((n_exp, d), lambda i: (0, 0)), pl.BlockSpec((n_exp, 1), lambda i: (0, 0))],
        out_specs=(pl.BlockSpec((tm, d), lambda i: (i, 0)), pl.BlockSpec((TOP_K, tm), lambda i: (0, i)),
                   pl.BlockSpec((TOP_K, tm), lambda i: (0, i))),
        compiler_params=_cparams(1), name="router",
    )(x, gain.reshape(1, d), w_router.T, b_router.reshape(n_exp, 1))


def _moe_plan(top_idx, n_exp, zero_token):
    n = top_idx.size
    n_chunks = n_exp + n // MOE_CHUNK + 1
    sub_per_chunk = MOE_CHUNK // MOE_SUB
    n_gather = n_exp + n // MOE_SUB + 1
    flat_e = top_idx.reshape(-1).astype(jnp.int32)
    counts = jnp.bincount(flat_e, length=n_exp).astype(jnp.int32)
    order = jnp.argsort(flat_e, stable=True).astype(jnp.int32)
    position = jnp.argsort(order).astype(jnp.int32)
    group_start = jnp.cumsum(counts) - counts
    chunks_e = (counts + MOE_CHUNK - 1) // MOE_CHUNK
    chunk_end_e = jnp.cumsum(chunks_e)
    chunk_start_e = chunk_end_e - chunks_e
    dest = chunk_start_e[flat_e] * MOE_CHUNK + position - group_start[flat_e]
    n_used = chunk_end_e[-1]
    cidx = jnp.arange(n_chunks, dtype=jnp.int32)
    c_exp = jnp.clip(jnp.searchsorted(chunk_end_e, cidx, side="right"), 0, n_exp - 1).astype(jnp.int32)
    c_cnt = jnp.clip(counts[c_exp] - (cidx - chunk_start_e[c_exp]) * MOE_CHUNK, 0, MOE_CHUNK)
    used = cidx < n_used
    c_exp = jnp.where(used, c_exp, c_exp[n_used - 1])
    c_cnt = jnp.where(used, c_cnt, 0).astype(jnp.int32)
    c_blk = jnp.where(used, cidx, n_used - 1).astype(jnp.int32)
    subs_c = (c_cnt + MOE_SUB - 1) // MOE_SUB
    sub_end = jnp.cumsum(subs_c)
    gidx = jnp.minimum(jnp.arange(n_gather, dtype=jnp.int32), sub_end[-1] - 1)
    g_chunk = jnp.clip(jnp.searchsorted(sub_end, gidx, side="right"), 0, n_chunks - 1).astype(jnp.int32)
    g_sub = gidx - (sub_end - subs_c)[g_chunk]
    g_dst = (g_chunk * sub_per_chunk + g_sub).astype(jnp.int32)
    g_exp = c_exp[g_chunk]
    in_expert = ((g_chunk - chunk_start_e[g_exp]) * MOE_CHUNK + g_sub * MOE_SUB)[:, None] \
        + jnp.arange(MOE_SUB, dtype=jnp.int32)[None, :]
    live = in_expert < counts[g_exp][:, None]
    element = order[jnp.clip(group_start[g_exp][:, None] + in_expert, 0, n - 1)]
    g_tok = jnp.where(live, element // TOP_K, zero_token).astype(jnp.int32).reshape(-1)
    return dict(n_chunks=n_chunks, dest=dest.astype(jnp.int32), c_exp=c_exp, c_cnt=c_cnt, c_blk=c_blk,
                n_used=n_used.reshape(1).astype(jnp.int32), g_dst=g_dst, g_tok=g_tok)


def _row_copy(src_hbm, row, dst_vmem, slot, sem):
    return pltpu.make_async_copy(src_hbm.at[pl.ds(row, 1), :], dst_vmem.at[pl.ds(slot, 1), :], sem)


def _moe_gather_kernel(tok_ref, dst_ref, h_hbm, o_ref, buf, sem):
    base = pl.program_id(0) * MOE_SUB

    def issue(r, carry):
        _row_copy(h_hbm, tok_ref[base + r], buf, r, sem).start()
        return carry

    def drain(r, carry):
        _row_copy(h_hbm, 0, buf, r, sem).wait()
        return carry

    lax.fori_loop(0, MOE_SUB, issue, 0, unroll=8)
    lax.fori_loop(0, MOE_SUB, drain, 0, unroll=8)
    o_ref[...] = buf[...].astype(o_ref.dtype)


def _moe_gather(h_all, plan):
    d = h_all.shape[1]
    n_gather = plan["g_dst"].shape[0]
    grid_spec = pltpu.PrefetchScalarGridSpec(
        num_scalar_prefetch=2, grid=(n_gather,),
        in_specs=[pl.BlockSpec(memory_space=pl.ANY)],
        out_specs=pl.BlockSpec((MOE_SUB, d), lambda i, tok, dst: (dst[i], 0)),
        scratch_shapes=[pltpu.VMEM((MOE_SUB, d), F32), pltpu.SemaphoreType.DMA])
    return pl.pallas_call(
        _moe_gather_kernel, out_shape=jax.ShapeDtypeStruct((plan["n_chunks"] * MOE_CHUNK, d), BF16),
        grid_spec=grid_spec, compiler_params=_cparams(1), name="moe_gather",
    )(plan["g_tok"], plan["g_dst"], h_all)


def _moe_matmul_kernel(ce_ref, cnt_ref, blk_ref, nu_ref, x_ref, *refs, gated):
    if gated:
        wg_ref, bg_ref, wu_ref, bu_ref, o_ref, wgb_ref, wub_ref = refs
    else:
        wg_ref, bg_ref, o_ref, wgb_ref = refs
    c = pl.program_id(0)
    cnt = cnt_ref[c]

    @pl.when(cnt > 0)
    def _():
        wgb_ref[...] = wg_ref[0].astype(BF16)
        if gated:
            wub_ref[...] = wu_ref[0].astype(BF16)

        def rows_block(start, size):
            rows = pl.ds(pl.multiple_of(start, MOE_TAIL), size)
            xs = x_ref[rows, :]
            y = jnp.dot(xs, wgb_ref[...], preferred_element_type=F32) + bg_ref[0]
            if gated:
                up = jnp.dot(xs, wub_ref[...], preferred_element_type=F32) + bu_ref[0]
                gate = jnp.minimum(y, SWIGLU_LIMIT)
                up = jnp.clip(up, -SWIGLU_LIMIT, SWIGLU_LIMIT)
                y = (up + 1.0) * (gate * jax.nn.sigmoid(SWIGLU_ALPHA * gate))
            o_ref[rows, :] = y.astype(o_ref.dtype)

        n_full = cnt // MOE_SUB
        n_tail = (cnt - n_full * MOE_SUB + MOE_TAIL - 1) // MOE_TAIL

        def full_body(r, carry):
            rows_block(r * MOE_SUB, MOE_SUB)
            return carry

        def tail_body(r, carry):
            rows_block(n_full * MOE_SUB + r * MOE_TAIL, MOE_TAIL)
            return carry

        lax.fori_loop(0, n_full, full_body, 0)
        lax.fori_loop(0, n_tail, tail_body, 0)


def _moe_matmul(x_rows, plan, weights, biases, out_dtype):
    gated = len(weights) == 2
    n_exp, k, n = weights[0].shape
    n_chunks = plan["n_chunks"]
    nj = n // MOE_TN

    def col(c, j, nu):
        return jnp.where(c < nu[0], j, nj - 1)

    x_spec = pl.BlockSpec((MOE_CHUNK, k), lambda c, j, ce, cnt, blk, nu: (blk[c], 0))
    w_spec = pl.BlockSpec((1, k, MOE_TN), lambda c, j, ce, cnt, blk, nu: (ce[c], 0, col(c, j, nu)))
    b_spec = pl.BlockSpec((1, 1, MOE_TN), lambda c, j, ce, cnt, blk, nu: (ce[c], 0, col(c, j, nu)))
    operands, in_specs = [x_rows], [x_spec]
    for w, b in zip(weights, biases):
        operands += [w, b.reshape(n_exp, 1, n)]
        in_specs += [w_spec, b_spec]
    grid_spec = pltpu.PrefetchScalarGridSpec(
        num_scalar_prefetch=4, grid=(n_chunks, nj), in_specs=in_specs,
        out_specs=pl.BlockSpec((MOE_CHUNK, MOE_TN), lambda c, j, ce, cnt, blk, nu: (blk[c], col(c, j, nu))),
        scratch_shapes=[pltpu.VMEM((k, MOE_TN), BF16)] * len(weights))
    return pl.pallas_call(
        functools.partial(_moe_matmul_kernel, gated=gated),
        out_shape=jax.ShapeDtypeStruct((n_chunks * MOE_CHUNK, n), out_dtype), grid_spec=grid_spec,
        compiler_params=_cparams(2), name="moe_up" if gated else "moe_down",
    )(plan["c_exp"], plan["c_cnt"], plan["c_blk"], plan["n_used"], *operands)


def _moe_combine_kernel(dest_ref, y_hbm, x_ref, g_ref, o_ref, buf, sem, *, tb):
    base = pl.program_id(0) * tb * TOP_K

    def issue(r, carry):
        for k in range(TOP_K):
            _row_copy(y_hbm, dest_ref[base + r * TOP_K + k], buf.at[k], r, sem).start()
        return carry

    def drain(r, carry):
        for k in range(TOP_K):
            _row_copy(y_hbm, 0, buf.at[k], r, sem).wait()
        return carry

    lax.fori_loop(0, tb, issue, 0, unroll=4)
    lax.fori_loop(0, tb, drain, 0, unroll=4)
    gates = g_ref[...]
    moe = gates[:, 0:1] * buf[0]
    for k in range(1, TOP_K):
        moe = moe + gates[:, k:k + 1] * buf[k]
    o_ref[...] = x_ref[...] + moe


def _moe_combine(y_rows, dest, x, gates, tb):
    t, d = x.shape
    grid_spec = pltpu.PrefetchScalarGridSpec(
        num_scalar_prefetch=1, grid=(t // tb,),
        in_specs=[pl.BlockSpec(memory_space=pl.ANY),
                  pl.BlockSpec((tb, d), lambda i, dest: (i, 0)),
                  pl.BlockSpec((tb, TOP_K), lambda i, dest: (i, 0))],
        out_specs=pl.BlockSpec((tb, d), lambda i, dest: (i, 0)),
        scratch_shapes=[pltpu.VMEM((TOP_K, tb, d), F32), pltpu.SemaphoreType.DMA])
    return pl.pallas_call(
        functools.partial(_moe_combine_kernel, tb=tb), out_shape=jax.ShapeDtypeStruct((t, d), F32),
        grid_spec=grid_spec, compiler_params=_cparams(1), name="moe_combine",
    )(dest, y_rows, x, gates)


def _pick_tile(n, candidates):
    for c in candidates:
        if n % c == 0:
            return c
    return n


def kernel(x_prompt, x_sample, cache_sb_k, cache_sb_v, cache_diff_k, cache_diff_v, page_table, attn_norm, w_in, diff_q_norm, diff_k_norm, diff_lambda_q1, diff_lambda_k1, diff_lambda_q2, diff_lambda_k2, rel_bias, sb_out_norm, diff_subln, w_out, ffn_norm, w_router, b_router, w_gate, b_gate, w_up, b_up, w_down, b_down):
    depth = attn_norm.shape[0]
    assert depth == 1, "single-layer trunk"
    bsz, seq, d = x_prompt.shape
    db, dseq, _ = x_sample.shape
    assert dseq == 1
    n_exp = w_router.shape[2]
    half = d // 2
    sb_heads = half // HEAD_DIM
    sb_kv = sb_heads // GROUP
    diff_heads = half // (2 * HEAD_DIM)
    diff_kv = diff_heads // GROUP
    sbq, sbk = sb_heads * HEAD_DIM, sb_kv * HEAD_DIM
    dq, dk = diff_heads * 2 * HEAD_DIM, diff_kv * 2 * HEAD_DIM
    off_sbk, off_sbv = sbq, sbq + sbk
    off_dq = sbq + 2 * sbk
    off_dk = off_dq + dq
    off_dv = off_dk + dk
    in_cols = off_dv + dk
    layer = 0
    lambda_init = 0.8 - 0.6 * math.exp(-0.3 * layer)
    lam = (jnp.exp(jnp.sum(diff_lambda_q1[layer] * diff_lambda_k1[layer]))
           - jnp.exp(jnp.sum(diff_lambda_q2[layer] * diff_lambda_k2[layer])) + lambda_init).astype(F32)

    tn_in = 512
    assert off_dq % tn_in == 0 and off_dv % tn_in == 0
    qk_gain = jnp.concatenate([jnp.ones((off_dq,), F32), jnp.tile(diff_q_norm[layer], dq // HEAD_DIM),
                               jnp.tile(diff_k_norm[layer], dk // HEAD_DIM), jnp.ones((dk,), F32)]).reshape(1, in_cols)
    norm_tiles = dict(norm_lo=off_dq // tn_in, norm_hi=off_dv // tn_in)
    n_tok = bsz * seq

    xp = x_prompt.reshape(n_tok, d)
    h_p = _rmsnorm(xp, attn_norm[layer], BF16, _pick_tile(n_tok, (256, 128, 8)))
    tm_p = _pick_tile(n_tok, (1024, 512, 256, 128, 8))
    proj_p = _matmul(h_p, w_in[layer], qk_gain, mode="qknorm", tm=tm_p, tn=tn_in, **norm_tiles)
    proj3 = proj_p.reshape(bsz, seq, in_cols)
    mix_sb = _sb_prompt(proj3, sb_out_norm[layer].reshape(-1), kv_heads=sb_kv,
                        q_off=0, k_off=off_sbk, v_off=off_sbv)
    mix_d = _diff_prompt(proj3, rel_bias, lam, diff_subln[layer], kv_heads=diff_kv,
                         q_off=off_dq, k_off=off_dk, v_off=off_dv, out_scale=1.0 - lambda_init)
    mix_p = jnp.concatenate([mix_sb, mix_d], axis=-1).reshape(n_tok, d)
    x2_p = _matmul(mix_p, w_out[layer], xp, mode="residual", tm=tm_p, tn=512)

    xs = x_sample.reshape(db, d)
    h_s = _rmsnorm(xs, attn_norm[layer], BF16, db)
    proj_s = _matmul(h_s, w_in[layer], qk_gain, mode="qknorm", tm=db, tn=tn_in, **norm_tiles)
    pool = cache_sb_k.shape[1]
    page = cache_sb_k.shape[2]
    assert page == LANES
    rows4 = (pool, page * (sbk // HEAD_DIM), HEAD_DIM)
    assert sbk == dk == 4 * HEAD_DIM
    sb_o = _sb_decode(proj_s[:, :sbq].reshape(db, sb_heads, HEAD_DIM),
                      cache_sb_k.reshape(rows4), cache_sb_v.reshape(rows4),
                      page_table, sb_out_norm[layer], kv_heads=sb_kv)
    diff_v_rows = cache_diff_v.reshape(pool, page, diff_kv, 2, HEAD_DIM).transpose(0, 1, 3, 2, 4).reshape(rows4)
    q_d = proj_s[:, off_dq:off_dk].reshape(db, diff_heads, 2, HEAD_DIM).transpose(0, 2, 1, 3)
    rbt = jnp.concatenate([rel_bias.T, rel_bias.T], axis=0).astype(F32)
    d_o = _diff_decode(q_d.reshape(db, 2 * diff_heads, HEAD_DIM),
                       proj_s[:, off_dk:off_dv].reshape(db, 1, dk), proj_s[:, off_dv:].reshape(db, 1, dk),
                       cache_diff_k.reshape(rows4), diff_v_rows,
                       page_table, rbt, lam, diff_subln[layer], kv_heads=diff_kv, out_scale=1.0 - lambda_init)
    mix_s = jnp.concatenate([sb_o.reshape(db, sbq), d_o.reshape(db, dq)], axis=-1).astype(BF16)
    x2_s = _matmul(mix_s, w_out[layer], xs, mode="residual", tm=db, tn=512)

    h2_p, idx_p, gate_p = _router(x2_p, ffn_norm[layer], w_router[layer], b_router[layer],
                                  _pick_tile(n_tok, (256, 128)))
    h2_s, idx_s, gate_s = _router(x2_s, ffn_norm[layer], w_router[layer], b_router[layer], db)
    n_all = n_tok + db
    h_all = jnp.concatenate([h2_p, h2_s, jnp.zeros((8, d), F32)], axis=0)
    top_idx = jnp.concatenate([idx_p, idx_s], axis=1).T
    plan = _moe_plan(top_idx, n_exp, zero_token=n_all)
    x_rows = _moe_gather(h_all, plan)
    act = _moe_matmul(x_rows, plan, (w_gate[layer], w_up[layer]), (b_gate[layer], b_up[layer]), BF16)
    y_rows = _moe_matmul(act, plan, (w_down[layer],), (b_down[layer],), F32)
    dest = plan["dest"]
    y_p = _moe_combine(y_rows, dest[:n_tok * TOP_K], x2_p, gate_p.T, _pick_tile(n_tok, (COMBINE_TOKENS, 8)))
    y_s = _moe_combine(y_rows, dest[n_tok * TOP_K:], x2_s, gate_s.T, db)

    def rows(p, lead, lo, hi, shape):
        return p[:, lo:hi].reshape((depth,) + lead + shape)

    lead_p, lead_s = (bsz, seq), (db, dseq)
    return (y_p.reshape(bsz, seq, d), y_s.reshape(db, dseq, d),
            rows(proj_p, lead_p, off_sbk, off_sbv, (sb_kv, HEAD_DIM)),
            rows(proj_p, lead_p, off_sbv, off_dq, (sb_kv, HEAD_DIM)),
            rows(proj_p, lead_p, off_dk, off_dv, (diff_kv, 2, HEAD_DIM)),
            rows(proj_p, lead_p, off_dv, in_cols, (diff_kv, 2 * HEAD_DIM)),
            rows(proj_s, lead_s, off_sbk, off_sbv, (sb_kv, HEAD_DIM)),
            rows(proj_s, lead_s, off_sbv, off_dq, (sb_kv, HEAD_DIM)),
            rows(proj_s, lead_s, off_dk, off_dv, (diff_kv, 2, HEAD_DIM)),
            rows(proj_s, lead_s, off_dv, in_cols, (diff_kv, 2 * HEAD_DIM)))
```

```python
import functools
import math

import numpy as np
import jax
import jax.numpy as jnp
from jax import lax
from jax.experimental import pallas as pl
from jax.experimental.pallas import tpu as pltpu

F32 = jnp.float32
BF16 = jnp.bfloat16

HEAD_DIM = 128
GROUP = 4
N_BUCKETS = 32
MAX_EXACT = N_BUCKETS // 2
MAX_DISTANCE = 128
TOP_K = 4
SWIGLU_LIMIT = 7.0
SWIGLU_ALPHA = 1.702
EPS = 1e-5
NEG_BIG = -1e30

V7X_VMEM_LIMIT_BYTES = 56 * 1024 * 1024
LANES = 128
ATTN_BLOCK = 128
PAGES_PER_STEP = 8
DIFF_PAGES_PER_STEP = 16
MOE_SUB = 256
MOE_TAIL = 128
MOE_BLOCKS = (1024, 512, 256, 128)
MOE_CHUNK = 5 * MOE_SUB
MOE_TN = 256
COMBINE_TOKENS = 64
DMA_THREADS = 2


def _cparams(n_axes):
    return pltpu.CompilerParams(dimension_semantics=("arbitrary",) * n_axes,
                                vmem_limit_bytes=V7X_VMEM_LIMIT_BYTES)


def _nt_dot(a, b, precision=None):
    return lax.dot_general(a, b, (((1,), (1,)), ((), ())), precision=precision,
                           preferred_element_type=F32)


def _rms(x, gain):
    ms = jnp.mean(x * x, axis=-1, keepdims=True)
    return x * lax.rsqrt(ms + EPS) * gain


def _bucket_thresholds():
    n = np.arange(MAX_DISTANCE + 1)
    nf = np.maximum(n, MAX_EXACT).astype(np.float32)
    large = MAX_EXACT + (np.log(nf / np.float32(MAX_EXACT)) / np.float32(math.log(MAX_DISTANCE / MAX_EXACT))
                         * np.float32(N_BUCKETS - MAX_EXACT)).astype(np.int32)
    large = np.minimum(large, N_BUCKETS - 1)
    bucket = np.where(n < MAX_EXACT, n, large)
    return [int(np.argmax(bucket >= k)) for k in range(1, N_BUCKETS)]


_BUCKET_THR = _bucket_thresholds()


def _rmsnorm_kernel(x_ref, g_ref, o_ref):
    o_ref[...] = _rms(x_ref[...], g_ref[...]).astype(o_ref.dtype)


def _rmsnorm(x, gain, out_dtype, tm):
    t, d = x.shape
    return pl.pallas_call(
        _rmsnorm_kernel,
        out_shape=jax.ShapeDtypeStruct((t, d), out_dtype),
        grid=(t // tm,),
        in_specs=[pl.BlockSpec((tm, d), lambda i: (i, 0)), pl.BlockSpec((1, d), lambda i: (0, 0))],
        out_specs=pl.BlockSpec((tm, d), lambda i: (i, 0)),
        compiler_params=_cparams(1), name="rmsnorm",
    )(x, gain.reshape(1, d))


def _matmul_kernel(a_ref, w_ref, e_ref, o_ref, wb_ref, *, mode, norm_lo, norm_hi, tn):
    j = pl.program_id(0)
    i = pl.program_id(1)

    @pl.when(i == 0)
    def _():
        wb_ref[...] = w_ref[...].astype(BF16)

    acc = jnp.dot(a_ref[...], wb_ref[...], preferred_element_type=F32)
    if mode == "residual":
        o_ref[...] = e_ref[...] + acc
    else:
        in_range = jnp.logical_and(j >= norm_lo, j < norm_hi)

        @pl.when(in_range)
        def _():
            for c in range(tn // LANES):
                sl = slice(c * LANES, (c + 1) * LANES)
                o_ref[:, sl] = _rms(acc[:, sl], e_ref[:, sl])

        @pl.when(jnp.logical_not(in_range))
        def _():
            o_ref[...] = acc


def _matmul(a, w, extra, *, mode, tm, tn, norm_lo=0, norm_hi=0):
    m, k = a.shape
    n = w.shape[1]
    if mode == "residual":
        e_spec = pl.BlockSpec((tm, tn), lambda j, i: (i, j))
    else:
        e_spec = pl.BlockSpec((1, tn), lambda j, i: (0, j))
    kern = functools.partial(_matmul_kernel, mode=mode, norm_lo=norm_lo, norm_hi=norm_hi, tn=tn)
    return pl.pallas_call(
        kern,
        out_shape=jax.ShapeDtypeStruct((m, n), F32),
        grid=(n // tn, m // tm),
        in_specs=[pl.BlockSpec((tm, k), lambda j, i: (i, 0)),
                  pl.BlockSpec((k, tn), lambda j, i: (0, j)),
                  e_spec],
        out_specs=pl.BlockSpec((tm, tn), lambda j, i: (i, j)),
        scratch_shapes=[pltpu.VMEM((k, tn), BF16)],
        compiler_params=_cparams(2), name="matmul_" + mode,
    )(a, w, extra)


def _softplus(z):
    return jnp.maximum(z, 0.0) + jnp.log(1.0 + jnp.exp(-jnp.abs(z)))


def _sb_prompt_kernel(q_ref, k_ref, v_ref, g_ref, o_ref, acc_ref, *, bq, scale):
    qi = pl.program_id(2)
    rows = GROUP * bq
    q = jnp.concatenate([q_ref[0, :, g * HEAD_DIM:(g + 1) * HEAD_DIM] for g in range(GROUP)], axis=0)
    qs = (q * scale).astype(BF16)
    row_i = lax.broadcasted_iota(jnp.int32, (bq, bq), 0)
    col_i = lax.broadcasted_iota(jnp.int32, (bq, bq), 1)
    later = jnp.where(row_i > col_i, 1.0, 0.0).astype(BF16)
    visible = jnp.concatenate([col_i < row_i] * GROUP, axis=0)

    later2 = jnp.concatenate([later, later], axis=0)

    def weights(z, carry, masked):
        sp = _softplus(z)
        log_keep = jnp.where(visible, -sp, 0.0) if masked else -sp
        hi, lo = _split_bf16(log_keep)
        between = jnp.dot(jnp.concatenate([hi, lo], axis=1), later2, preferred_element_type=F32) + carry
        w = jnp.exp(z - sp + between)
        if masked:
            w = jnp.where(visible, w, 0.0)
        return w.astype(BF16), carry + jnp.sum(log_keep, axis=-1, keepdims=True)

    def single(kb, carry, masked):
        start = pl.multiple_of(kb * bq, bq)
        k = k_ref[0, pl.ds(start, bq), :].astype(BF16)
        v = v_ref[0, pl.ds(start, bq), :].astype(BF16)
        w, carry = weights(_nt_dot(qs, k), carry, masked)
        acc_ref[...] += jnp.dot(w, v, preferred_element_type=F32)
        return carry

    n_pairs = qi // 2

    def pair(t, carry):
        start = pl.multiple_of((2 * (n_pairs - 1 - t)) * bq, bq)
        k = k_ref[0, pl.ds(start, 2 * bq), :].astype(BF16)
        v = v_ref[0, pl.ds(start, 2 * bq), :].astype(BF16)
        z = _nt_dot(qs, k)
        w_late, carry = weights(z[:, bq:], carry, False)
        w_early, carry = weights(z[:, :bq], carry, False)
        acc_ref[...] += jnp.dot(jnp.concatenate([w_early, w_late], axis=1), v, preferred_element_type=F32)
        return carry

    acc_ref[...] = jnp.zeros_like(acc_ref)
    carry = single(qi, jnp.zeros((rows, 1), F32), True)
    carry = lax.cond(qi % 2 == 1, lambda c: single(qi - 1, c, False), lambda c: c, carry)
    lax.fori_loop(0, n_pairs, pair, carry)

    acc = acc_ref[...]
    for g in range(GROUP):
        sl = slice(g * HEAD_DIM, (g + 1) * HEAD_DIM)
        o_ref[0, :, sl] = _rms(acc[g * bq:(g + 1) * bq, :], g_ref[:, sl]).astype(o_ref.dtype)


def _sb_prompt(proj3, gain, *, kv_heads, q_off, k_off, v_off):
    b, t, _ = proj3.shape
    bq = ATTN_BLOCK
    qw = GROUP * HEAD_DIM
    width = kv_heads * qw
    assert q_off % qw == 0 and k_off % HEAD_DIM == 0 and v_off % HEAD_DIM == 0 and t % bq == 0
    kern = functools.partial(_sb_prompt_kernel, bq=bq, scale=HEAD_DIM ** -0.5)
    return pl.pallas_call(
        kern,
        out_shape=jax.ShapeDtypeStruct((b, t, width), BF16),
        grid=(b, kv_heads, t // bq),
        in_specs=[pl.BlockSpec((1, bq, qw), lambda bi, h, qi: (bi, qi, q_off // qw + h)),
                  pl.BlockSpec((1, t, HEAD_DIM), lambda bi, h, qi: (bi, 0, k_off // HEAD_DIM + h)),
                  pl.BlockSpec((1, t, HEAD_DIM), lambda bi, h, qi: (bi, 0, v_off // HEAD_DIM + h)),
                  pl.BlockSpec((1, qw), lambda bi, h, qi: (0, h))],
        out_specs=pl.BlockSpec((1, bq, qw), lambda bi, h, qi: (bi, qi, h)),
        scratch_shapes=[pltpu.VMEM((GROUP * bq, HEAD_DIM), F32)],
        compiler_params=_cparams(3), name="sb_prompt",
    )(proj3, proj3, proj3, gain.reshape(1, width))


def _bias_from_distance(n, table):
    bias = table(0)
    for k in range(1, N_BUCKETS):
        bias = jnp.where(n >= _BUCKET_THR[k - 1], table(k), bias)
    return bias


def _diff_prompt_kernel(rb_ref, lam_ref, q_ref, k_ref, v_ref, sub_ref, o_ref,
                        tiles_ref, s_ref, m_ref, l_ref, acc_ref, *, bq, scale, kv_heads, out_scale):
    bi = pl.program_id(0)
    h = pl.program_id(1)
    qi = pl.program_id(2)
    rows = GROUP * bq
    vw = 2 * HEAD_DIM
    row_i = lax.broadcasted_iota(jnp.int32, (bq, bq), 0)
    col_i = lax.broadcasted_iota(jnp.int32, (bq, bq), 1)

    @pl.when(jnp.logical_and(jnp.logical_and(bi == 0, h == 0), qi == 0))
    def _():
        for kind in range(2):
            n = jnp.maximum(row_i - col_i + kind * bq, 0)
            for hh in range(kv_heads):
                for g in range(GROUP):
                    head = hh * GROUP + g
                    tiles_ref[kind, hh, g * bq:(g + 1) * bq, :] = _bias_from_distance(
                        n, lambda k, head=head: rb_ref[k, head])

    far_bias = jnp.concatenate(
        [jnp.full((bq, 1), rb_ref[N_BUCKETS - 1, h * GROUP + g], F32) for g in range(GROUP)], axis=0)
    causal = jnp.concatenate([col_i <= row_i] * GROUP, axis=0)
    qs = []
    for c in range(2):
        qc = jnp.concatenate([q_ref[0, :, (2 * g + c) * HEAD_DIM:(2 * g + c + 1) * HEAD_DIM]
                              for g in range(GROUP)], axis=0)
        qs.append((qc * scale).astype(BF16))

    def logits(c, start, width):
        k = k_ref[0, pl.ds(pl.multiple_of(start, bq), width), c * HEAD_DIM:(c + 1) * HEAD_DIM].astype(BF16)
        return _nt_dot(qs[c], k)

    n_far = jnp.maximum(qi - 1, 0)
    n_far_pairs = n_far // 2
    m_ref[...] = jnp.full_like(m_ref, NEG_BIG)

    def far_pair(j, carry):
        for c in range(2):
            s = logits(c, j * 2 * bq, 2 * bq)
            m_ref[c] = jnp.maximum(m_ref[c], jnp.maximum(s[:, :bq], s[:, bq:]))
            s_ref[c, 2 * j] = s[:, :bq] + far_bias
            s_ref[c, 2 * j + 1] = s[:, bq:] + far_bias
        return carry

    lax.fori_loop(0, n_far_pairs, far_pair, 0)

    @pl.when(n_far % 2 == 1)
    def _():
        for c in range(2):
            s = logits(c, (n_far - 1) * bq, bq)
            m_ref[c] = jnp.maximum(m_ref[c], s)
            s_ref[c, n_far - 1] = s + far_bias

    for c in range(2):
        m_ref[c] = m_ref[c] + far_bias

    @pl.when(qi >= 1)
    def _():
        for c in range(2):
            s = logits(c, (qi - 1) * bq, bq) + tiles_ref[1, h]
            m_ref[c] = jnp.maximum(m_ref[c], s)
            s_ref[c, qi - 1] = s

    for c in range(2):
        s = jnp.where(causal, logits(c, qi * bq, bq) + tiles_ref[0, h], NEG_BIG)
        s_ref[c, qi] = s
        m_ref[c] = jnp.broadcast_to(jnp.max(jnp.maximum(m_ref[c], s), axis=-1, keepdims=True), (rows, bq))

    @pl.when(qi % 2 == 0)
    def _():
        for c in range(2):
            s_ref[c, qi + 1] = jnp.full((rows, bq), NEG_BIG, F32)

    l_ref[...] = jnp.zeros_like(l_ref)
    acc_ref[...] = jnp.zeros_like(acc_ref)

    def accumulate(j, carry):
        v = v_ref[0, pl.ds(pl.multiple_of(j * 2 * bq, 2 * bq), 2 * bq), :].astype(BF16)
        for c in range(2):
            p0 = jnp.exp(s_ref[c, 2 * j] - m_ref[c])
            p1 = jnp.exp(s_ref[c, 2 * j + 1] - m_ref[c])
            l_ref[c] += p0 + p1
            acc_ref[c] += jnp.dot(jnp.concatenate([p0, p1], axis=1).astype(BF16), v, preferred_element_type=F32)
        return carry

    lax.fori_loop(0, qi // 2 + 1, accumulate, 0)

    lam = lam_ref[0]
    l0 = jnp.sum(l_ref[0], axis=-1, keepdims=True)
    l1 = jnp.sum(l_ref[1], axis=-1, keepdims=True)
    o = acc_ref[0] / l0 - lam * (acc_ref[1] / l1)
    for g in range(GROUP):
        o_ref[0, :, g * vw:(g + 1) * vw] = (_rms(o[g * bq:(g + 1) * bq, :], sub_ref[...]) * out_scale
                                            ).astype(o_ref.dtype)


def _diff_prompt(proj3, rel_bias, lam, subln, *, kv_heads, q_off, k_off, v_off, out_scale):
    b, t, _ = proj3.shape
    bq = ATTN_BLOCK
    qw = GROUP * 2 * HEAD_DIM
    vw = 2 * HEAD_DIM
    width = kv_heads * GROUP * vw
    assert q_off % qw == 0 and k_off % vw == 0 and v_off % vw == 0 and t % (2 * bq) == 0
    assert bq + 1 >= _BUCKET_THR[-1]
    kern = functools.partial(_diff_prompt_kernel, bq=bq, scale=HEAD_DIM ** -0.5, kv_heads=kv_heads,
                             out_scale=out_scale)
    smem = pl.BlockSpec(memory_space=pltpu.SMEM)
    return pl.pallas_call(
        kern,
        out_shape=jax.ShapeDtypeStruct((b, t, width), BF16),
        grid=(b, kv_heads, t // bq),
        in_specs=[smem, smem,
                  pl.BlockSpec((1, bq, qw), lambda bi, h, qi: (bi, qi, q_off // qw + h)),
                  pl.BlockSpec((1, t, vw), lambda bi, h, qi: (bi, 0, k_off // vw + h)),
                  pl.BlockSpec((1, t, vw), lambda bi, h, qi: (bi, 0, v_off // vw + h)),
                  pl.BlockSpec((1, vw), lambda bi, h, qi: (0, 0))],
        out_specs=pl.BlockSpec((1, bq, GROUP * vw), lambda bi, h, qi: (bi, qi, h)),
        scratch_shapes=[pltpu.VMEM((2, kv_heads, GROUP * bq, bq), F32),
                        pltpu.VMEM((2, t // bq, GROUP * bq, bq), F32),
                        pltpu.VMEM((2, GROUP * bq, bq), F32),
                        pltpu.VMEM((2, GROUP * bq, bq), F32),
                        pltpu.VMEM((2, GROUP * bq, vw), F32)],
        compiler_params=_cparams(3), name="diff_prompt",
    )(rel_bias, lam.reshape(1), proj3, proj3, proj3, subln.reshape(1, vw))


def _block_diag(q, n_blocks, block_of_row):
    r = q.shape[0]
    row = lax.broadcasted_iota(jnp.int32, (r, HEAD_DIM), 0)
    blk = block_of_row(row)
    return jnp.concatenate([jnp.where(blk == hb, q, 0.0) for hb in range(n_blocks)], axis=1)


def _split_bf16(x):
    hi = x.astype(BF16)
    lo = (x - hi.astype(F32)).astype(BF16)
    return hi, lo


def _page_lanes(ref, subs):
    n = ref.shape[1] // LANES
    return jnp.concatenate([ref[0, pl.ds(sub, LANES, stride=n), :] for sub in subs], axis=1).astype(BF16)


def _page_specs(n_pages, pages_per_step, width, step_of):
    specs = []
    for n in range(pages_per_step):
        def imap(b, s, pt, n=n):
            return (pt[b * n_pages + step_of(s) * pages_per_step + n], 0, 0)
        specs.append(pl.BlockSpec((1, width, LANES), imap))
    return specs


def _sb_decode_kernel(pt_ref, q_ref, *refs, kv_heads, scale):
    k_refs = refs[:PAGES_PER_STEP]
    v_refs = refs[PAGES_PER_STEP:2 * PAGES_PER_STEP]
    g_ref, o_ref, acc_ref, carry_ref = refs[2 * PAGES_PER_STEP:]
    s = pl.program_id(1)
    heads = kv_heads * GROUP
    width = PAGES_PER_STEP * LANES

    @pl.when(s == 0)
    def _():
        acc_ref[...] = jnp.zeros_like(acc_ref)
        carry_ref[...] = jnp.zeros_like(carry_ref)

    q_bd = _block_diag(q_ref[0], kv_heads, block_of_row=lambda r: r // GROUP).astype(BF16)
    subs = tuple(range(kv_heads))
    z = jnp.concatenate([_nt_dot(q_bd, _page_lanes(k_refs[p], subs)) for p in range(PAGES_PER_STEP)],
                        axis=1) * scale
    sp = _softplus(z)
    log_keep = -sp
    lane = lax.broadcasted_iota(jnp.int32, (heads, width), 1) % LANES
    suffix = log_keep
    sh = 1
    while sh < LANES:
        suffix = suffix + jnp.where(lane + sh < LANES, pltpu.roll(suffix, width - sh, 1), 0.0)
        sh *= 2
    run = carry_ref[...]
    offsets = [None] * PAGES_PER_STEP
    for p in reversed(range(PAGES_PER_STEP)):
        offsets[p] = run
        run = run + jnp.broadcast_to(suffix[:, p * LANES:p * LANES + 1], (heads, LANES))
    carry_ref[...] = run
    between = suffix - log_keep + jnp.concatenate(offsets, axis=1)
    w = jnp.exp(z - sp + between)
    o = acc_ref[...]
    for p in range(PAGES_PER_STEP):
        o = o + jnp.dot(w[:, p * LANES:(p + 1) * LANES].astype(BF16), _page_lanes(v_refs[p], subs),
                        preferred_element_type=F32)
    acc_ref[...] = o

    @pl.when(s == pl.num_programs(1) - 1)
    def _():
        row = lax.broadcasted_iota(jnp.int32, (heads, HEAD_DIM), 0)
        out = jnp.zeros((heads, HEAD_DIM), F32)
        for hb in range(kv_heads):
            out = out + jnp.where(row // GROUP == hb, o[:, hb * HEAD_DIM:(hb + 1) * HEAD_DIM], 0.0)
        o_ref[0] = _rms(out, g_ref[...])


def _sb_decode(q, cache_k, cache_v, page_table, gain, *, kv_heads):
    db, heads, _ = q.shape
    n_pages = page_table.shape[1]
    assert n_pages % PAGES_PER_STEP == 0
    n_steps = n_pages // PAGES_PER_STEP
    width = kv_heads * HEAD_DIM
    kern = functools.partial(_sb_decode_kernel, kv_heads=kv_heads, scale=HEAD_DIM ** -0.5)
    pages = _page_specs(n_pages, PAGES_PER_STEP, width, lambda s: n_steps - 1 - s)
    grid_spec = pltpu.PrefetchScalarGridSpec(
        num_scalar_prefetch=1, grid=(db, n_steps),
        in_specs=[pl.BlockSpec((1, heads, HEAD_DIM), lambda b, s, pt: (b, 0, 0))] + pages + pages
                 + [pl.BlockSpec((heads, HEAD_DIM), lambda b, s, pt: (0, 0))],
        out_specs=pl.BlockSpec((1, heads, HEAD_DIM), lambda b, s, pt: (b, 0, 0)),
        scratch_shapes=[pltpu.VMEM((heads, width), F32), pltpu.VMEM((heads, LANES), F32)])
    return pl.pallas_call(
        kern, out_shape=jax.ShapeDtypeStruct((db, heads, HEAD_DIM), F32), grid_spec=grid_spec,
        compiler_params=_cparams(2), name="sb_decode",
    )(page_table.reshape(-1), q, *([cache_k] * PAGES_PER_STEP), *([cache_v] * PAGES_PER_STEP), gain)


def _diff_decode_kernel(pt_ref, lam_ref, q_ref, kn_ref, vn_ref, rbt_ref, *refs,
                        kv_heads, scale, out_scale, n_steps):
    k_refs = refs[:DIFF_PAGES_PER_STEP]
    v_refs = refs[DIFF_PAGES_PER_STEP:2 * DIFF_PAGES_PER_STEP]
    sub_ref, o_ref, s_ref, m_ref, acc_ref = refs[2 * DIFF_PAGES_PER_STEP:]
    s = pl.program_id(1)
    heads = kv_heads * GROUP
    rows = 2 * heads
    vw = 2 * HEAD_DIM
    lam = lam_ref[0]
    rbt = rbt_ref[...]
    q_bd = _block_diag(q_ref[0], 2 * kv_heads,
                       block_of_row=lambda r: 2 * ((r % heads) // GROUP) + r // heads).astype(BF16)
    k_subs = tuple(range(2 * kv_heads))
    v_subs = tuple(half * kv_heads + hb for hb in range(kv_heads) for half in range(2))

    @pl.when(s == 0)
    def _():
        m_ref[...] = jnp.full_like(m_ref, NEG_BIG)

    @pl.when(s < n_steps)
    def _():
        far = jnp.broadcast_to(rbt[:, N_BUCKETS - 1:N_BUCKETS], (rows, LANES))
        n = LANES - lax.broadcasted_iota(jnp.int32, (rows, LANES), 1)
        near = _bias_from_distance(n, lambda k: jnp.broadcast_to(rbt[:, k:k + 1], (rows, LANES)))
        tiles = []
        for p in range(DIFF_PAGES_PER_STEP):
            bias = jnp.where(s == n_steps - 1, near, far) if p == DIFF_PAGES_PER_STEP - 1 else far
            tiles.append(_nt_dot(q_bd, _page_lanes(k_refs[p], k_subs)) * scale + bias)
        s_ref[s] = jnp.concatenate(tiles, axis=1)
        m_ref[...] = jnp.maximum(m_ref[...], functools.reduce(jnp.maximum, tiles))

    @pl.when(s == n_steps)
    def _():
        k_self = kn_ref[0].astype(BF16).astype(F32)
        s_self = jnp.sum(q_bd.astype(F32) * k_self, axis=-1, keepdims=True) * scale + rbt[:, 0:1]
        m = jnp.maximum(jnp.max(m_ref[...], axis=-1, keepdims=True), s_self)
        e_self = jnp.exp(s_self - m)
        denom = e_self
        for st in range(n_steps):
            e = jnp.exp(s_ref[st] - m)
            s_ref[st] = e
            denom = denom + jnp.sum(e, axis=-1, keepdims=True)
        for st in range(n_steps):
            p = s_ref[st] / denom
            s_ref[st, :heads] = p[:heads] - lam * p[heads:]
        p_self = e_self / denom
        a_self = (p_self[:heads] - lam * p_self[heads:]).astype(BF16).astype(F32)
        acc_ref[...] = a_self * vn_ref[0].astype(BF16).astype(F32)

    @pl.when(s >= n_steps)
    def _():
        attn = s_ref[s - n_steps, :heads].astype(BF16)
        o = acc_ref[...]
        for p in range(DIFF_PAGES_PER_STEP):
            o = o + jnp.dot(attn[:, p * LANES:(p + 1) * LANES], _page_lanes(v_refs[p], v_subs),
                            preferred_element_type=F32)
        acc_ref[...] = o

        @pl.when(s == 2 * n_steps - 1)
        def _():
            row = lax.broadcasted_iota(jnp.int32, (heads, vw), 0)
            out = jnp.zeros((heads, vw), F32)
            for hb in range(kv_heads):
                out = out + jnp.where(row // GROUP == hb, o[:, hb * vw:(hb + 1) * vw], 0.0)
            o_ref[0] = _rms(out, sub_ref[...]) * out_scale


def _diff_decode(q, k_new, v_new, cache_k, cache_v, page_table, rbt, lam, subln, *, kv_heads, out_scale):
    db, rows, _ = q.shape
    heads = rows // 2
    n_pages = page_table.shape[1]
    pps = DIFF_PAGES_PER_STEP
    assert n_pages % pps == 0 and LANES >= MAX_DISTANCE
    n_steps = n_pages // pps
    width = kv_heads * 2 * HEAD_DIM
    vw = 2 * HEAD_DIM
    kern = functools.partial(_diff_decode_kernel, kv_heads=kv_heads, scale=HEAD_DIM ** -0.5, out_scale=out_scale,
                             n_steps=n_steps)
    k_pages = _page_specs(n_pages, pps, width, lambda s: jnp.minimum(s, n_steps - 1))
    v_pages = _page_specs(n_pages, pps, width, lambda s: jnp.maximum(s - n_steps, 0))
    grid_spec = pltpu.PrefetchScalarGridSpec(
        num_scalar_prefetch=1, grid=(db, 2 * n_steps),
        in_specs=[pl.BlockSpec(memory_space=pltpu.SMEM),
                  pl.BlockSpec((1, rows, HEAD_DIM), lambda b, s, pt: (b, 0, 0)),
                  pl.BlockSpec((1, 1, width), lambda b, s, pt: (b, 0, 0)),
                  pl.BlockSpec((1, 1, width), lambda b, s, pt: (b, 0, 0)),
                  pl.BlockSpec((rows, N_BUCKETS), lambda b, s, pt: (0, 0))] + k_pages + v_pages
                 + [pl.BlockSpec((1, vw), lambda b, s, pt: (0, 0))],
        out_specs=pl.BlockSpec((1, heads, vw), lambda b, s, pt: (b, 0, 0)),
        scratch_shapes=[pltpu.VMEM((n_steps, rows, pps * LANES), F32), pltpu.VMEM((rows, LANES), F32),
                        pltpu.VMEM((heads, width), F32)])
    return pl.pallas_call(
        kern, out_shape=jax.ShapeDtypeStruct((db, heads, vw), F32), grid_spec=grid_spec,
        compiler_params=_cparams(2), name="diff_decode",
    )(page_table.reshape(-1), lam.reshape(1), q, k_new, v_new, rbt,
      *([cache_k] * pps), *([cache_v] * pps), subln.reshape(1, vw))


def _router_kernel(x_ref, g_ref, wr_ref, br_ref, h_ref, idx_ref, gate_ref):
    h = _rms(x_ref[...], g_ref[...])
    h_ref[...] = h
    logits = _nt_dot(wr_ref[...].astype(BF16), h.astype(BF16)) + br_ref[...]
    n_exp = logits.shape[0]
    expert = lax.broadcasted_iota(jnp.int32, logits.shape, 0)
    vals, idxs = [], []
    for _ in range(TOP_K):
        top = jnp.max(logits, axis=0, keepdims=True)
        idx = jnp.min(jnp.where(logits == top, expert, n_exp), axis=0, keepdims=True)
        vals.append(top)
        idxs.append(idx)
        logits = jnp.where(expert == idx, -jnp.inf, logits)
    top_val = jnp.concatenate(vals, axis=0)
    e = jnp.exp(top_val - top_val[0:1])
    gate_ref[...] = e / jnp.sum(e, axis=0, keepdims=True)
    idx_ref[...] = jnp.concatenate(idxs, axis=0)


def _router(x, gain, w_router, b_router, tm):
    t, d = x.shape
    n_exp = w_router.shape[1]
    return pl.pallas_call(
        _router_kernel,
        out_shape=(jax.ShapeDtypeStruct((t, d), F32), jax.ShapeDtypeStruct((TOP_K, t), jnp.int32),
                   jax.ShapeDtypeStruct((TOP_K, t), F32)),
        grid=(t // tm,),
        in_specs=[pl.BlockSpec((tm, d), lambda i: (i, 0)), pl.BlockSpec((1, d), lambda i: (0, 0)),
                  pl.BlockSpec((n_exp, d), lambda i: (0, 0)), pl.BlockSpec((n_exp, 1), lambda i: (0, 0))],
        out_specs=(pl.BlockSpec((tm, d), lambda i: (i, 0)), pl.BlockSpec((TOP_K, tm), lambda i: (0, i)),
                   pl.BlockSpec((TOP_K, tm), lambda i: (0, i))),
        compiler_params=_cparams(1), name="router",
    )(x, gain.reshape(1, d), w_router.T, b_router.reshape(n_exp, 1))


def _moe_plan(top_idx, n_exp, zero_token):
    n = top_idx.size
    n_chunks = n_exp + n // MOE_CHUNK + 1
    sub_per_chunk = MOE_CHUNK // MOE_SUB
    n_gather = n_exp + n // MOE_SUB + 1
    flat_e = top_idx.reshape(-1).astype(jnp.int32)
    counts = jnp.bincount(flat_e, length=n_exp).astype(jnp.int32)
    order = jnp.argsort(flat_e, stable=True).astype(jnp.int32)
    position = jnp.argsort(order).astype(jnp.int32)
    group_start = jnp.cumsum(counts) - counts
    chunks_e = (counts + MOE_CHUNK - 1) // MOE_CHUNK
    chunk_end_e = jnp.cumsum(chunks_e)
    chunk_start_e = chunk_end_e - chunks_e
    dest = chunk_start_e[flat_e] * MOE_CHUNK + position - group_start[flat_e]
    n_used = chunk_end_e[-1]
    cidx = jnp.arange(n_chunks, dtype=jnp.int32)
    c_exp = jnp.clip(jnp.searchsorted(chunk_end_e, cidx, side="right"), 0, n_exp - 1).astype(jnp.int32)
    c_cnt = jnp.clip(counts[c_exp] - (cidx - chunk_start_e[c_exp]) * MOE_CHUNK, 0, MOE_CHUNK)
    used = cidx < n_used
    c_exp = jnp.where(used, c_exp, c_exp[n_used - 1])
    c_cnt = jnp.where(used, c_cnt, 0).astype(jnp.int32)
    c_blk = jnp.where(used, cidx, n_used - 1).astype(jnp.int32)
    subs_c = (c_cnt + MOE_SUB - 1) // MOE_SUB
    sub_end = jnp.cumsum(subs_c)
    gidx = jnp.minimum(jnp.arange(n_gather, dtype=jnp.int32), sub_end[-1] - 1)
    g_chunk = jnp.clip(jnp.searchsorted(sub_end, gidx, side="right"), 0, n_chunks - 1).astype(jnp.int32)
    g_sub = gidx - (sub_end - subs_c)[g_chunk]
    g_dst = (g_chunk * sub_per_chunk + g_sub).astype(jnp.int32)
    g_exp = c_exp[g_chunk]
    in_expert = ((g_chunk - chunk_start_e[g_exp]) * MOE_CHUNK + g_sub * MOE_SUB)[:, None] \
        + jnp.arange(MOE_SUB, dtype=jnp.int32)[None, :]
    live = in_expert < counts[g_exp][:, None]
    element = order[jnp.clip(group_start[g_exp][:, None] + in_expert, 0, n - 1)]
    g_tok = jnp.where(live, element // TOP_K, zero_token).astype(jnp.int32).reshape(-1)
    return dict(n_chunks=n_chunks, dest=dest.astype(jnp.int32), c_exp=c_exp, c_cnt=c_cnt, c_blk=c_blk,
                n_used=n_used.reshape(1).astype(jnp.int32), g_dst=g_dst, g_tok=g_tok)


def _row_copy(src_hbm, row, dst_vmem, slot, sem):
    return pltpu.make_async_copy(src_hbm.at[pl.ds(row, 1), :], dst_vmem.at[pl.ds(slot, 1), :], sem)


def _moe_gather_kernel(tok_ref, dst_ref, h_hbm, o_ref, buf, sem):
    base = pl.program_id(0) * MOE_SUB

    def issue(i, carry):
        for u in range(DMA_THREADS):
            r = i * DMA_THREADS + u
            _row_copy(h_hbm, tok_ref[base + r], buf, r, sem).start(priority=u)
        return carry

    def drain(r, carry):
        _row_copy(h_hbm, 0, buf, r, sem).wait()
        return carry

    lax.fori_loop(0, MOE_SUB // DMA_THREADS, issue, 0, unroll=4)
    lax.fori_loop(0, MOE_SUB, drain, 0, unroll=8)
    o_ref[...] = buf[...].astype(o_ref.dtype)


def _moe_gather(h_all, plan):
    d = h_all.shape[1]
    n_gather = plan["g_dst"].shape[0]
    grid_spec = pltpu.PrefetchScalarGridSpec(
        num_scalar_prefetch=2, grid=(n_gather,),
        in_specs=[pl.BlockSpec(memory_space=pl.ANY)],
        out_specs=pl.BlockSpec((MOE_SUB, d), lambda i, tok, dst: (dst[i], 0)),
        scratch_shapes=[pltpu.VMEM((MOE_SUB, d), F32), pltpu.SemaphoreType.DMA])
    return pl.pallas_call(
        _moe_gather_kernel, out_shape=jax.ShapeDtypeStruct((plan["n_chunks"] * MOE_CHUNK, d), BF16),
        grid_spec=grid_spec, compiler_params=_cparams(1), name="moe_gather",
    )(plan["g_tok"], plan["g_dst"], h_all)


def _moe_matmul_kernel(ce_ref, cnt_ref, blk_ref, nu_ref, x_ref, *refs, gated):
    if gated:
        wg_ref, bg_ref, wu_ref, bu_ref, o_ref, wgb_ref, wub_ref = refs
    else:
        wg_ref, bg_ref, o_ref, wgb_ref = refs
    c = pl.program_id(0)
    cnt = cnt_ref[c]

    def cast_weights():
        wgb_ref[...] = wg_ref[0].astype(BF16)
        if gated:
            wub_ref[...] = wu_ref[0].astype(BF16)

    def rows_block(start, size):
        rows = pl.ds(pl.multiple_of(start, MOE_TAIL), size)
        xs = x_ref[rows, :]
        y = jnp.dot(xs, wgb_ref[...], preferred_element_type=F32) + bg_ref[0]
        if gated:
            up = jnp.dot(xs, wub_ref[...], preferred_element_type=F32) + bu_ref[0]
            gate = jnp.minimum(y, SWIGLU_LIMIT)
            up = jnp.clip(up, -SWIGLU_LIMIT, SWIGLU_LIMIT)
            y = (up + 1.0) * (gate * jax.nn.sigmoid(SWIGLU_ALPHA * gate))
        o_ref[rows, :] = y.astype(o_ref.dtype)

    units = (cnt + MOE_TAIL - 1) // MOE_TAIL
    top = MOE_BLOCKS[0]
    assert MOE_CHUNK < 2 * top
    has_top = units >= top // MOE_TAIL

    @pl.when(has_top)
    def _():
        cast_weights()
        rows_block(0, top)

    @pl.when(jnp.logical_and(jnp.logical_not(has_top), cnt > 0))
    def _():
        cast_weights()

    start = jnp.where(has_top, top, 0)
    units = units - start // MOE_TAIL
    for size in MOE_BLOCKS[1:]:
        take = units >= size // MOE_TAIL

        @pl.when(take)
        def _(start=start, size=size):
            rows_block(start, size)

        start = start + jnp.where(take, size, 0)
        units = units - jnp.where(take, size // MOE_TAIL, 0)


def _moe_matmul(x_rows, plan, weights, biases, out_dtype):
    gated = len(weights) == 2
    n_exp, k, n = weights[0].shape
    n_chunks = plan["n_chunks"]
    nj = n // MOE_TN

    def col(c, j, nu):
        return jnp.where(c < nu[0], j, nj - 1)

    x_spec = pl.BlockSpec((MOE_CHUNK, k), lambda c, j, ce, cnt, blk, nu: (blk[c], 0))
    w_spec = pl.BlockSpec((1, k, MOE_TN), lambda c, j, ce, cnt, blk, nu: (ce[c], 0, col(c, j, nu)))
    b_spec = pl.BlockSpec((1, 1, MOE_TN), lambda c, j, ce, cnt, blk, nu: (ce[c], 0, col(c, j, nu)))
    operands, in_specs = [x_rows], [x_spec]
    for w, b in zip(weights, biases):
        operands += [w, b.reshape(n_exp, 1, n)]
        in_specs += [w_spec, b_spec]
    grid_spec = pltpu.PrefetchScalarGridSpec(
        num_scalar_prefetch=4, grid=(n_chunks, nj), in_specs=in_specs,
        out_specs=pl.BlockSpec((MOE_CHUNK, MOE_TN), lambda c, j, ce, cnt, blk, nu: (blk[c], col(c, j, nu))),
        scratch_shapes=[pltpu.VMEM((k, MOE_TN), BF16)] * len(weights))
    return pl.pallas_call(
        functools.partial(_moe_matmul_kernel, gated=gated),
        out_shape=jax.ShapeDtypeStruct((n_chunks * MOE_CHUNK, n), out_dtype), grid_spec=grid_spec,
        compiler_params=_cparams(2), name="moe_up" if gated else "moe_down",
    )(plan["c_exp"], plan["c_cnt"], plan["c_blk"], plan["n_used"], *operands)


def _moe_combine_kernel(dest_ref, y_hbm, x_ref, g_ref, o_ref, buf, sem, *, tb):
    base = pl.program_id(0) * tb * TOP_K

    def issue(r, carry):
        for k in range(TOP_K):
            _row_copy(y_hbm, dest_ref[base + r * TOP_K + k], buf.at[k], r, sem).start(priority=k % DMA_THREADS)
        return carry

    def drain(r, carry):
        for k in range(TOP_K):
            _row_copy(y_hbm, 0, buf.at[k], r, sem).wait()
        return carry

    lax.fori_loop(0, tb, issue, 0, unroll=4)
    lax.fori_loop(0, tb, drain, 0, unroll=4)
    gates = g_ref[...]
    moe = gates[:, 0:1] * buf[0]
    for k in range(1, TOP_K):
        moe = moe + gates[:, k:k + 1] * buf[k]
    o_ref[...] = x_ref[...] + moe


def _moe_combine(y_rows, dest, x, gates, tb):
    t, d = x.shape
    grid_spec = pltpu.PrefetchScalarGridSpec(
        num_scalar_prefetch=1, grid=(t // tb,),
        in_specs=[pl.BlockSpec(memory_space=pl.ANY),
                  pl.BlockSpec((tb, d), lambda i, dest: (i, 0)),
                  pl.BlockSpec((tb, TOP_K), lambda i, dest: (i, 0))],
        out_specs=pl.BlockSpec((tb, d), lambda i, dest: (i, 0)),
        scratch_shapes=[pltpu.VMEM((TOP_K, tb, d), F32), pltpu.SemaphoreType.DMA])
    return pl.pallas_call(
        functools.partial(_moe_combine_kernel, tb=tb), out_shape=jax.ShapeDtypeStruct((t, d), F32),
        grid_spec=grid_spec, compiler_params=_cparams(1), name="moe_combine",
    )(dest, y_rows, x, gates)


def _pick_tile(n, candidates):
    for c in candidates:
        if n % c == 0:
            return c
    return n


def kernel(x_prompt, x_sample, cache_sb_k, cache_sb_v, cache_diff_k, cache_diff_v, page_table, attn_norm, w_in, diff_q_norm, diff_k_norm, diff_lambda_q1, diff_lambda_k1, diff_lambda_q2, diff_lambda_k2, rel_bias, sb_out_norm, diff_subln, w_out, ffn_norm, w_router, b_router, w_gate, b_gate, w_up, b_up, w_down, b_down):
    depth = attn_norm.shape[0]
    assert depth == 1, "single-layer trunk"
    bsz, seq, d = x_prompt.shape
    db, dseq, _ = x_sample.shape
    assert dseq == 1
    n_exp = w_router.shape[2]
    half = d // 2
    sb_heads = half // HEAD_DIM
    sb_kv = sb_heads // GROUP
    diff_heads = half // (2 * HEAD_DIM)
    diff_kv = diff_heads // GROUP
    sbq, sbk = sb_heads * HEAD_DIM, sb_kv * HEAD_DIM
    dq, dk = diff_heads * 2 * HEAD_DIM, diff_kv * 2 * HEAD_DIM
    off_sbk, off_sbv = sbq, sbq + sbk
    off_dq = sbq + 2 * sbk
    off_dk = off_dq + dq
    off_dv = off_dk + dk
    in_cols = off_dv + dk
    layer = 0
    lambda_init = 0.8 - 0.6 * math.exp(-0.3 * layer)
    lam = (jnp.exp(jnp.sum(diff_lambda_q1[layer] * diff_lambda_k1[layer]))
           - jnp.exp(jnp.sum(diff_lambda_q2[layer] * diff_lambda_k2[layer])) + lambda_init).astype(F32)

    tn_in = 512
    assert off_dq % tn_in == 0 and off_dv % tn_in == 0
    qk_gain = jnp.concatenate([jnp.ones((off_dq,), F32), jnp.tile(diff_q_norm[layer], dq // HEAD_DIM),
                               jnp.tile(diff_k_norm[layer], dk // HEAD_DIM), jnp.ones((dk,), F32)]).reshape(1, in_cols)
    norm_tiles = dict(norm_lo=off_dq // tn_in, norm_hi=off_dv // tn_in)
    n_tok = bsz * seq

    xp = x_prompt.reshape(n_tok, d)
    h_p = _rmsnorm(xp, attn_norm[layer], BF16, _pick_tile(n_tok, (256, 128, 8)))
    tm_p = _pick_tile(n_tok, (1024, 512, 256, 128, 8))
    proj_p = _matmul(h_p, w_in[layer], qk_gain, mode="qknorm", tm=tm_p, tn=tn_in, **norm_tiles)
    proj3 = proj_p.reshape(bsz, seq, in_cols)
    mix_sb = _sb_prompt(proj3, sb_out_norm[layer].reshape(-1), kv_heads=sb_kv,
                        q_off=0, k_off=off_sbk, v_off=off_sbv)
    mix_d = _diff_prompt(proj3, rel_bias, lam, diff_subln[layer], kv_heads=diff_kv,
                         q_off=off_dq, k_off=off_dk, v_off=off_dv, out_scale=1.0 - lambda_init)
    mix_p = jnp.concatenate([mix_sb, mix_d], axis=-1).reshape(n_tok, d)
    x2_p = _matmul(mix_p, w_out[layer], xp, mode="residual", tm=tm_p, tn=512)

    xs = x_sample.reshape(db, d)
    h_s = _rmsnorm(xs, attn_norm[layer], BF16, db)
    proj_s = _matmul(h_s, w_in[layer], qk_gain, mode="qknorm", tm=db, tn=tn_in, **norm_tiles)
    pool = cache_sb_k.shape[1]
    page = cache_sb_k.shape[2]
    assert page == LANES
    rows4 = (pool, page * (sbk // HEAD_DIM), HEAD_DIM)
    assert sbk == dk == 4 * HEAD_DIM
    sb_o = _sb_decode(proj_s[:, :sbq].reshape(db, sb_heads, HEAD_DIM),
                      cache_sb_k.reshape(rows4), cache_sb_v.reshape(rows4),
                      page_table, sb_out_norm[layer], kv_heads=sb_kv)
    diff_v_rows = cache_diff_v.reshape(pool, page, diff_kv, 2, HEAD_DIM).transpose(0, 1, 3, 2, 4).reshape(rows4)
    q_d = proj_s[:, off_dq:off_dk].reshape(db, diff_heads, 2, HEAD_DIM).transpose(0, 2, 1, 3)
    rbt = jnp.concatenate([rel_bias.T, rel_bias.T], axis=0).astype(F32)
    d_o = _diff_decode(q_d.reshape(db, 2 * diff_heads, HEAD_DIM),
                       proj_s[:, off_dk:off_dv].reshape(db, 1, dk), proj_s[:, off_dv:].reshape(db, 1, dk),
                       cache_diff_k.reshape(rows4), diff_v_rows,
                       page_table, rbt, lam, diff_subln[layer], kv_heads=diff_kv, out_scale=1.0 - lambda_init)
    mix_s = jnp.concatenate([sb_o.reshape(db, sbq), d_o.reshape(db, dq)], axis=-1).astype(BF16)
    x2_s = _matmul(mix_s, w_out[layer], xs, mode="residual", tm=db, tn=512)

    h2_p, idx_p, gate_p = _router(x2_p, ffn_norm[layer], w_router[layer], b_router[layer],
                                  _pick_tile(n_tok, (256, 128)))
    h2_s, idx_s, gate_s = _router(x2_s, ffn_norm[layer], w_router[layer], b_router[layer], db)
    n_all = n_tok + db
    h_all = jnp.concatenate([h2_p, h2_s, jnp.zeros((8, d), F32)], axis=0)
    top_idx = jnp.concatenate([idx_p, idx_s], axis=1).T
    plan = _moe_plan(top_idx, n_exp, zero_token=n_all)
    x_rows = _moe_gather(h_all, plan)
    act = _moe_matmul(x_rows, plan, (w_gate[layer], w_up[layer]), (b_gate[layer], b_up[layer]), BF16)
    y_rows = _moe_matmul(act, plan, (w_down[layer],), (b_down[layer],), F32)
    dest = plan["dest"]
    y_p = _moe_combine(y_rows, dest[:n_tok * TOP_K], x2_p, gate_p.T, _pick_tile(n_tok, (COMBINE_TOKENS, 8)))
    y_s = _moe_combine(y_rows, dest[n_tok * TOP_K:], x2_s, gate_s.T, db)

    def rows(p, lead, lo, hi, shape):
        return p[:, lo:hi].reshape((depth,) + lead + shape)

    lead_p, lead_s = (bsz, seq), (db, dseq)
    return (y_p.reshape(bsz, seq, d), y_s.reshape(db, dseq, d),
            rows(proj_p, lead_p, off_sbk, off_sbv, (sb_kv, HEAD_DIM)),
            rows(proj_p, lead_p, off_sbv, off_dq, (sb_kv, HEAD_DIM)),
            rows(proj_p, lead_p, off_dk, off_dv, (diff_kv, 2, HEAD_DIM)),
            rows(proj_p, lead_p, off_dv, in_cols, (diff_kv, 2 * HEAD_DIM)),
            rows(proj_s, lead_s, off_sbk, off_sbv, (sb_kv, HEAD_DIM)),
            rows(proj_s, lead_s, off_sbv, off_dq, (sb_kv, HEAD_DIM)),
            rows(proj_s, lead_s, off_dk, off_dv, (diff_kv, 2, HEAD_DIM)),
            rows(proj_s, lead_s, off_dv, in_cols, (diff_kv, 2 * HEAD_DIM)))
```

```python
import functools
import math

import numpy as np
import jax
import jax.numpy as jnp
from jax import lax
from jax.experimental import pallas as pl
from jax.experimental.pallas import tpu as pltpu

F32 = jnp.float32
BF16 = jnp.bfloat16

HEAD_DIM = 128
GROUP = 4
N_BUCKETS = 32
MAX_EXACT = N_BUCKETS // 2
MAX_DISTANCE = 128
TOP_K = 4
SWIGLU_LIMIT = 7.0
SWIGLU_ALPHA = 1.702
EPS = 1e-5
NEG_BIG = -1e30

V7X_VMEM_LIMIT_BYTES = 56 * 1024 * 1024
LANES = 128
ATTN_BLOCK = 128
PAGES_PER_STEP = 8
DIFF_PAGES_PER_STEP = 16
MOE_TAIL = 128
MOE_BLOCKS = (1024, 512, 256, 128)
MOE_CHUNK = 1280
BF16_ROWS = 16
MOE_TN = 256
COMBINE_TOKENS = 64
DMA_THREADS = 2


def _cparams(n_axes):
    return pltpu.CompilerParams(dimension_semantics=("arbitrary",) * n_axes,
                                vmem_limit_bytes=V7X_VMEM_LIMIT_BYTES)


def _nt_dot(a, b, precision=None):
    return lax.dot_general(a, b, (((1,), (1,)), ((), ())), precision=precision,
                           preferred_element_type=F32)


def _rms(x, gain):
    ms = jnp.mean(x * x, axis=-1, keepdims=True)
    return x * lax.rsqrt(ms + EPS) * gain


def _bucket_thresholds():
    n = np.arange(MAX_DISTANCE + 1)
    nf = np.maximum(n, MAX_EXACT).astype(np.float32)
    large = MAX_EXACT + (np.log(nf / np.float32(MAX_EXACT)) / np.float32(math.log(MAX_DISTANCE / MAX_EXACT))
                         * np.float32(N_BUCKETS - MAX_EXACT)).astype(np.int32)
    large = np.minimum(large, N_BUCKETS - 1)
    bucket = np.where(n < MAX_EXACT, n, large)
    return [int(np.argmax(bucket >= k)) for k in range(1, N_BUCKETS)]


_BUCKET_THR = _bucket_thresholds()


def _rmsnorm_kernel(x_ref, g_ref, o_ref):
    o_ref[...] = _rms(x_ref[...], g_ref[...]).astype(o_ref.dtype)


def _rmsnorm(x, gain, out_dtype, tm):
    t, d = x.shape
    return pl.pallas_call(
        _rmsnorm_kernel,
        out_shape=jax.ShapeDtypeStruct((t, d), out_dtype),
        grid=(t // tm,),
        in_specs=[pl.BlockSpec((tm, d), lambda i: (i, 0)), pl.BlockSpec((1, d), lambda i: (0, 0))],
        out_specs=pl.BlockSpec((tm, d), lambda i: (i, 0)),
        compiler_params=_cparams(1), name="rmsnorm",
    )(x, gain.reshape(1, d))


def _matmul_kernel(*refs, n_parts, mode, norm_lo, norm_hi, tn):
    a_refs = refs[:n_parts]
    w_ref, e_ref, o_ref, wb_ref = refs[n_parts:]
    j = pl.program_id(0)
    i = pl.program_id(1)

    @pl.when(i == 0)
    def _():
        wb_ref[...] = w_ref[...].astype(BF16)

    acc, k_lo = None, 0
    for a_ref in a_refs:
        k_hi = k_lo + a_ref.shape[1]
        part = jnp.dot(a_ref[...], wb_ref[k_lo:k_hi, :], preferred_element_type=F32)
        acc = part if acc is None else acc + part
        k_lo = k_hi
    if mode == "residual":
        o_ref[...] = e_ref[...] + acc
    else:
        in_range = jnp.logical_and(j >= norm_lo, j < norm_hi)

        @pl.when(in_range)
        def _():
            for c in range(tn // LANES):
                sl = slice(c * LANES, (c + 1) * LANES)
                o_ref[:, sl] = _rms(acc[:, sl], e_ref[:, sl])

        @pl.when(jnp.logical_not(in_range))
        def _():
            o_ref[...] = acc


def _matmul(a_parts, w, extra, *, mode, tm, tn, norm_lo=0, norm_hi=0):
    m = a_parts[0].shape[0]
    k, n = w.shape
    assert sum(a.shape[1] for a in a_parts) == k
    if mode == "residual":
        e_spec = pl.BlockSpec((tm, tn), lambda j, i: (i, j))
    else:
        e_spec = pl.BlockSpec((1, tn), lambda j, i: (0, j))
    kern = functools.partial(_matmul_kernel, n_parts=len(a_parts), mode=mode, norm_lo=norm_lo, norm_hi=norm_hi,
                             tn=tn)
    return pl.pallas_call(
        kern,
        out_shape=jax.ShapeDtypeStruct((m, n), F32),
        grid=(n // tn, m // tm),
        in_specs=[pl.BlockSpec((tm, a.shape[1]), lambda j, i: (i, 0)) for a in a_parts]
                 + [pl.BlockSpec((k, tn), lambda j, i: (0, j)), e_spec],
        out_specs=pl.BlockSpec((tm, tn), lambda j, i: (i, j)),
        scratch_shapes=[pltpu.VMEM((k, tn), BF16)],
        compiler_params=_cparams(2), name="matmul_" + mode,
    )(*a_parts, w, extra)


def _softplus(z):
    return jnp.maximum(z, 0.0) + jnp.log(1.0 + jnp.exp(-jnp.abs(z)))


def _sb_prompt_kernel(q_ref, k_ref, v_ref, g_ref, o_ref, acc_ref, *, bq, scale):
    qi = pl.program_id(2)
    rows = GROUP * bq
    q = jnp.concatenate([q_ref[0, :, g * HEAD_DIM:(g + 1) * HEAD_DIM] for g in range(GROUP)], axis=0)
    qs = (q * scale).astype(BF16)
    row_i = lax.broadcasted_iota(jnp.int32, (bq, bq), 0)
    col_i = lax.broadcasted_iota(jnp.int32, (bq, bq), 1)
    later = jnp.where(row_i > col_i, 1.0, 0.0).astype(BF16)
    visible = jnp.concatenate([col_i < row_i] * GROUP, axis=0)

    later2 = jnp.concatenate([later, later], axis=0)

    def weights(z, carry, masked):
        sp = _softplus(z)
        log_keep = jnp.where(visible, -sp, 0.0) if masked else -sp
        hi, lo = _split_bf16(log_keep)
        between = jnp.dot(jnp.concatenate([hi, lo], axis=1), later2, preferred_element_type=F32) + carry
        w = jnp.exp(z - sp + between)
        if masked:
            w = jnp.where(visible, w, 0.0)
        return w.astype(BF16), carry + jnp.sum(log_keep, axis=-1, keepdims=True)

    def single(kb, carry, masked):
        start = pl.multiple_of(kb * bq, bq)
        k = k_ref[0, pl.ds(start, bq), :].astype(BF16)
        v = v_ref[0, pl.ds(start, bq), :].astype(BF16)
        w, carry = weights(_nt_dot(qs, k), carry, masked)
        acc_ref[...] += jnp.dot(w, v, preferred_element_type=F32)
        return carry

    n_pairs = qi // 2

    def pair(t, carry):
        start = pl.multiple_of((2 * (n_pairs - 1 - t)) * bq, bq)
        k = k_ref[0, pl.ds(start, 2 * bq), :].astype(BF16)
        v = v_ref[0, pl.ds(start, 2 * bq), :].astype(BF16)
        z = _nt_dot(qs, k)
        w_late, carry = weights(z[:, bq:], carry, False)
        w_early, carry = weights(z[:, :bq], carry, False)
        acc_ref[...] += jnp.dot(jnp.concatenate([w_early, w_late], axis=1), v, preferred_element_type=F32)
        return carry

    acc_ref[...] = jnp.zeros_like(acc_ref)
    carry = single(qi, jnp.zeros((rows, 1), F32), True)
    carry = lax.cond(qi % 2 == 1, lambda c: single(qi - 1, c, False), lambda c: c, carry)
    lax.fori_loop(0, n_pairs, pair, carry)

    acc = acc_ref[...]
    for g in range(GROUP):
        sl = slice(g * HEAD_DIM, (g + 1) * HEAD_DIM)
        o_ref[0, :, sl] = _rms(acc[g * bq:(g + 1) * bq, :], g_ref[:, sl]).astype(o_ref.dtype)


def _sb_prompt(proj3, gain, *, kv_heads, q_off, k_off, v_off):
    b, t, _ = proj3.shape
    bq = ATTN_BLOCK
    qw = GROUP * HEAD_DIM
    width = kv_heads * qw
    assert q_off % qw == 0 and k_off % HEAD_DIM == 0 and v_off % HEAD_DIM == 0 and t % bq == 0
    kern = functools.partial(_sb_prompt_kernel, bq=bq, scale=HEAD_DIM ** -0.5)
    return pl.pallas_call(
        kern,
        out_shape=jax.ShapeDtypeStruct((b, t, width), BF16),
        grid=(b, kv_heads, t // bq),
        in_specs=[pl.BlockSpec((1, bq, qw), lambda bi, h, qi: (bi, qi, q_off // qw + h)),
                  pl.BlockSpec((1, t, HEAD_DIM), lambda bi, h, qi: (bi, 0, k_off // HEAD_DIM + h)),
                  pl.BlockSpec((1, t, HEAD_DIM), lambda bi, h, qi: (bi, 0, v_off // HEAD_DIM + h)),
                  pl.BlockSpec((1, qw), lambda bi, h, qi: (0, h))],
        out_specs=pl.BlockSpec((1, bq, qw), lambda bi, h, qi: (bi, qi, h)),
        scratch_shapes=[pltpu.VMEM((GROUP * bq, HEAD_DIM), F32)],
        compiler_params=_cparams(3), name="sb_prompt",
    )(proj3, proj3, proj3, gain.reshape(1, width))


def _bias_from_distance(n, table):
    bias = table(0)
    for k in range(1, N_BUCKETS):
        bias = jnp.where(n >= _BUCKET_THR[k - 1], table(k), bias)
    return bias


def _diff_prompt_kernel(rb_ref, lam_ref, q_ref, k_ref, v_ref, sub_ref, o_ref,
                        tiles_ref, s_ref, m_ref, l_ref, acc_ref, *, bq, scale, kv_heads, out_scale):
    bi = pl.program_id(0)
    h = pl.program_id(1)
    qi = pl.program_id(2)
    rows = GROUP * bq
    vw = 2 * HEAD_DIM
    row_i = lax.broadcasted_iota(jnp.int32, (bq, bq), 0)
    col_i = lax.broadcasted_iota(jnp.int32, (bq, bq), 1)

    @pl.when(jnp.logical_and(jnp.logical_and(bi == 0, h == 0), qi == 0))
    def _():
        for kind in range(2):
            n = jnp.maximum(row_i - col_i + kind * bq, 0)
            for hh in range(kv_heads):
                for g in range(GROUP):
                    head = hh * GROUP + g
                    tiles_ref[kind, hh, g * bq:(g + 1) * bq, :] = _bias_from_distance(
                        n, lambda k, head=head: rb_ref[k, head])

    far_bias = jnp.concatenate(
        [jnp.full((bq, 1), rb_ref[N_BUCKETS - 1, h * GROUP + g], F32) for g in range(GROUP)], axis=0)
    causal = jnp.concatenate([col_i <= row_i] * GROUP, axis=0)
    qs = []
    for c in range(2):
        qc = jnp.concatenate([q_ref[0, :, (2 * g + c) * HEAD_DIM:(2 * g + c + 1) * HEAD_DIM]
                              for g in range(GROUP)], axis=0)
        qs.append((qc * scale).astype(BF16))

    def logits(c, start, width):
        k = k_ref[0, pl.ds(pl.multiple_of(start, bq), width), c * HEAD_DIM:(c + 1) * HEAD_DIM].astype(BF16)
        return _nt_dot(qs[c], k)

    n_far = jnp.maximum(qi - 1, 0)
    n_far_pairs = n_far // 2
    m_ref[...] = jnp.full_like(m_ref, NEG_BIG)

    def far_pair(j, carry):
        for c in range(2):
            s = logits(c, j * 2 * bq, 2 * bq)
            m_ref[c] = jnp.maximum(m_ref[c], jnp.maximum(s[:, :bq], s[:, bq:]))
            s_ref[c, 2 * j] = s[:, :bq] + far_bias
            s_ref[c, 2 * j + 1] = s[:, bq:] + far_bias
        return carry

    lax.fori_loop(0, n_far_pairs, far_pair, 0)

    @pl.when(n_far % 2 == 1)
    def _():
        for c in range(2):
            s = logits(c, (n_far - 1) * bq, bq)
            m_ref[c] = jnp.maximum(m_ref[c], s)
            s_ref[c, n_far - 1] = s + far_bias

    for c in range(2):
        m_ref[c] = m_ref[c] + far_bias

    @pl.when(qi >= 1)
    def _():
        for c in range(2):
            s = logits(c, (qi - 1) * bq, bq) + tiles_ref[1, h]
            m_ref[c] = jnp.maximum(m_ref[c], s)
            s_ref[c, qi - 1] = s

    for c in range(2):
        s = jnp.where(causal, logits(c, qi * bq, bq) + tiles_ref[0, h], NEG_BIG)
        s_ref[c, qi] = s
        m_ref[c] = jnp.broadcast_to(jnp.max(jnp.maximum(m_ref[c], s), axis=-1, keepdims=True), (rows, bq))

    @pl.when(qi % 2 == 0)
    def _():
        for c in range(2):
            s_ref[c, qi + 1] = jnp.full((rows, bq), NEG_BIG, F32)

    l_ref[...] = jnp.zeros_like(l_ref)
    acc_ref[...] = jnp.zeros_like(acc_ref)

    def accumulate(j, carry):
        v = v_ref[0, pl.ds(pl.multiple_of(j * 2 * bq, 2 * bq), 2 * bq), :].astype(BF16)
        for c in range(2):
            p0 = jnp.exp(s_ref[c, 2 * j] - m_ref[c])
            p1 = jnp.exp(s_ref[c, 2 * j + 1] - m_ref[c])
            l_ref[c] += p0 + p1
            acc_ref[c] += jnp.dot(jnp.concatenate([p0, p1], axis=1).astype(BF16), v, preferred_element_type=F32)
        return carry

    lax.fori_loop(0, qi // 2 + 1, accumulate, 0)

    lam = lam_ref[0]
    l0 = jnp.sum(l_ref[0], axis=-1, keepdims=True)
    l1 = jnp.sum(l_ref[1], axis=-1, keepdims=True)
    o = acc_ref[0] / l0 - lam * (acc_ref[1] / l1)
    for g in range(GROUP):
        o_ref[0, :, g * vw:(g + 1) * vw] = (_rms(o[g * bq:(g + 1) * bq, :], sub_ref[...]) * out_scale
                                            ).astype(o_ref.dtype)


def _diff_prompt(proj3, rel_bias, lam, subln, *, kv_heads, q_off, k_off, v_off, out_scale):
    b, t, _ = proj3.shape
    bq = ATTN_BLOCK
    qw = GROUP * 2 * HEAD_DIM
    vw = 2 * HEAD_DIM
    width = kv_heads * GROUP * vw
    assert q_off % qw == 0 and k_off % vw == 0 and v_off % vw == 0 and t % (2 * bq) == 0
    assert bq + 1 >= _BUCKET_THR[-1]
    kern = functools.partial(_diff_prompt_kernel, bq=bq, scale=HEAD_DIM ** -0.5, kv_heads=kv_heads,
                             out_scale=out_scale)
    smem = pl.BlockSpec(memory_space=pltpu.SMEM)
    return pl.pallas_call(
        kern,
        out_shape=jax.ShapeDtypeStruct((b, t, width), BF16),
        grid=(b, kv_heads, t // bq),
        in_specs=[smem, smem,
                  pl.BlockSpec((1, bq, qw), lambda bi, h, qi: (bi, qi, q_off // qw + h)),
                  pl.BlockSpec((1, t, vw), lambda bi, h, qi: (bi, 0, k_off // vw + h)),
                  pl.BlockSpec((1, t, vw), lambda bi, h, qi: (bi, 0, v_off // vw + h)),
                  pl.BlockSpec((1, vw), lambda bi, h, qi: (0, 0))],
        out_specs=pl.BlockSpec((1, bq, GROUP * vw), lambda bi, h, qi: (bi, qi, h)),
        scratch_shapes=[pltpu.VMEM((2, kv_heads, GROUP * bq, bq), F32),
                        pltpu.VMEM((2, t // bq, GROUP * bq, bq), F32),
                        pltpu.VMEM((2, GROUP * bq, bq), F32),
                        pltpu.VMEM((2, GROUP * bq, bq), F32),
                        pltpu.VMEM((2, GROUP * bq, vw), F32)],
        compiler_params=_cparams(3), name="diff_prompt",
    )(rel_bias, lam.reshape(1), proj3, proj3, proj3, subln.reshape(1, vw))


def _block_diag(q, n_blocks, block_of_row):
    r = q.shape[0]
    row = lax.broadcasted_iota(jnp.int32, (r, HEAD_DIM), 0)
    blk = block_of_row(row)
    return jnp.concatenate([jnp.where(blk == hb, q, 0.0) for hb in range(n_blocks)], axis=1)


def _split_bf16(x):
    hi = x.astype(BF16)
    lo = (x - hi.astype(F32)).astype(BF16)
    return hi, lo


def _page_lanes(ref, subs):
    n = ref.shape[1] // LANES
    return jnp.concatenate([ref[0, pl.ds(sub, LANES, stride=n), :] for sub in subs], axis=1).astype(BF16)


def _page_specs(n_pages, pages_per_step, width, step_of):
    specs = []
    for n in range(pages_per_step):
        def imap(b, s, pt, n=n):
            return (pt[b * n_pages + step_of(s) * pages_per_step + n], 0, 0)
        specs.append(pl.BlockSpec((1, width, LANES), imap))
    return specs


def _sb_decode_kernel(pt_ref, q_ref, *refs, kv_heads, scale):
    k_refs = refs[:PAGES_PER_STEP]
    v_refs = refs[PAGES_PER_STEP:2 * PAGES_PER_STEP]
    g_ref, o_ref, acc_ref, carry_ref = refs[2 * PAGES_PER_STEP:]
    s = pl.program_id(1)
    heads = kv_heads * GROUP
    width = PAGES_PER_STEP * LANES

    @pl.when(s == 0)
    def _():
        acc_ref[...] = jnp.zeros_like(acc_ref)
        carry_ref[...] = jnp.zeros_like(carry_ref)

    q_bd = _block_diag(q_ref[0], kv_heads, block_of_row=lambda r: r // GROUP).astype(BF16)
    subs = tuple(range(kv_heads))
    z = jnp.concatenate([_nt_dot(q_bd, _page_lanes(k_refs[p], subs)) for p in range(PAGES_PER_STEP)],
                        axis=1) * scale
    sp = _softplus(z)
    log_keep = -sp
    lane = lax.broadcasted_iota(jnp.int32, (heads, width), 1) % LANES
    suffix = log_keep
    sh = 1
    while sh < LANES:
        suffix = suffix + jnp.where(lane + sh < LANES, pltpu.roll(suffix, width - sh, 1), 0.0)
        sh *= 2
    run = carry_ref[...]
    offsets = [None] * PAGES_PER_STEP
    for p in reversed(range(PAGES_PER_STEP)):
        offsets[p] = run
        run = run + jnp.broadcast_to(suffix[:, p * LANES:p * LANES + 1], (heads, LANES))
    carry_ref[...] = run
    between = suffix - log_keep + jnp.concatenate(offsets, axis=1)
    w = jnp.exp(z - sp + between)
    o = acc_ref[...]
    for p in range(PAGES_PER_STEP):
        o = o + jnp.dot(w[:, p * LANES:(p + 1) * LANES].astype(BF16), _page_lanes(v_refs[p], subs),
                        preferred_element_type=F32)
    acc_ref[...] = o

    @pl.when(s == pl.num_programs(1) - 1)
    def _():
        row = lax.broadcasted_iota(jnp.int32, (heads, HEAD_DIM), 0)
        out = jnp.zeros((heads, HEAD_DIM), F32)
        for hb in range(kv_heads):
            out = out + jnp.where(row // GROUP == hb, o[:, hb * HEAD_DIM:(hb + 1) * HEAD_DIM], 0.0)
        o_ref[0] = _rms(out, g_ref[...])


def _sb_decode(q, cache_k, cache_v, page_table, gain, *, kv_heads):
    db, heads, _ = q.shape
    n_pages = page_table.shape[1]
    assert n_pages % PAGES_PER_STEP == 0
    n_steps = n_pages // PAGES_PER_STEP
    width = kv_heads * HEAD_DIM
    kern = functools.partial(_sb_decode_kernel, kv_heads=kv_heads, scale=HEAD_DIM ** -0.5)
    pages = _page_specs(n_pages, PAGES_PER_STEP, width, lambda s: n_steps - 1 - s)
    grid_spec = pltpu.PrefetchScalarGridSpec(
        num_scalar_prefetch=1, grid=(db, n_steps),
        in_specs=[pl.BlockSpec((1, heads, HEAD_DIM), lambda b, s, pt: (b, 0, 0))] + pages + pages
                 + [pl.BlockSpec((heads, HEAD_DIM), lambda b, s, pt: (0, 0))],
        out_specs=pl.BlockSpec((1, heads, HEAD_DIM), lambda b, s, pt: (b, 0, 0)),
        scratch_shapes=[pltpu.VMEM((heads, width), F32), pltpu.VMEM((heads, LANES), F32)])
    return pl.pallas_call(
        kern, out_shape=jax.ShapeDtypeStruct((db, heads, HEAD_DIM), F32), grid_spec=grid_spec,
        compiler_params=_cparams(2), name="sb_decode",
    )(page_table.reshape(-1), q, *([cache_k] * PAGES_PER_STEP), *([cache_v] * PAGES_PER_STEP), gain)


def _diff_decode_kernel(pt_ref, lam_ref, q_ref, kn_ref, vn_ref, rbt_ref, *refs,
                        kv_heads, scale, out_scale, n_steps):
    k_refs = refs[:DIFF_PAGES_PER_STEP]
    v_refs = refs[DIFF_PAGES_PER_STEP:2 * DIFF_PAGES_PER_STEP]
    sub_ref, o_ref, s_ref, m_ref, acc_ref = refs[2 * DIFF_PAGES_PER_STEP:]
    s = pl.program_id(1)
    heads = kv_heads * GROUP
    rows = 2 * heads
    vw = 2 * HEAD_DIM
    lam = lam_ref[0]
    rbt = rbt_ref[...]
    q_bd = _block_diag(q_ref[0], 2 * kv_heads,
                       block_of_row=lambda r: 2 * ((r % heads) // GROUP) + r // heads).astype(BF16)
    k_subs = tuple(range(2 * kv_heads))
    v_subs = tuple(half * kv_heads + hb for hb in range(kv_heads) for half in range(2))

    @pl.when(s == 0)
    def _():
        m_ref[...] = jnp.full_like(m_ref, NEG_BIG)

    @pl.when(s < n_steps)
    def _():
        far = jnp.broadcast_to(rbt[:, N_BUCKETS - 1:N_BUCKETS], (rows, LANES))
        n = LANES - lax.broadcasted_iota(jnp.int32, (rows, LANES), 1)
        near = _bias_from_distance(n, lambda k: jnp.broadcast_to(rbt[:, k:k + 1], (rows, LANES)))
        tiles = []
        for p in range(DIFF_PAGES_PER_STEP):
            bias = jnp.where(s == n_steps - 1, near, far) if p == DIFF_PAGES_PER_STEP - 1 else far
            tiles.append(_nt_dot(q_bd, _page_lanes(k_refs[p], k_subs)) * scale + bias)
        s_ref[s] = jnp.concatenate(tiles, axis=1)
        m_ref[...] = jnp.maximum(m_ref[...], functools.reduce(jnp.maximum, tiles))

    @pl.when(s == n_steps)
    def _():
        k_self = kn_ref[0].astype(BF16).astype(F32)
        s_self = jnp.sum(q_bd.astype(F32) * k_self, axis=-1, keepdims=True) * scale + rbt[:, 0:1]
        m = jnp.maximum(jnp.max(m_ref[...], axis=-1, keepdims=True), s_self)
        e_self = jnp.exp(s_self - m)
        denom = e_self
        for st in range(n_steps):
            e = jnp.exp(s_ref[st] - m)
            s_ref[st] = e
            denom = denom + jnp.sum(e, axis=-1, keepdims=True)
        for st in range(n_steps):
            p = s_ref[st] / denom
            s_ref[st, :heads] = p[:heads] - lam * p[heads:]
        p_self = e_self / denom
        a_self = (p_self[:heads] - lam * p_self[heads:]).astype(BF16).astype(F32)
        acc_ref[...] = a_self * vn_ref[0].astype(BF16).astype(F32)

    @pl.when(s >= n_steps)
    def _():
        attn = s_ref[s - n_steps, :heads].astype(BF16)
        o = acc_ref[...]
        for p in range(DIFF_PAGES_PER_STEP):
            o = o + jnp.dot(attn[:, p * LANES:(p + 1) * LANES], _page_lanes(v_refs[p], v_subs),
                            preferred_element_type=F32)
        acc_ref[...] = o

        @pl.when(s == 2 * n_steps - 1)
        def _():
            row = lax.broadcasted_iota(jnp.int32, (heads, vw), 0)
            out = jnp.zeros((heads, vw), F32)
            for hb in range(kv_heads):
                out = out + jnp.where(row // GROUP == hb, o[:, hb * vw:(hb + 1) * vw], 0.0)
            o_ref[0] = _rms(out, sub_ref[...]) * out_scale


def _diff_decode(q, k_new, v_new, cache_k, cache_v, page_table, rbt, lam, subln, *, kv_heads, out_scale):
    db, rows, _ = q.shape
    heads = rows // 2
    n_pages = page_table.shape[1]
    pps = DIFF_PAGES_PER_STEP
    assert n_pages % pps == 0 and LANES >= MAX_DISTANCE
    n_steps = n_pages // pps
    width = kv_heads * 2 * HEAD_DIM
    vw = 2 * HEAD_DIM
    kern = functools.partial(_diff_decode_kernel, kv_heads=kv_heads, scale=HEAD_DIM ** -0.5, out_scale=out_scale,
                             n_steps=n_steps)
    k_pages = _page_specs(n_pages, pps, width, lambda s: jnp.minimum(s, n_steps - 1))
    v_pages = _page_specs(n_pages, pps, width, lambda s: jnp.maximum(s - n_steps, 0))
    grid_spec = pltpu.PrefetchScalarGridSpec(
        num_scalar_prefetch=1, grid=(db, 2 * n_steps),
        in_specs=[pl.BlockSpec(memory_space=pltpu.SMEM),
                  pl.BlockSpec((1, rows, HEAD_DIM), lambda b, s, pt: (b, 0, 0)),
                  pl.BlockSpec((1, 1, width), lambda b, s, pt: (b, 0, 0)),
                  pl.BlockSpec((1, 1, width), lambda b, s, pt: (b, 0, 0)),
                  pl.BlockSpec((rows, N_BUCKETS), lambda b, s, pt: (0, 0))] + k_pages + v_pages
                 + [pl.BlockSpec((1, vw), lambda b, s, pt: (0, 0))],
        out_specs=pl.BlockSpec((1, heads, vw), lambda b, s, pt: (b, 0, 0)),
        scratch_shapes=[pltpu.VMEM((n_steps, rows, pps * LANES), F32), pltpu.VMEM((rows, LANES), F32),
                        pltpu.VMEM((heads, width), F32)])
    return pl.pallas_call(
        kern, out_shape=jax.ShapeDtypeStruct((db, heads, vw), F32), grid_spec=grid_spec,
        compiler_params=_cparams(2), name="diff_decode",
    )(page_table.reshape(-1), lam.reshape(1), q, k_new, v_new, rbt,
      *([cache_k] * pps), *([cache_v] * pps), subln.reshape(1, vw))


def _router_kernel(x_ref, g_ref, wr_ref, br_ref, *refs, append):
    h_ref, idx_ref, gate_ref = refs[-3:]
    h = _rms(x_ref[...], g_ref[...])
    if append:
        tm = h.shape[0]
        h_ref[:tm] = h
        h_ref[tm:] = jnp.zeros_like(h)
    else:
        h_ref[...] = h
    logits = _nt_dot(wr_ref[...].astype(BF16), h.astype(BF16)) + br_ref[...]
    n_exp = logits.shape[0]
    expert = lax.broadcasted_iota(jnp.int32, logits.shape, 0)
    vals, idxs = [], []
    for _ in range(TOP_K):
        top = jnp.max(logits, axis=0, keepdims=True)
        idx = jnp.min(jnp.where(logits == top, expert, n_exp), axis=0, keepdims=True)
        vals.append(top)
        idxs.append(idx)
        logits = jnp.where(expert == idx, -jnp.inf, logits)
    top_val = jnp.concatenate(vals, axis=0)
    e = jnp.exp(top_val - top_val[0:1])
    gate_ref[...] = e / jnp.sum(e, axis=0, keepdims=True)
    idx_ref[...] = jnp.concatenate(idxs, axis=0)


def _router(x, gain, w_router, b_router, tm, table_rows=None, table=None, row_offset=0):
    t, d = x.shape
    n_exp = w_router.shape[1]
    append = table is not None
    operands = [x, gain.reshape(1, d), w_router.T, b_router.reshape(n_exp, 1)]
    in_specs = [pl.BlockSpec((tm, d), lambda i: (i, 0)), pl.BlockSpec((1, d), lambda i: (0, 0)),
                pl.BlockSpec((n_exp, d), lambda i: (0, 0)), pl.BlockSpec((n_exp, 1), lambda i: (0, 0))]
    if append:
        assert t == tm and row_offset % (2 * t) == 0
        block = row_offset // (2 * t)
        operands.append(table)
        in_specs.append(pl.BlockSpec(memory_space=pl.ANY))
        h_shape, h_spec = table.shape, pl.BlockSpec((2 * t, d), lambda i: (block, 0))
    else:
        h_shape, h_spec = (table_rows or t, d), pl.BlockSpec((tm, d), lambda i: (i, 0))
    return pl.pallas_call(
        functools.partial(_router_kernel, append=append),
        out_shape=(jax.ShapeDtypeStruct(h_shape, F32), jax.ShapeDtypeStruct((TOP_K, t), jnp.int32),
                   jax.ShapeDtypeStruct((TOP_K, t), F32)),
        grid=(t // tm,),
        in_specs=in_specs,
        out_specs=(h_spec, pl.BlockSpec((TOP_K, tm), lambda i: (0, i)),
                   pl.BlockSpec((TOP_K, tm), lambda i: (0, i))),
        input_output_aliases={len(operands) - 1: 0} if append else {},
        compiler_params=_cparams(1), name="router",
    )(*operands)


def _moe_plan(top_idx, n_exp, zero_token):
    n = top_idx.size
    n_chunks = n_exp + n // MOE_CHUNK + 1
    flat_e = top_idx.reshape(-1).astype(jnp.int32)
    counts = jnp.bincount(flat_e, length=n_exp).astype(jnp.int32)
    order = jnp.argsort(flat_e, stable=True).astype(jnp.int32)
    position = jnp.argsort(order).astype(jnp.int32)
    group_start = jnp.cumsum(counts) - counts
    chunks_e = (counts + MOE_CHUNK - 1) // MOE_CHUNK
    chunk_end_e = jnp.cumsum(chunks_e)
    chunk_start_e = chunk_end_e - chunks_e
    dest = chunk_start_e[flat_e] * MOE_CHUNK + position - group_start[flat_e]
    n_used = chunk_end_e[-1]
    cidx = jnp.arange(n_chunks, dtype=jnp.int32)
    c_exp = jnp.clip(jnp.searchsorted(chunk_end_e, cidx, side="right"), 0, n_exp - 1).astype(jnp.int32)
    c_cnt = jnp.clip(counts[c_exp] - (cidx - chunk_start_e[c_exp]) * MOE_CHUNK, 0, MOE_CHUNK)
    used = cidx < n_used
    c_exp = jnp.where(used, c_exp, c_exp[n_used - 1])
    c_cnt = jnp.where(used, c_cnt, 0).astype(jnp.int32)
    c_blk = jnp.where(used, cidx, n_used - 1).astype(jnp.int32)
    in_expert = ((cidx - chunk_start_e[c_exp]) * MOE_CHUNK)[:, None] \
        + jnp.arange(MOE_CHUNK, dtype=jnp.int32)[None, :]
    live = jnp.logical_and(in_expert < counts[c_exp][:, None], used[:, None])
    element = order[jnp.clip(group_start[c_exp][:, None] + in_expert, 0, n - 1)]
    row_token = jnp.where(live, element // TOP_K, zero_token).astype(jnp.int32).reshape(-1)
    return dict(n_chunks=n_chunks, dest=dest.astype(jnp.int32), c_exp=c_exp, c_cnt=c_cnt, c_blk=c_blk,
                n_used=n_used.reshape(1).astype(jnp.int32), row_token=row_token)


def _row_copy(src_hbm, row, dst_vmem, slot, sem):
    return pltpu.make_async_copy(src_hbm.at[pl.ds(row, 1), :], dst_vmem.at[pl.ds(slot, 1), :], sem)


def _moe_matmul_kernel(ce_ref, cnt_ref, blk_ref, nu_ref, *refs, gated, n_chunks, nj):
    c = pl.program_id(0)
    j = pl.program_id(1)
    cnt = cnt_ref[c]
    if gated:
        tok_ref, h_hbm, wg_ref, bg_ref, wu_ref, bu_ref, o_ref, wgb_ref, wub_ref, xbuf, stage, sem = refs
        slot = c % 2
        piece = stage.shape[0]

        def read_rows(rows):
            return xbuf[slot, rows, :]

        def live_rows(chunk):
            return (cnt_ref[chunk] + MOE_TAIL - 1) // MOE_TAIL * MOE_TAIL

        def start_rows(chunk, lo, n):
            base = chunk * MOE_CHUNK + lo

            def issue(i, carry):
                for u in range(BF16_ROWS):
                    r = i * BF16_ROWS + u
                    _row_copy(h_hbm, tok_ref[base + r], stage, r, sem).start()
                return carry

            lax.fori_loop(0, n // BF16_ROWS, issue, 0)

        def finish_rows(dst_slot, lo, n):
            def drain(g, carry):
                for u in range(BF16_ROWS):
                    _row_copy(h_hbm, 0, stage, g * BF16_ROWS + u, sem).wait()
                return carry

            def cast(g, carry):
                src = pl.ds(pl.multiple_of(g * BF16_ROWS, BF16_ROWS), BF16_ROWS)
                dst = pl.ds(pl.multiple_of(lo + g * BF16_ROWS, BF16_ROWS), BF16_ROWS)
                xbuf[dst_slot, dst, :] = stage[src, :].astype(BF16)
                return carry

            lax.fori_loop(0, n // BF16_ROWS, drain, 0)
            lax.fori_loop(0, n // BF16_ROWS, cast, 0)

        @pl.when(jnp.logical_and(c == 0, j == 0))
        def _():
            def first(p, carry):
                n = jnp.clip(live_rows(0) - p * piece, 0, piece)
                start_rows(0, p * piece, n)
                finish_rows(0, p * piece, n)
                return carry

            lax.fori_loop(0, nj, first, 0)

        nxt = jnp.minimum(c + 1, n_chunks - 1)
        ahead = jnp.clip(jnp.where(c + 1 < n_chunks, live_rows(nxt), 0) - j * piece, 0, piece)
        start_rows(nxt, j * piece, ahead)
    else:
        x_ref, wg_ref, bg_ref, o_ref, wgb_ref = refs

        def read_rows(rows):
            return x_ref[rows, :]

    def cast_weights():
        wgb_ref[...] = wg_ref[0].astype(BF16)
        if gated:
            wub_ref[...] = wu_ref[0].astype(BF16)

    def rows_block(start, size):
        rows = pl.ds(pl.multiple_of(start, MOE_TAIL), size)
        xs = read_rows(rows)
        y = jnp.dot(xs, wgb_ref[...], preferred_element_type=F32) + bg_ref[0]
        if gated:
            up = jnp.dot(xs, wub_ref[...], preferred_element_type=F32) + bu_ref[0]
            gate = jnp.minimum(y, SWIGLU_LIMIT)
            up = jnp.clip(up, -SWIGLU_LIMIT, SWIGLU_LIMIT)
            y = (up + 1.0) * (gate * jax.nn.sigmoid(SWIGLU_ALPHA * gate))
        o_ref[rows, :] = y.astype(o_ref.dtype)

    units = (cnt + MOE_TAIL - 1) // MOE_TAIL
    top = MOE_BLOCKS[0]
    assert MOE_CHUNK < 2 * top
    has_top = units >= top // MOE_TAIL

    @pl.when(has_top)
    def _():
        cast_weights()
        rows_block(0, top)

    @pl.when(jnp.logical_and(jnp.logical_not(has_top), cnt > 0))
    def _():
        cast_weights()

    start = jnp.where(has_top, top, 0)
    units = units - start // MOE_TAIL
    for size in MOE_BLOCKS[1:]:
        take = units >= size // MOE_TAIL

        @pl.when(take)
        def _(start=start, size=size):
            rows_block(start, size)

        start = start + jnp.where(take, size, 0)
        units = units - jnp.where(take, size // MOE_TAIL, 0)

    if gated:
        finish_rows(1 - slot, j * piece, ahead)


def _moe_matmul(x, plan, weights, biases, out_dtype):
    gated = len(weights) == 2
    n_exp, k, n = weights[0].shape
    n_chunks = plan["n_chunks"]
    nj = n // MOE_TN
    n_prefetch = 5 if gated else 4

    def col(c, j, nu):
        return jnp.where(c < nu[0], j, nj - 1)

    def w_map(c, j, ce, cnt, blk, nu, *_):
        return (ce[c], 0, col(c, j, nu))

    def o_map(c, j, ce, cnt, blk, nu, *_):
        return (blk[c], col(c, j, nu))

    w_spec = pl.BlockSpec((1, k, MOE_TN), w_map)
    b_spec = pl.BlockSpec((1, 1, MOE_TN), w_map)
    scratch = [pltpu.VMEM((k, MOE_TN), BF16)] * len(weights)
    if gated:
        assert MOE_CHUNK % nj == 0 and (MOE_CHUNK // nj) % BF16_ROWS == 0
        prefetch = [plan["row_token"]]
        operands, in_specs = [x], [pl.BlockSpec(memory_space=pl.ANY)]
        scratch += [pltpu.VMEM((2, MOE_CHUNK, k), BF16), pltpu.VMEM((MOE_CHUNK // nj, k), F32),
                    pltpu.SemaphoreType.DMA]
    else:
        prefetch = []
        operands = [x]
        in_specs = [pl.BlockSpec((MOE_CHUNK, k), lambda c, j, ce, cnt, blk, nu: (blk[c], 0))]
    for w, b in zip(weights, biases):
        operands += [w, b.reshape(n_exp, 1, n)]
        in_specs += [w_spec, b_spec]
    grid_spec = pltpu.PrefetchScalarGridSpec(
        num_scalar_prefetch=n_prefetch, grid=(n_chunks, nj), in_specs=in_specs,
        out_specs=pl.BlockSpec((MOE_CHUNK, MOE_TN), o_map), scratch_shapes=scratch)
    return pl.pallas_call(
        functools.partial(_moe_matmul_kernel, gated=gated, n_chunks=n_chunks, nj=nj),
        out_shape=jax.ShapeDtypeStruct((n_chunks * MOE_CHUNK, n), out_dtype), grid_spec=grid_spec,
        compiler_params=_cparams(2), name="moe_up" if gated else "moe_down",
    )(plan["c_exp"], plan["c_cnt"], plan["c_blk"], plan["n_used"], *prefetch, *operands)


def _moe_combine_kernel(dest_ref, y_hbm, x_ref, g_ref, o_ref, buf, sem, *, tb):
    base = pl.program_id(0) * tb * TOP_K

    def issue(r, carry):
        for k in range(TOP_K):
            _row_copy(y_hbm, dest_ref[base + r * TOP_K + k], buf.at[k], r, sem).start(priority=k % DMA_THREADS)
        return carry

    def drain(r, carry):
        for k in range(TOP_K):
            _row_copy(y_hbm, 0, buf.at[k], r, sem).wait()
        return carry

    lax.fori_loop(0, tb, issue, 0, unroll=4)
    lax.fori_loop(0, tb, drain, 0, unroll=4)
    gates = g_ref[...]
    moe = gates[:, 0:1] * buf[0]
    for k in range(1, TOP_K):
        moe = moe + gates[:, k:k + 1] * buf[k]
    o_ref[...] = x_ref[...] + moe


def _moe_combine(y_rows, dest, x, gates, tb):
    t, d = x.shape
    grid_spec = pltpu.PrefetchScalarGridSpec(
        num_scalar_prefetch=1, grid=(t // tb,),
        in_specs=[pl.BlockSpec(memory_space=pl.ANY),
                  pl.BlockSpec((tb, d), lambda i, dest: (i, 0)),
                  pl.BlockSpec((tb, TOP_K), lambda i, dest: (i, 0))],
        out_specs=pl.BlockSpec((tb, d), lambda i, dest: (i, 0)),
        scratch_shapes=[pltpu.VMEM((TOP_K, tb, d), F32), pltpu.SemaphoreType.DMA])
    return pl.pallas_call(
        functools.partial(_moe_combine_kernel, tb=tb), out_shape=jax.ShapeDtypeStruct((t, d), F32),
        grid_spec=grid_spec, compiler_params=_cparams(1), name="moe_combine",
    )(dest, y_rows, x, gates)


def _pick_tile(n, candidates):
    for c in candidates:
        if n % c == 0:
            return c
    return n


def kernel(x_prompt, x_sample, cache_sb_k, cache_sb_v, cache_diff_k, cache_diff_v, page_table, attn_norm, w_in, diff_q_norm, diff_k_norm, diff_lambda_q1, diff_lambda_k1, diff_lambda_q2, diff_lambda_k2, rel_bias, sb_out_norm, diff_subln, w_out, ffn_norm, w_router, b_router, w_gate, b_gate, w_up, b_up, w_down, b_down):
    depth = attn_norm.shape[0]
    assert depth == 1, "single-layer trunk"
    bsz, seq, d = x_prompt.shape
    db, dseq, _ = x_sample.shape
    assert dseq == 1
    n_exp = w_router.shape[2]
    half = d // 2
    sb_heads = half // HEAD_DIM
    sb_kv = sb_heads // GROUP
    diff_heads = half // (2 * HEAD_DIM)
    diff_kv = diff_heads // GROUP
    sbq, sbk = sb_heads * HEAD_DIM, sb_kv * HEAD_DIM
    dq, dk = diff_heads * 2 * HEAD_DIM, diff_kv * 2 * HEAD_DIM
    off_sbk, off_sbv = sbq, sbq + sbk
    off_dq = sbq + 2 * sbk
    off_dk = off_dq + dq
    off_dv = off_dk + dk
    in_cols = off_dv + dk
    layer = 0
    lambda_init = 0.8 - 0.6 * math.exp(-0.3 * layer)
    lam = (jnp.exp(jnp.sum(diff_lambda_q1[layer] * diff_lambda_k1[layer]))
           - jnp.exp(jnp.sum(diff_lambda_q2[layer] * diff_lambda_k2[layer])) + lambda_init).astype(F32)

    tn_in = 512
    assert off_dq % tn_in == 0 and off_dv % tn_in == 0
    qk_gain = jnp.concatenate([jnp.ones((off_dq,), F32), jnp.tile(diff_q_norm[layer], dq // HEAD_DIM),
                               jnp.tile(diff_k_norm[layer], dk // HEAD_DIM), jnp.ones((dk,), F32)]).reshape(1, in_cols)
    norm_tiles = dict(norm_lo=off_dq // tn_in, norm_hi=off_dv // tn_in)
    n_tok = bsz * seq

    xp = x_prompt.reshape(n_tok, d)
    h_p = _rmsnorm(xp, attn_norm[layer], BF16, _pick_tile(n_tok, (256, 128, 8)))
    tm_p = _pick_tile(n_tok, (1024, 512, 256, 128, 8))
    proj_p = _matmul((h_p,), w_in[layer], qk_gain, mode="qknorm", tm=tm_p, tn=tn_in, **norm_tiles)
    proj3 = proj_p.reshape(bsz, seq, in_cols)
    mix_sb = _sb_prompt(proj3, sb_out_norm[layer].reshape(-1), kv_heads=sb_kv,
                        q_off=0, k_off=off_sbk, v_off=off_sbv)
    mix_d = _diff_prompt(proj3, rel_bias, lam, diff_subln[layer], kv_heads=diff_kv,
                         q_off=off_dq, k_off=off_dk, v_off=off_dv, out_scale=1.0 - lambda_init)
    x2_p = _matmul((mix_sb.reshape(n_tok, sbq), mix_d.reshape(n_tok, dq)), w_out[layer], xp,
                   mode="residual", tm=tm_p, tn=512)

    xs = x_sample.reshape(db, d)
    h_s = _rmsnorm(xs, attn_norm[layer], BF16, db)
    proj_s = _matmul((h_s,), w_in[layer], qk_gain, mode="qknorm", tm=db, tn=tn_in, **norm_tiles)
    pool = cache_sb_k.shape[1]
    page = cache_sb_k.shape[2]
    assert page == LANES
    rows4 = (pool, page * (sbk // HEAD_DIM), HEAD_DIM)
    assert sbk == dk == 4 * HEAD_DIM
    sb_o = _sb_decode(proj_s[:, :sbq].reshape(db, sb_heads, HEAD_DIM),
                      cache_sb_k.reshape(rows4), cache_sb_v.reshape(rows4),
                      page_table, sb_out_norm[layer], kv_heads=sb_kv)
    diff_v_rows = cache_diff_v.reshape(pool, page, diff_kv, 2, HEAD_DIM).transpose(0, 1, 3, 2, 4).reshape(rows4)
    q_d = proj_s[:, off_dq:off_dk].reshape(db, diff_heads, 2, HEAD_DIM).transpose(0, 2, 1, 3)
    rbt = jnp.concatenate([rel_bias.T, rel_bias.T], axis=0).astype(F32)
    d_o = _diff_decode(q_d.reshape(db, 2 * diff_heads, HEAD_DIM),
                       proj_s[:, off_dk:off_dv].reshape(db, 1, dk), proj_s[:, off_dv:].reshape(db, 1, dk),
                       cache_diff_k.reshape(rows4), diff_v_rows,
                       page_table, rbt, lam, diff_subln[layer], kv_heads=diff_kv, out_scale=1.0 - lambda_init)
    x2_s = _matmul((sb_o.reshape(db, sbq).astype(BF16), d_o.reshape(db, dq).astype(BF16)), w_out[layer], xs,
                   mode="residual", tm=db, tn=512)

    h_table, idx_p, gate_p = _router(x2_p, ffn_norm[layer], w_router[layer], b_router[layer],
                                     _pick_tile(n_tok, (256, 128)), table_rows=n_tok + 2 * db)
    h_all, idx_s, gate_s = _router(x2_s, ffn_norm[layer], w_router[layer], b_router[layer], db,
                                   table=h_table, row_offset=n_tok)
    top_idx = jnp.concatenate([idx_p, idx_s], axis=1).T
    plan = _moe_plan(top_idx, n_exp, zero_token=n_tok + db)
    act = _moe_matmul(h_all, plan, (w_gate[layer], w_up[layer]), (b_gate[layer], b_up[layer]), BF16)
    y_rows = _moe_matmul(act, plan, (w_down[layer],), (b_down[layer],), F32)
    dest = plan["dest"]
    y_p = _moe_combine(y_rows, dest[:n_tok * TOP_K], x2_p, gate_p.T, _pick_tile(n_tok, (COMBINE_TOKENS, 8)))
    y_s = _moe_combine(y_rows, dest[n_tok * TOP_K:], x2_s, gate_s.T, db)

    def rows(p, lead, lo, hi, shape):
        return p[:, lo:hi].reshape((depth,) + lead + shape)

    lead_p, lead_s = (bsz, seq), (db, dseq)
    return (y_p.reshape(bsz, seq, d), y_s.reshape(db, dseq, d),
            rows(proj_p, lead_p, off_sbk, off_sbv, (sb_kv, HEAD_DIM)),
            rows(proj_p, lead_p, off_sbv, off_dq, (sb_kv, HEAD_DIM)),
            rows(proj_p, lead_p, off_dk, off_dv, (diff_kv, 2, HEAD_DIM)),
            rows(proj_p, lead_p, off_dv, in_cols, (diff_kv, 2 * HEAD_DIM)),
            rows(proj_s, lead_s, off_sbk, off_sbv, (sb_kv, HEAD_DIM)),
            rows(proj_s, lead_s, off_sbv, off_dq, (sb_kv, HEAD_DIM)),
            rows(proj_s, lead_s, off_dk, off_dv, (diff_kv, 2, HEAD_DIM)),
            rows(proj_s, lead_s, off_dv, in_cols, (diff_kv, 2 * HEAD_DIM)))
```

```python
import functools
import math

import numpy as np
import jax
import jax.numpy as jnp
from jax import lax
from jax.experimental import pallas as pl
from jax.experimental.pallas import tpu as pltpu

F32 = jnp.float32
BF16 = jnp.bfloat16

HEAD_DIM = 128
GROUP = 4
N_BUCKETS = 32
MAX_EXACT = N_BUCKETS // 2
MAX_DISTANCE = 128
TOP_K = 4
SWIGLU_LIMIT = 7.0
SWIGLU_ALPHA = 1.702
EPS = 1e-5
NEG_BIG = -1e30

V7X_VMEM_LIMIT_BYTES = 56 * 1024 * 1024
LANES = 128
ATTN_BLOCK = 128
PAGES_PER_STEP = 8
DIFF_PAGES_PER_STEP = 16
MOE_TAIL = 128
MOE_BLOCKS = (1024, 512, 256, 128)
MOE_CHUNK = 1280
BF16_ROWS = 16
MOE_TN = 256
COMBINE_TOKENS = 64
DMA_THREADS = 2


def _cparams(n_axes):
    return pltpu.CompilerParams(dimension_semantics=("arbitrary",) * n_axes,
                                vmem_limit_bytes=V7X_VMEM_LIMIT_BYTES)


def _nt_dot(a, b, precision=None):
    return lax.dot_general(a, b, (((1,), (1,)), ((), ())), precision=precision,
                           preferred_element_type=F32)


def _rms(x, gain):
    ms = jnp.mean(x * x, axis=-1, keepdims=True)
    return x * lax.rsqrt(ms + EPS) * gain


def _bucket_thresholds():
    n = np.arange(MAX_DISTANCE + 1)
    nf = np.maximum(n, MAX_EXACT).astype(np.float32)
    large = MAX_EXACT + (np.log(nf / np.float32(MAX_EXACT)) / np.float32(math.log(MAX_DISTANCE / MAX_EXACT))
                         * np.float32(N_BUCKETS - MAX_EXACT)).astype(np.int32)
    large = np.minimum(large, N_BUCKETS - 1)
    bucket = np.where(n < MAX_EXACT, n, large)
    return [int(np.argmax(bucket >= k)) for k in range(1, N_BUCKETS)]


_BUCKET_THR = _bucket_thresholds()


def _rmsnorm_kernel(x_ref, g_ref, o_ref):
    o_ref[...] = _rms(x_ref[...], g_ref[...]).astype(o_ref.dtype)


def _rmsnorm(x, gain, out_dtype, tm):
    t, d = x.shape
    return pl.pallas_call(
        _rmsnorm_kernel,
        out_shape=jax.ShapeDtypeStruct((t, d), out_dtype),
        grid=(t // tm,),
        in_specs=[pl.BlockSpec((tm, d), lambda i: (i, 0)), pl.BlockSpec((1, d), lambda i: (0, 0))],
        out_specs=pl.BlockSpec((tm, d), lambda i: (i, 0)),
        compiler_params=_cparams(1), name="rmsnorm",
    )(x, gain.reshape(1, d))


def _matmul_kernel(*refs, n_parts, mode, norm_lo, norm_hi, tn):
    a_refs = refs[:n_parts]
    w_ref, e_ref, o_ref, wb_ref = refs[n_parts:]
    j = pl.program_id(0)
    i = pl.program_id(1)

    @pl.when(i == 0)
    def _():
        wb_ref[...] = w_ref[...].astype(BF16)

    acc, k_lo = None, 0
    for a_ref in a_refs:
        k_hi = k_lo + a_ref.shape[1]
        part = jnp.dot(a_ref[...], wb_ref[k_lo:k_hi, :], preferred_element_type=F32)
        acc = part if acc is None else acc + part
        k_lo = k_hi
    if mode == "residual":
        o_ref[...] = e_ref[...] + acc
    else:
        in_range = jnp.logical_and(j >= norm_lo, j < norm_hi)

        @pl.when(in_range)
        def _():
            for c in range(tn // LANES):
                sl = slice(c * LANES, (c + 1) * LANES)
                o_ref[:, sl] = _rms(acc[:, sl], e_ref[:, sl])

        @pl.when(jnp.logical_not(in_range))
        def _():
            o_ref[...] = acc


def _matmul(a_parts, w, extra, *, mode, tm, tn, norm_lo=0, norm_hi=0):
    m = a_parts[0].shape[0]
    k, n = w.shape
    assert sum(a.shape[1] for a in a_parts) == k
    if mode == "residual":
        e_spec = pl.BlockSpec((tm, tn), lambda j, i: (i, j))
    else:
        e_spec = pl.BlockSpec((1, tn), lambda j, i: (0, j))
    kern = functools.partial(_matmul_kernel, n_parts=len(a_parts), mode=mode, norm_lo=norm_lo, norm_hi=norm_hi,
                             tn=tn)
    return pl.pallas_call(
        kern,
        out_shape=jax.ShapeDtypeStruct((m, n), F32),
        grid=(n // tn, m // tm),
        in_specs=[pl.BlockSpec((tm, a.shape[1]), lambda j, i: (i, 0)) for a in a_parts]
                 + [pl.BlockSpec((k, tn), lambda j, i: (0, j)), e_spec],
        out_specs=pl.BlockSpec((tm, tn), lambda j, i: (i, j)),
        scratch_shapes=[pltpu.VMEM((k, tn), BF16)],
        compiler_params=_cparams(2), name="matmul_" + mode,
    )(*a_parts, w, extra)


def _softplus(z):
    return jnp.maximum(z, 0.0) + jnp.log(1.0 + jnp.exp(-jnp.abs(z)))


def _sb_prompt_kernel(q_ref, k_ref, v_ref, g_ref, o_ref, acc_ref, *, bq, scale):
    qi = pl.program_id(2)
    rows = GROUP * bq
    q = jnp.concatenate([q_ref[0, :, g * HEAD_DIM:(g + 1) * HEAD_DIM] for g in range(GROUP)], axis=0)
    qs = (q * scale).astype(BF16)
    row_i = lax.broadcasted_iota(jnp.int32, (bq, bq), 0)
    col_i = lax.broadcasted_iota(jnp.int32, (bq, bq), 1)
    later = jnp.where(row_i > col_i, 1.0, 0.0).astype(BF16)
    visible = jnp.concatenate([col_i < row_i] * GROUP, axis=0)

    later2 = jnp.concatenate([later, later], axis=0)

    def weights(z, carry, masked):
        sp = _softplus(z)
        log_keep = jnp.where(visible, -sp, 0.0) if masked else -sp
        hi, lo = _split_bf16(log_keep)
        between = jnp.dot(jnp.concatenate([hi, lo], axis=1), later2, preferred_element_type=F32) + carry
        w = jnp.exp(z - sp + between)
        if masked:
            w = jnp.where(visible, w, 0.0)
        return w.astype(BF16), carry + jnp.sum(log_keep, axis=-1, keepdims=True)

    def single(kb, carry, masked):
        start = pl.multiple_of(kb * bq, bq)
        k = k_ref[0, pl.ds(start, bq), :].astype(BF16)
        v = v_ref[0, pl.ds(start, bq), :].astype(BF16)
        w, carry = weights(_nt_dot(qs, k), carry, masked)
        acc_ref[...] += jnp.dot(w, v, preferred_element_type=F32)
        return carry

    n_pairs = qi // 2

    def pair(t, carry):
        start = pl.multiple_of((2 * (n_pairs - 1 - t)) * bq, bq)
        k = k_ref[0, pl.ds(start, 2 * bq), :].astype(BF16)
        v = v_ref[0, pl.ds(start, 2 * bq), :].astype(BF16)
        z = _nt_dot(qs, k)
        w_late, carry = weights(z[:, bq:], carry, False)
        w_early, carry = weights(z[:, :bq], carry, False)
        acc_ref[...] += jnp.dot(jnp.concatenate([w_early, w_late], axis=1), v, preferred_element_type=F32)
        return carry

    acc_ref[...] = jnp.zeros_like(acc_ref)
    carry = single(qi, jnp.zeros((rows, 1), F32), True)
    carry = lax.cond(qi % 2 == 1, lambda c: single(qi - 1, c, False), lambda c: c, carry)
    lax.fori_loop(0, n_pairs, pair, carry)

    acc = acc_ref[...]
    for g in range(GROUP):
        sl = slice(g * HEAD_DIM, (g + 1) * HEAD_DIM)
        o_ref[0, :, sl] = _rms(acc[g * bq:(g + 1) * bq, :], g_ref[:, sl]).astype(o_ref.dtype)


def _sb_prompt(proj3, gain, *, kv_heads, q_off, k_off, v_off):
    b, t, _ = proj3.shape
    bq = ATTN_BLOCK
    qw = GROUP * HEAD_DIM
    width = kv_heads * qw
    assert q_off % qw == 0 and k_off % HEAD_DIM == 0 and v_off % HEAD_DIM == 0 and t % bq == 0
    kern = functools.partial(_sb_prompt_kernel, bq=bq, scale=HEAD_DIM ** -0.5)
    return pl.pallas_call(
        kern,
        out_shape=jax.ShapeDtypeStruct((b, t, width), BF16),
        grid=(b, kv_heads, t // bq),
        in_specs=[pl.BlockSpec((1, bq, qw), lambda bi, h, qi: (bi, qi, q_off // qw + h)),
                  pl.BlockSpec((1, t, HEAD_DIM), lambda bi, h, qi: (bi, 0, k_off // HEAD_DIM + h)),
                  pl.BlockSpec((1, t, HEAD_DIM), lambda bi, h, qi: (bi, 0, v_off // HEAD_DIM + h)),
                  pl.BlockSpec((1, qw), lambda bi, h, qi: (0, h))],
        out_specs=pl.BlockSpec((1, bq, qw), lambda bi, h, qi: (bi, qi, h)),
        scratch_shapes=[pltpu.VMEM((GROUP * bq, HEAD_DIM), F32)],
        compiler_params=_cparams(3), name="sb_prompt",
    )(proj3, proj3, proj3, gain.reshape(1, width))


def _bias_from_distance(n, table):
    bias = table(0)
    for k in range(1, N_BUCKETS):
        bias = jnp.where(n >= _BUCKET_THR[k - 1], table(k), bias)
    return bias


def _diff_prompt_kernel(rb_ref, lam_ref, q_ref, k_ref, v_ref, sub_ref, o_ref,
                        tiles_ref, s_ref, m_ref, l_ref, acc_ref, *, bq, scale, kv_heads, out_scale):
    bi = pl.program_id(0)
    h = pl.program_id(1)
    qi = pl.program_id(2)
    rows = GROUP * bq
    vw = 2 * HEAD_DIM
    row_i = lax.broadcasted_iota(jnp.int32, (bq, bq), 0)
    col_i = lax.broadcasted_iota(jnp.int32, (bq, bq), 1)

    @pl.when(jnp.logical_and(jnp.logical_and(bi == 0, h == 0), qi == 0))
    def _():
        for kind in range(2):
            n = jnp.maximum(row_i - col_i + kind * bq, 0)
            for hh in range(kv_heads):
                for g in range(GROUP):
                    head = hh * GROUP + g
                    tiles_ref[kind, hh, g * bq:(g + 1) * bq, :] = _bias_from_distance(
                        n, lambda k, head=head: rb_ref[k, head])

    far_bias = jnp.concatenate(
        [jnp.full((bq, 1), rb_ref[N_BUCKETS - 1, h * GROUP + g], F32) for g in range(GROUP)], axis=0)
    causal = jnp.concatenate([col_i <= row_i] * GROUP, axis=0)
    qs = []
    for c in range(2):
        qc = jnp.concatenate([q_ref[0, :, (2 * g + c) * HEAD_DIM:(2 * g + c + 1) * HEAD_DIM]
                              for g in range(GROUP)], axis=0)
        qs.append((qc * scale).astype(BF16))

    def logits(c, start, width):
        k = k_ref[0, pl.ds(pl.multiple_of(start, bq), width), c * HEAD_DIM:(c + 1) * HEAD_DIM].astype(BF16)
        return _nt_dot(qs[c], k)

    n_far = jnp.maximum(qi - 1, 0)
    n_far_pairs = n_far // 2
    m_ref[...] = jnp.full_like(m_ref, NEG_BIG)

    def far_pair(j, carry):
        for c in range(2):
            s = logits(c, j * 2 * bq, 2 * bq)
            m_ref[c] = jnp.maximum(m_ref[c], jnp.maximum(s[:, :bq], s[:, bq:]))
            s_ref[c, 2 * j] = s[:, :bq] + far_bias
            s_ref[c, 2 * j + 1] = s[:, bq:] + far_bias
        return carry

    lax.fori_loop(0, n_far_pairs, far_pair, 0)

    @pl.when(n_far % 2 == 1)
    def _():
        for c in range(2):
            s = logits(c, (n_far - 1) * bq, bq)
            m_ref[c] = jnp.maximum(m_ref[c], s)
            s_ref[c, n_far - 1] = s + far_bias

    for c in range(2):
        m_ref[c] = m_ref[c] + far_bias

    @pl.when(qi >= 1)
    def _():
        for c in range(2):
            s = logits(c, (qi - 1) * bq, bq) + tiles_ref[1, h]
            m_ref[c] = jnp.maximum(m_ref[c], s)
            s_ref[c, qi - 1] = s

    for c in range(2):
        s = jnp.where(causal, logits(c, qi * bq, bq) + tiles_ref[0, h], NEG_BIG)
        s_ref[c, qi] = s
        m_ref[c] = jnp.broadcast_to(jnp.max(jnp.maximum(m_ref[c], s), axis=-1, keepdims=True), (rows, bq))

    @pl.when(qi % 2 == 0)
    def _():
        for c in range(2):
            s_ref[c, qi + 1] = jnp.full((rows, bq), NEG_BIG, F32)

    l_ref[...] = jnp.zeros_like(l_ref)
    acc_ref[...] = jnp.zeros_like(acc_ref)

    def accumulate(j, carry):
        v = v_ref[0, pl.ds(pl.multiple_of(j * 2 * bq, 2 * bq), 2 * bq), :].astype(BF16)
        for c in range(2):
            p0 = jnp.exp(s_ref[c, 2 * j] - m_ref[c])
            p1 = jnp.exp(s_ref[c, 2 * j + 1] - m_ref[c])
            l_ref[c] += p0 + p1
            acc_ref[c] += jnp.dot(jnp.concatenate([p0, p1], axis=1).astype(BF16), v, preferred_element_type=F32)
        return carry

    lax.fori_loop(0, qi // 2 + 1, accumulate, 0)

    lam = lam_ref[0]
    l0 = jnp.sum(l_ref[0], axis=-1, keepdims=True)
    l1 = jnp.sum(l_ref[1], axis=-1, keepdims=True)
    o = acc_ref[0] / l0 - lam * (acc_ref[1] / l1)
    for g in range(GROUP):
        o_ref[0, :, g * vw:(g + 1) * vw] = (_rms(o[g * bq:(g + 1) * bq, :], sub_ref[...]) * out_scale
                                            ).astype(o_ref.dtype)


def _diff_prompt(proj3, rel_bias, lam, subln, *, kv_heads, q_off, k_off, v_off, out_scale):
    b, t, _ = proj3.shape
    bq = ATTN_BLOCK
    qw = GROUP * 2 * HEAD_DIM
    vw = 2 * HEAD_DIM
    width = kv_heads * GROUP * vw
    assert q_off % qw == 0 and k_off % vw == 0 and v_off % vw == 0 and t % (2 * bq) == 0
    assert bq + 1 >= _BUCKET_THR[-1]
    kern = functools.partial(_diff_prompt_kernel, bq=bq, scale=HEAD_DIM ** -0.5, kv_heads=kv_heads,
                             out_scale=out_scale)
    smem = pl.BlockSpec(memory_space=pltpu.SMEM)
    return pl.pallas_call(
        kern,
        out_shape=jax.ShapeDtypeStruct((b, t, width), BF16),
        grid=(b, kv_heads, t // bq),
        in_specs=[smem, smem,
                  pl.BlockSpec((1, bq, qw), lambda bi, h, qi: (bi, qi, q_off // qw + h)),
                  pl.BlockSpec((1, t, vw), lambda bi, h, qi: (bi, 0, k_off // vw + h)),
                  pl.BlockSpec((1, t, vw), lambda bi, h, qi: (bi, 0, v_off // vw + h)),
                  pl.BlockSpec((1, vw), lambda bi, h, qi: (0, 0))],
        out_specs=pl.BlockSpec((1, bq, GROUP * vw), lambda bi, h, qi: (bi, qi, h)),
        scratch_shapes=[pltpu.VMEM((2, kv_heads, GROUP * bq, bq), F32),
                        pltpu.VMEM((2, t // bq, GROUP * bq, bq), F32),
                        pltpu.VMEM((2, GROUP * bq, bq), F32),
                        pltpu.VMEM((2, GROUP * bq, bq), F32),
                        pltpu.VMEM((2, GROUP * bq, vw), F32)],
        compiler_params=_cparams(3), name="diff_prompt",
    )(rel_bias, lam.reshape(1), proj3, proj3, proj3, subln.reshape(1, vw))


def _block_diag(q, n_blocks, block_of_row):
    r = q.shape[0]
    row = lax.broadcasted_iota(jnp.int32, (r, HEAD_DIM), 0)
    blk = block_of_row(row)
    return jnp.concatenate([jnp.where(blk == hb, q, 0.0) for hb in range(n_blocks)], axis=1)


def _split_bf16(x):
    hi = x.astype(BF16)
    lo = (x - hi.astype(F32)).astype(BF16)
    return hi, lo


def _page_lanes(ref, subs):
    n = ref.shape[1] // LANES
    return jnp.concatenate([ref[0, pl.ds(sub, LANES, stride=n), :] for sub in subs], axis=1).astype(BF16)


def _page_specs(n_pages, pages_per_step, width, step_of):
    specs = []
    for n in range(pages_per_step):
        def imap(b, s, pt, n=n):
            return (pt[b * n_pages + step_of(s) * pages_per_step + n], 0, 0)
        specs.append(pl.BlockSpec((1, width, LANES), imap))
    return specs


def _sb_decode_kernel(pt_ref, q_ref, *refs, kv_heads, scale):
    k_refs = refs[:PAGES_PER_STEP]
    v_refs = refs[PAGES_PER_STEP:2 * PAGES_PER_STEP]
    g_ref, o_ref, acc_ref, carry_ref = refs[2 * PAGES_PER_STEP:]
    s = pl.program_id(1)
    heads = kv_heads * GROUP
    width = PAGES_PER_STEP * LANES

    @pl.when(s == 0)
    def _():
        acc_ref[...] = jnp.zeros_like(acc_ref)
        carry_ref[...] = jnp.zeros_like(carry_ref)

    q_bd = _block_diag(q_ref[0], kv_heads, block_of_row=lambda r: r // GROUP).astype(BF16)
    subs = tuple(range(kv_heads))
    z = jnp.concatenate([_nt_dot(q_bd, _page_lanes(k_refs[p], subs)) for p in range(PAGES_PER_STEP)],
                        axis=1) * scale
    sp = _softplus(z)
    log_keep = -sp
    lane = lax.broadcasted_iota(jnp.int32, (heads, width), 1) % LANES
    suffix = log_keep
    sh = 1
    while sh < LANES:
        suffix = suffix + jnp.where(lane + sh < LANES, pltpu.roll(suffix, width - sh, 1), 0.0)
        sh *= 2
    run = carry_ref[...]
    offsets = [None] * PAGES_PER_STEP
    for p in reversed(range(PAGES_PER_STEP)):
        offsets[p] = run
        run = run + jnp.broadcast_to(suffix[:, p * LANES:p * LANES + 1], (heads, LANES))
    carry_ref[...] = run
    between = suffix - log_keep + jnp.concatenate(offsets, axis=1)
    w = jnp.exp(z - sp + between)
    o = acc_ref[...]
    for p in range(PAGES_PER_STEP):
        o = o + jnp.dot(w[:, p * LANES:(p + 1) * LANES].astype(BF16), _page_lanes(v_refs[p], subs),
                        preferred_element_type=F32)
    acc_ref[...] = o

    @pl.when(s == pl.num_programs(1) - 1)
    def _():
        row = lax.broadcasted_iota(jnp.int32, (heads, HEAD_DIM), 0)
        out = jnp.zeros((heads, HEAD_DIM), F32)
        for hb in range(kv_heads):
            out = out + jnp.where(row // GROUP == hb, o[:, hb * HEAD_DIM:(hb + 1) * HEAD_DIM], 0.0)
        o_ref[0] = _rms(out, g_ref[...])


def _sb_decode(q, cache_k, cache_v, page_table, gain, *, kv_heads):
    db, heads, _ = q.shape
    n_pages = page_table.shape[1]
    assert n_pages % PAGES_PER_STEP == 0
    n_steps = n_pages // PAGES_PER_STEP
    width = kv_heads * HEAD_DIM
    kern = functools.partial(_sb_decode_kernel, kv_heads=kv_heads, scale=HEAD_DIM ** -0.5)
    pages = _page_specs(n_pages, PAGES_PER_STEP, width, lambda s: n_steps - 1 - s)
    grid_spec = pltpu.PrefetchScalarGridSpec(
        num_scalar_prefetch=1, grid=(db, n_steps),
        in_specs=[pl.BlockSpec((1, heads, HEAD_DIM), lambda b, s, pt: (b, 0, 0))] + pages + pages
                 + [pl.BlockSpec((heads, HEAD_DIM), lambda b, s, pt: (0, 0))],
        out_specs=pl.BlockSpec((1, heads, HEAD_DIM), lambda b, s, pt: (b, 0, 0)),
        scratch_shapes=[pltpu.VMEM((heads, width), F32), pltpu.VMEM((heads, LANES), F32)])
    return pl.pallas_call(
        kern, out_shape=jax.ShapeDtypeStruct((db, heads, HEAD_DIM), F32), grid_spec=grid_spec,
        compiler_params=_cparams(2), name="sb_decode",
    )(page_table.reshape(-1), q, *([cache_k] * PAGES_PER_STEP), *([cache_v] * PAGES_PER_STEP), gain)


def _diff_decode_kernel(pt_ref, lam_ref, q_ref, kn_ref, vn_ref, rbt_ref, *refs,
                        kv_heads, scale, out_scale, n_steps):
    k_refs = refs[:DIFF_PAGES_PER_STEP]
    v_refs = refs[DIFF_PAGES_PER_STEP:2 * DIFF_PAGES_PER_STEP]
    sub_ref, o_ref, s_ref, m_ref, acc_ref = refs[2 * DIFF_PAGES_PER_STEP:]
    s = pl.program_id(1)
    heads = kv_heads * GROUP
    rows = 2 * heads
    vw = 2 * HEAD_DIM
    lam = lam_ref[0]
    rbt = rbt_ref[...]
    q_bd = _block_diag(q_ref[0], 2 * kv_heads,
                       block_of_row=lambda r: 2 * ((r % heads) // GROUP) + r // heads).astype(BF16)
    k_subs = tuple(range(2 * kv_heads))
    v_subs = tuple(half * kv_heads + hb for hb in range(kv_heads) for half in range(2))

    @pl.when(s == 0)
    def _():
        m_ref[...] = jnp.full_like(m_ref, NEG_BIG)

    @pl.when(s < n_steps)
    def _():
        far = jnp.broadcast_to(rbt[:, N_BUCKETS - 1:N_BUCKETS], (rows, LANES))
        n = LANES - lax.broadcasted_iota(jnp.int32, (rows, LANES), 1)
        near = _bias_from_distance(n, lambda k: jnp.broadcast_to(rbt[:, k:k + 1], (rows, LANES)))
        tiles = []
        for p in range(DIFF_PAGES_PER_STEP):
            bias = jnp.where(s == n_steps - 1, near, far) if p == DIFF_PAGES_PER_STEP - 1 else far
            tiles.append(_nt_dot(q_bd, _page_lanes(k_refs[p], k_subs)) * scale + bias)
        s_ref[s] = jnp.concatenate(tiles, axis=1)
        m_ref[...] = jnp.maximum(m_ref[...], functools.reduce(jnp.maximum, tiles))

    @pl.when(s == n_steps)
    def _():
        k_self = kn_ref[0].astype(BF16).astype(F32)
        s_self = jnp.sum(q_bd.astype(F32) * k_self, axis=-1, keepdims=True) * scale + rbt[:, 0:1]
        m = jnp.maximum(jnp.max(m_ref[...], axis=-1, keepdims=True), s_self)
        e_self = jnp.exp(s_self - m)
        denom = e_self
        for st in range(n_steps):
            e = jnp.exp(s_ref[st] - m)
            s_ref[st] = e
            denom = denom + jnp.sum(e, axis=-1, keepdims=True)
        for st in range(n_steps):
            p = s_ref[st] / denom
            s_ref[st, :heads] = p[:heads] - lam * p[heads:]
        p_self = e_self / denom
        a_self = (p_self[:heads] - lam * p_self[heads:]).astype(BF16).astype(F32)
        acc_ref[...] = a_self * vn_ref[0].astype(BF16).astype(F32)

    @pl.when(s >= n_steps)
    def _():
        attn = s_ref[s - n_steps, :heads].astype(BF16)
        o = acc_ref[...]
        for p in range(DIFF_PAGES_PER_STEP):
            o = o + jnp.dot(attn[:, p * LANES:(p + 1) * LANES], _page_lanes(v_refs[p], v_subs),
                            preferred_element_type=F32)
        acc_ref[...] = o

        @pl.when(s == 2 * n_steps - 1)
        def _():
            row = lax.broadcasted_iota(jnp.int32, (heads, vw), 0)
            out = jnp.zeros((heads, vw), F32)
            for hb in range(kv_heads):
                out = out + jnp.where(row // GROUP == hb, o[:, hb * vw:(hb + 1) * vw], 0.0)
            o_ref[0] = _rms(out, sub_ref[...]) * out_scale


def _diff_decode(q, k_new, v_new, cache_k, cache_v, page_table, rbt, lam, subln, *, kv_heads, out_scale):
    db, rows, _ = q.shape
    heads = rows // 2
    n_pages = page_table.shape[1]
    pps = DIFF_PAGES_PER_STEP
    assert n_pages % pps == 0 and LANES >= MAX_DISTANCE
    n_steps = n_pages // pps
    width = kv_heads * 2 * HEAD_DIM
    vw = 2 * HEAD_DIM
    kern = functools.partial(_diff_decode_kernel, kv_heads=kv_heads, scale=HEAD_DIM ** -0.5, out_scale=out_scale,
                             n_steps=n_steps)
    k_pages = _page_specs(n_pages, pps, width, lambda s: jnp.minimum(s, n_steps - 1))
    v_pages = _page_specs(n_pages, pps, width, lambda s: jnp.maximum(s - n_steps, 0))
    grid_spec = pltpu.PrefetchScalarGridSpec(
        num_scalar_prefetch=1, grid=(db, 2 * n_steps),
        in_specs=[pl.BlockSpec(memory_space=pltpu.SMEM),
                  pl.BlockSpec((1, rows, HEAD_DIM), lambda b, s, pt: (b, 0, 0)),
                  pl.BlockSpec((1, 1, width), lambda b, s, pt: (b, 0, 0)),
                  pl.BlockSpec((1, 1, width), lambda b, s, pt: (b, 0, 0)),
                  pl.BlockSpec((rows, N_BUCKETS), lambda b, s, pt: (0, 0))] + k_pages + v_pages
                 + [pl.BlockSpec((1, vw), lambda b, s, pt: (0, 0))],
        out_specs=pl.BlockSpec((1, heads, vw), lambda b, s, pt: (b, 0, 0)),
        scratch_shapes=[pltpu.VMEM((n_steps, rows, pps * LANES), F32), pltpu.VMEM((rows, LANES), F32),
                        pltpu.VMEM((heads, width), F32)])
    return pl.pallas_call(
        kern, out_shape=jax.ShapeDtypeStruct((db, heads, vw), F32), grid_spec=grid_spec,
        compiler_params=_cparams(2), name="diff_decode",
    )(page_table.reshape(-1), lam.reshape(1), q, k_new, v_new, rbt,
      *([cache_k] * pps), *([cache_v] * pps), subln.reshape(1, vw))


def _router_kernel(x_ref, g_ref, wr_ref, br_ref, *refs, append):
    h_ref, idx_ref, gate_ref = refs[-3:]
    h = _rms(x_ref[...], g_ref[...])
    if append:
        tm = h.shape[0]
        h_ref[:tm] = h
        h_ref[tm:] = jnp.zeros_like(h)
    else:
        h_ref[...] = h
    logits = _nt_dot(wr_ref[...].astype(BF16), h.astype(BF16)) + br_ref[...]
    n_exp = logits.shape[0]
    expert = lax.broadcasted_iota(jnp.int32, logits.shape, 0)
    vals, idxs = [], []
    for _ in range(TOP_K):
        top = jnp.max(logits, axis=0, keepdims=True)
        idx = jnp.min(jnp.where(logits == top, expert, n_exp), axis=0, keepdims=True)
        vals.append(top)
        idxs.append(idx)
        logits = jnp.where(expert == idx, -jnp.inf, logits)
    top_val = jnp.concatenate(vals, axis=0)
    e = jnp.exp(top_val - top_val[0:1])
    gate_ref[...] = e / jnp.sum(e, axis=0, keepdims=True)
    idx_ref[...] = jnp.concatenate(idxs, axis=0)


def _router(x, gain, w_router, b_router, tm, table_rows=None, table=None, row_offset=0):
    t, d = x.shape
    n_exp = w_router.shape[1]
    append = table is not None
    operands = [x, gain.reshape(1, d), w_router.T, b_router.reshape(n_exp, 1)]
    in_specs = [pl.BlockSpec((tm, d), lambda i: (i, 0)), pl.BlockSpec((1, d), lambda i: (0, 0)),
                pl.BlockSpec((n_exp, d), lambda i: (0, 0)), pl.BlockSpec((n_exp, 1), lambda i: (0, 0))]
    if append:
        assert t == tm and row_offset % (2 * t) == 0
        block = row_offset // (2 * t)
        operands.append(table)
        in_specs.append(pl.BlockSpec(memory_space=pl.ANY))
        h_shape, h_spec = table.shape, pl.BlockSpec((2 * t, d), lambda i: (block, 0))
    else:
        h_shape, h_spec = (table_rows or t, d), pl.BlockSpec((tm, d), lambda i: (i, 0))
    return pl.pallas_call(
        functools.partial(_router_kernel, append=append),
        out_shape=(jax.ShapeDtypeStruct(h_shape, F32), jax.ShapeDtypeStruct((TOP_K, t), jnp.int32),
                   jax.ShapeDtypeStruct((TOP_K, t), F32)),
        grid=(t // tm,),
        in_specs=in_specs,
        out_specs=(h_spec, pl.BlockSpec((TOP_K, tm), lambda i: (0, i)),
                   pl.BlockSpec((TOP_K, tm), lambda i: (0, i))),
        input_output_aliases={len(operands) - 1: 0} if append else {},
        compiler_params=_cparams(1), name="router",
    )(*operands)


def _moe_plan(top_idx, n_exp, zero_token):
    n = top_idx.size
    n_chunks = n_exp + n // MOE_CHUNK + 1
    flat_e = top_idx.reshape(-1).astype(jnp.int32)
    counts = jnp.bincount(flat_e, length=n_exp).astype(jnp.int32)
    order = jnp.argsort(flat_e, stable=True).astype(jnp.int32)
    position = jnp.argsort(order).astype(jnp.int32)
    group_start = jnp.cumsum(counts) - counts
    chunks_e = (counts + MOE_CHUNK - 1) // MOE_CHUNK
    chunk_end_e = jnp.cumsum(chunks_e)
    chunk_start_e = chunk_end_e - chunks_e
    dest = chunk_start_e[flat_e] * MOE_CHUNK + position - group_start[flat_e]
    n_used = chunk_end_e[-1]
    cidx = jnp.arange(n_chunks, dtype=jnp.int32)
    c_exp = jnp.clip(jnp.searchsorted(chunk_end_e, cidx, side="right"), 0, n_exp - 1).astype(jnp.int32)
    c_cnt = jnp.clip(counts[c_exp] - (cidx - chunk_start_e[c_exp]) * MOE_CHUNK, 0, MOE_CHUNK)
    used = cidx < n_used
    c_exp = jnp.where(used, c_exp, c_exp[n_used - 1])
    c_cnt = jnp.where(used, c_cnt, 0).astype(jnp.int32)
    c_blk = jnp.where(used, cidx, n_used - 1).astype(jnp.int32)
    c_first = (group_start[c_exp] + (cidx - chunk_start_e[c_exp]) * MOE_CHUNK).astype(jnp.int32)
    return dict(n_chunks=n_chunks, dest=dest.astype(jnp.int32), c_exp=c_exp, c_cnt=c_cnt, c_blk=c_blk,
                n_used=n_used.reshape(1).astype(jnp.int32), c_first=c_first, order=order,
                zero_token=zero_token)


def _row_copy(src_hbm, row, dst_vmem, slot, sem):
    return pltpu.make_async_copy(src_hbm.at[pl.ds(row, 1), :], dst_vmem.at[pl.ds(slot, 1), :], sem)


def _moe_matmul_kernel(ce_ref, cnt_ref, blk_ref, nu_ref, *refs, gated, n_chunks, nj, zero_token):
    c = pl.program_id(0)
    j = pl.program_id(1)
    cnt = cnt_ref[c]
    if gated:
        (first_ref, order_ref, h_hbm, wg_ref, bg_ref, wu_ref, bu_ref, o_ref,
         wgb_ref, wub_ref, xbuf, stage, sem) = refs
        slot = c % 2
        piece = stage.shape[0]
        n_elem = order_ref.shape[0]

        def read_rows(rows):
            return xbuf[slot, rows, :]

        def start_rows(chunk, chunk_cnt, lo):
            first = first_ref[chunk]
            for r in range(piece):
                row = lo + r
                elem = order_ref[jnp.minimum(first + row, n_elem - 1)]
                token = jnp.where(row < chunk_cnt, elem // TOP_K, zero_token)
                _row_copy(h_hbm, token, stage, r, sem).start()

        def finish_rows(dst_slot, lo):
            for r in range(piece):
                _row_copy(h_hbm, 0, stage, r, sem).wait()
            xbuf[dst_slot, pl.ds(pl.multiple_of(lo, BF16_ROWS), piece), :] = stage[...].astype(BF16)

        @pl.when(jnp.logical_and(c == 0, j == 0))
        def _():
            def first_chunk(p, carry):
                start_rows(0, cnt_ref[0], p * piece)
                finish_rows(0, p * piece)
                return carry

            lax.fori_loop(0, nj, first_chunk, 0)

        nxt = jnp.minimum(c + 1, n_chunks - 1)
        nxt_cnt = jnp.where(c + 1 < n_chunks, cnt_ref[nxt], 0)

        def start_next():
            start_rows(nxt, nxt_cnt, j * piece)
    else:
        x_ref, wg_ref, bg_ref, o_ref, wgb_ref = refs

        def read_rows(rows):
            return x_ref[rows, :]

    def cast_weights():
        wgb_ref[...] = wg_ref[0].astype(BF16)
        if gated:
            wub_ref[...] = wu_ref[0].astype(BF16)

    def rows_block(start, size):
        rows = pl.ds(pl.multiple_of(start, MOE_TAIL), size)
        xs = read_rows(rows)
        y = jnp.dot(xs, wgb_ref[...], preferred_element_type=F32) + bg_ref[0]
        if gated:
            up = jnp.dot(xs, wub_ref[...], preferred_element_type=F32) + bu_ref[0]
            gate = jnp.minimum(y, SWIGLU_LIMIT)
            up = jnp.clip(up, -SWIGLU_LIMIT, SWIGLU_LIMIT)
            y = (up + 1.0) * (gate * jax.nn.sigmoid(SWIGLU_ALPHA * gate))
        o_ref[rows, :] = y.astype(o_ref.dtype)

    units = (cnt + MOE_TAIL - 1) // MOE_TAIL
    top = MOE_BLOCKS[0]
    assert MOE_CHUNK < 2 * top
    has_top = units >= top // MOE_TAIL

    @pl.when(has_top)
    def _():
        if gated:
            start_next()
        cast_weights()
        rows_block(0, top)

    @pl.when(jnp.logical_and(jnp.logical_not(has_top), cnt > 0))
    def _():
        if gated:
            start_next()
        cast_weights()

    start = jnp.where(has_top, top, 0)
    units = units - start // MOE_TAIL
    for size in MOE_BLOCKS[1:]:
        take = units >= size // MOE_TAIL

        @pl.when(take)
        def _(start=start, size=size):
            rows_block(start, size)

        start = start + jnp.where(take, size, 0)
        units = units - jnp.where(take, size // MOE_TAIL, 0)

    if gated:
        @pl.when(cnt > 0)
        def _():
            finish_rows(1 - slot, j * piece)


def _moe_matmul(x, plan, weights, biases, out_dtype):
    gated = len(weights) == 2
    n_exp, k, n = weights[0].shape
    n_chunks = plan["n_chunks"]
    nj = n // MOE_TN
    n_prefetch = 6 if gated else 4

    def col(c, j, nu):
        return jnp.where(c < nu[0], j, nj - 1)

    def w_map(c, j, ce, cnt, blk, nu, *_):
        return (ce[c], 0, col(c, j, nu))

    def o_map(c, j, ce, cnt, blk, nu, *_):
        return (blk[c], col(c, j, nu))

    w_spec = pl.BlockSpec((1, k, MOE_TN), w_map)
    b_spec = pl.BlockSpec((1, 1, MOE_TN), w_map)
    scratch = [pltpu.VMEM((k, MOE_TN), BF16)] * len(weights)
    if gated:
        assert MOE_CHUNK % nj == 0 and (MOE_CHUNK // nj) % BF16_ROWS == 0
        prefetch = [plan["c_first"], plan["order"]]
        operands, in_specs = [x], [pl.BlockSpec(memory_space=pl.ANY)]
        scratch += [pltpu.VMEM((2, MOE_CHUNK, k), BF16), pltpu.VMEM((MOE_CHUNK // nj, k), F32),
                    pltpu.SemaphoreType.DMA]
    else:
        prefetch = []
        operands = [x]
        in_specs = [pl.BlockSpec((MOE_CHUNK, k), lambda c, j, ce, cnt, blk, nu: (blk[c], 0))]
    for w, b in zip(weights, biases):
        operands += [w, b.reshape(n_exp, 1, n)]
        in_specs += [w_spec, b_spec]
    grid_spec = pltpu.PrefetchScalarGridSpec(
        num_scalar_prefetch=n_prefetch, grid=(n_chunks, nj), in_specs=in_specs,
        out_specs=pl.BlockSpec((MOE_CHUNK, MOE_TN), o_map), scratch_shapes=scratch)
    return pl.pallas_call(
        functools.partial(_moe_matmul_kernel, gated=gated, n_chunks=n_chunks, nj=nj,
                          zero_token=plan["zero_token"]),
        out_shape=jax.ShapeDtypeStruct((n_chunks * MOE_CHUNK, n), out_dtype), grid_spec=grid_spec,
        compiler_params=_cparams(2), name="moe_up" if gated else "moe_down",
    )(plan["c_exp"], plan["c_cnt"], plan["c_blk"], plan["n_used"], *prefetch, *operands)


def _moe_combine_kernel(dest_ref, y_hbm, x_ref, g_ref, o_ref, buf, sem, *, tb):
    base = pl.program_id(0) * tb * TOP_K

    def issue(r, carry):
        for k in range(TOP_K):
            _row_copy(y_hbm, dest_ref[base + r * TOP_K + k], buf.at[k], r, sem).start(priority=k % DMA_THREADS)
        return carry

    def drain(r, carry):
        for k in range(TOP_K):
            _row_copy(y_hbm, 0, buf.at[k], r, sem).wait()
        return carry

    lax.fori_loop(0, tb, issue, 0, unroll=4)
    lax.fori_loop(0, tb, drain, 0, unroll=4)
    gates = g_ref[...]
    moe = gates[:, 0:1] * buf[0]
    for k in range(1, TOP_K):
        moe = moe + gates[:, k:k + 1] * buf[k]
    o_ref[...] = x_ref[...] + moe


def _moe_combine(y_rows, dest, x, gates, tb):
    t, d = x.shape
    grid_spec = pltpu.PrefetchScalarGridSpec(
        num_scalar_prefetch=1, grid=(t // tb,),
        in_specs=[pl.BlockSpec(memory_space=pl.ANY),
                  pl.BlockSpec((tb, d), lambda i, dest: (i, 0)),
                  pl.BlockSpec((tb, TOP_K), lambda i, dest: (i, 0))],
        out_specs=pl.BlockSpec((tb, d), lambda i, dest: (i, 0)),
        scratch_shapes=[pltpu.VMEM((TOP_K, tb, d), F32), pltpu.SemaphoreType.DMA])
    return pl.pallas_call(
        functools.partial(_moe_combine_kernel, tb=tb), out_shape=jax.ShapeDtypeStruct((t, d), F32),
        grid_spec=grid_spec, compiler_params=_cparams(1), name="moe_combine",
    )(dest, y_rows, x, gates)


def _pick_tile(n, candidates):
    for c in candidates:
        if n % c == 0:
            return c
    return n


def kernel(x_prompt, x_sample, cache_sb_k, cache_sb_v, cache_diff_k, cache_diff_v, page_table, attn_norm, w_in, diff_q_norm, diff_k_norm, diff_lambda_q1, diff_lambda_k1, diff_lambda_q2, diff_lambda_k2, rel_bias, sb_out_norm, diff_subln, w_out, ffn_norm, w_router, b_router, w_gate, b_gate, w_up, b_up, w_down, b_down):
    depth = attn_norm.shape[0]
    assert depth == 1, "single-layer trunk"
    bsz, seq, d = x_prompt.shape
    db, dseq, _ = x_sample.shape
    assert dseq == 1
    n_exp = w_router.shape[2]
    half = d // 2
    sb_heads = half // HEAD_DIM
    sb_kv = sb_heads // GROUP
    diff_heads = half // (2 * HEAD_DIM)
    diff_kv = diff_heads // GROUP
    sbq, sbk = sb_heads * HEAD_DIM, sb_kv * HEAD_DIM
    dq, dk = diff_heads * 2 * HEAD_DIM, diff_kv * 2 * HEAD_DIM
    off_sbk, off_sbv = sbq, sbq + sbk
    off_dq = sbq + 2 * sbk
    off_dk = off_dq + dq
    off_dv = off_dk + dk
    in_cols = off_dv + dk
    layer = 0
    lambda_init = 0.8 - 0.6 * math.exp(-0.3 * layer)
    lam = (jnp.exp(jnp.sum(diff_lambda_q1[layer] * diff_lambda_k1[layer]))
           - jnp.exp(jnp.sum(diff_lambda_q2[layer] * diff_lambda_k2[layer])) + lambda_init).astype(F32)

    tn_in = 512
    assert off_dq % tn_in == 0 and off_dv % tn_in == 0
    qk_gain = jnp.concatenate([jnp.ones((off_dq,), F32), jnp.tile(diff_q_norm[layer], dq // HEAD_DIM),
                               jnp.tile(diff_k_norm[layer], dk // HEAD_DIM), jnp.ones((dk,), F32)]).reshape(1, in_cols)
    norm_tiles = dict(norm_lo=off_dq // tn_in, norm_hi=off_dv // tn_in)
    n_tok = bsz * seq

    xp = x_prompt.reshape(n_tok, d)
    h_p = _rmsnorm(xp, attn_norm[layer], BF16, _pick_tile(n_tok, (256, 128, 8)))
    tm_p = _pick_tile(n_tok, (1024, 512, 256, 128, 8))
    proj_p = _matmul((h_p,), w_in[layer], qk_gain, mode="qknorm", tm=tm_p, tn=tn_in, **norm_tiles)
    proj3 = proj_p.reshape(bsz, seq, in_cols)
    mix_sb = _sb_prompt(proj3, sb_out_norm[layer].reshape(-1), kv_heads=sb_kv,
                        q_off=0, k_off=off_sbk, v_off=off_sbv)
    mix_d = _diff_prompt(proj3, rel_bias, lam, diff_subln[layer], kv_heads=diff_kv,
                         q_off=off_dq, k_off=off_dk, v_off=off_dv, out_scale=1.0 - lambda_init)
    x2_p = _matmul((mix_sb.reshape(n_tok, sbq), mix_d.reshape(n_tok, dq)), w_out[layer], xp,
                   mode="residual", tm=tm_p, tn=512)

    xs = x_sample.reshape(db, d)
    h_s = _rmsnorm(xs, attn_norm[layer], BF16, db)
    proj_s = _matmul((h_s,), w_in[layer], qk_gain, mode="qknorm", tm=db, tn=tn_in, **norm_tiles)
    pool = cache_sb_k.shape[1]
    page = cache_sb_k.shape[2]
    assert page == LANES
    rows4 = (pool, page * (sbk // HEAD_DIM), HEAD_DIM)
    assert sbk == dk == 4 * HEAD_DIM
    sb_o = _sb_decode(proj_s[:, :sbq].reshape(db, sb_heads, HEAD_DIM),
                      cache_sb_k.reshape(rows4), cache_sb_v.reshape(rows4),
                      page_table, sb_out_norm[layer], kv_heads=sb_kv)
    diff_v_rows = cache_diff_v.reshape(pool, page, diff_kv, 2, HEAD_DIM).transpose(0, 1, 3, 2, 4).reshape(rows4)
    q_d = proj_s[:, off_dq:off_dk].reshape(db, diff_heads, 2, HEAD_DIM).transpose(0, 2, 1, 3)
    rbt = jnp.concatenate([rel_bias.T, rel_bias.T], axis=0).astype(F32)
    d_o = _diff_decode(q_d.reshape(db, 2 * diff_heads, HEAD_DIM),
                       proj_s[:, off_dk:off_dv].reshape(db, 1, dk), proj_s[:, off_dv:].reshape(db, 1, dk),
                       cache_diff_k.reshape(rows4), diff_v_rows,
                       page_table, rbt, lam, diff_subln[layer], kv_heads=diff_kv, out_scale=1.0 - lambda_init)
    x2_s = _matmul((sb_o.reshape(db, sbq).astype(BF16), d_o.reshape(db, dq).astype(BF16)), w_out[layer], xs,
                   mode="residual", tm=db, tn=512)

    h_table, idx_p, gate_p = _router(x2_p, ffn_norm[layer], w_router[layer], b_router[layer],
                                     _pick_tile(n_tok, (256, 128)), table_rows=n_tok + 2 * db)
    h_all, idx_s, gate_s = _router(x2_s, ffn_norm[layer], w_router[layer], b_router[layer], db,
                                   table=h_table, row_offset=n_tok)
    top_idx = jnp.concatenate([idx_p, idx_s], axis=1).T
    plan = _moe_plan(top_idx, n_exp, zero_token=n_tok + db)
    act = _moe_matmul(h_all, plan, (w_gate[layer], w_up[layer]), (b_gate[layer], b_up[layer]), BF16)
    y_rows = _moe_matmul(act, plan, (w_down[layer],), (b_down[layer],), F32)
    dest = plan["dest"]
    y_p = _moe_combine(y_rows, dest[:n_tok * TOP_K], x2_p, gate_p.T, _pick_tile(n_tok, (COMBINE_TOKENS, 8)))
    y_s = _moe_combine(y_rows, dest[n_tok * TOP_K:], x2_s, gate_s.T, db)

    def rows(p, lead, lo, hi, shape):
        return p[:, lo:hi].reshape((depth,) + lead + shape)

    lead_p, lead_s = (bsz, seq), (db, dseq)
    return (y_p.reshape(bsz, seq, d), y_s.reshape(db, dseq, d),
            rows(proj_p, lead_p, off_sbk, off_sbv, (sb_kv, HEAD_DIM)),
            rows(proj_p, lead_p, off_sbv, off_dq, (sb_kv, HEAD_DIM)),
            rows(proj_p, lead_p, off_dk, off_dv, (diff_kv, 2, HEAD_DIM)),
            rows(proj_p, lead_p, off_dv, in_cols, (diff_kv, 2 * HEAD_DIM)),
            rows(proj_s, lead_s, off_sbk, off_sbv, (sb_kv, HEAD_DIM)),
            rows(proj_s, lead_s, off_sbv, off_dq, (sb_kv, HEAD_DIM)),
            rows(proj_s, lead_s, off_dk, off_dv, (diff_kv, 2, HEAD_DIM)),
            rows(proj_s, lead_s, off_dv, in_cols, (diff_kv, 2 * HEAD_DIM)))
```

```python
import functools
import math

import numpy as np
import jax
import jax.numpy as jnp
from jax import lax
from jax.experimental import pallas as pl
from jax.experimental.pallas import tpu as pltpu

F32 = jnp.float32
BF16 = jnp.bfloat16

HEAD_DIM = 128
GROUP = 4
N_BUCKETS = 32
MAX_EXACT = N_BUCKETS // 2
MAX_DISTANCE = 128
TOP_K = 4
SWIGLU_LIMIT = 7.0
SWIGLU_ALPHA = 1.702
EPS = 1e-5
NEG_BIG = -1e30

V7X_VMEM_LIMIT_BYTES = 56 * 1024 * 1024
LANES = 128
ATTN_BLOCK = 128
PAGES_PER_STEP = 8
DIFF_PAGES_PER_STEP = 16
MOE_TAIL = 128
MOE_BLOCKS = (1024, 512, 256, 128)
MOE_CHUNK = 1280
BF16_ROWS = 16
MOE_TN = 256
COMBINE_TOKENS = 64
DMA_THREADS = 2


def _cparams(n_axes):
    return pltpu.CompilerParams(dimension_semantics=("arbitrary",) * n_axes,
                                vmem_limit_bytes=V7X_VMEM_LIMIT_BYTES)


def _nt_dot(a, b, precision=None):
    return lax.dot_general(a, b, (((1,), (1,)), ((), ())), precision=precision,
                           preferred_element_type=F32)


def _rms(x, gain):
    ms = jnp.mean(x * x, axis=-1, keepdims=True)
    return x * lax.rsqrt(ms + EPS) * gain


def _bucket_thresholds():
    n = np.arange(MAX_DISTANCE + 1)
    nf = np.maximum(n, MAX_EXACT).astype(np.float32)
    large = MAX_EXACT + (np.log(nf / np.float32(MAX_EXACT)) / np.float32(math.log(MAX_DISTANCE / MAX_EXACT))
                         * np.float32(N_BUCKETS - MAX_EXACT)).astype(np.int32)
    large = np.minimum(large, N_BUCKETS - 1)
    bucket = np.where(n < MAX_EXACT, n, large)
    return [int(np.argmax(bucket >= k)) for k in range(1, N_BUCKETS)]


_BUCKET_THR = _bucket_thresholds()


def _rmsnorm_kernel(x_ref, g_ref, o_ref):
    o_ref[...] = _rms(x_ref[...], g_ref[...]).astype(o_ref.dtype)


def _rmsnorm(x, gain, out_dtype, tm):
    t, d = x.shape
    return pl.pallas_call(
        _rmsnorm_kernel,
        out_shape=jax.ShapeDtypeStruct((t, d), out_dtype),
        grid=(t // tm,),
        in_specs=[pl.BlockSpec((tm, d), lambda i: (i, 0)), pl.BlockSpec((1, d), lambda i: (0, 0))],
        out_specs=pl.BlockSpec((tm, d), lambda i: (i, 0)),
        compiler_params=_cparams(1), name="rmsnorm",
    )(x, gain.reshape(1, d))


def _matmul_kernel(*refs, n_parts, mode, norm_lo, norm_hi, tn):
    a_refs = refs[:n_parts]
    w_ref, e_ref, o_ref, wb_ref = refs[n_parts:]
    j = pl.program_id(0)
    i = pl.program_id(1)

    @pl.when(i == 0)
    def _():
        wb_ref[...] = w_ref[...].astype(BF16)

    acc, k_lo = None, 0
    for a_ref in a_refs:
        k_hi = k_lo + a_ref.shape[1]
        part = jnp.dot(a_ref[...], wb_ref[k_lo:k_hi, :], preferred_element_type=F32)
        acc = part if acc is None else acc + part
        k_lo = k_hi
    if mode == "residual":
        o_ref[...] = e_ref[...] + acc
    else:
        in_range = jnp.logical_and(j >= norm_lo, j < norm_hi)

        @pl.when(in_range)
        def _():
            for c in range(tn // LANES):
                sl = slice(c * LANES, (c + 1) * LANES)
                o_ref[:, sl] = _rms(acc[:, sl], e_ref[:, sl])

        @pl.when(jnp.logical_not(in_range))
        def _():
            o_ref[...] = acc


def _matmul(a_parts, w, extra, *, mode, tm, tn, norm_lo=0, norm_hi=0):
    m = a_parts[0].shape[0]
    k, n = w.shape
    assert sum(a.shape[1] for a in a_parts) == k
    if mode == "residual":
        e_spec = pl.BlockSpec((tm, tn), lambda j, i: (i, j))
    else:
        e_spec = pl.BlockSpec((1, tn), lambda j, i: (0, j))
    kern = functools.partial(_matmul_kernel, n_parts=len(a_parts), mode=mode, norm_lo=norm_lo, norm_hi=norm_hi,
                             tn=tn)
    return pl.pallas_call(
        kern,
        out_shape=jax.ShapeDtypeStruct((m, n), F32),
        grid=(n // tn, m // tm),
        in_specs=[pl.BlockSpec((tm, a.shape[1]), lambda j, i: (i, 0)) for a in a_parts]
                 + [pl.BlockSpec((k, tn), lambda j, i: (0, j)), e_spec],
        out_specs=pl.BlockSpec((tm, tn), lambda j, i: (i, j)),
        scratch_shapes=[pltpu.VMEM((k, tn), BF16)],
        compiler_params=_cparams(2), name="matmul_" + mode,
    )(*a_parts, w, extra)


def _softplus(z):
    return jnp.maximum(z, 0.0) + jnp.log(1.0 + jnp.exp(-jnp.abs(z)))


def _sb_prompt_kernel(q_ref, k_ref, v_ref, g_ref, o_ref, acc_ref, *, bq, scale):
    qi = pl.program_id(2)
    rows = GROUP * bq
    q = jnp.concatenate([q_ref[0, :, g * HEAD_DIM:(g + 1) * HEAD_DIM] for g in range(GROUP)], axis=0)
    qs = (q * scale).astype(BF16)
    row_i = lax.broadcasted_iota(jnp.int32, (bq, bq), 0)
    col_i = lax.broadcasted_iota(jnp.int32, (bq, bq), 1)
    later = jnp.where(row_i > col_i, 1.0, 0.0).astype(BF16)
    visible = jnp.concatenate([col_i < row_i] * GROUP, axis=0)

    later2 = jnp.concatenate([later, later], axis=0)

    def weights(z, carry, masked):
        sp = _softplus(z)
        log_keep = jnp.where(visible, -sp, 0.0) if masked else -sp
        hi, lo = _split_bf16(log_keep)
        between = jnp.dot(jnp.concatenate([hi, lo], axis=1), later2, preferred_element_type=F32) + carry
        w = jnp.exp(z - sp + between)
        if masked:
            w = jnp.where(visible, w, 0.0)
        return w.astype(BF16), carry + jnp.sum(log_keep, axis=-1, keepdims=True)

    def single(kb, carry, masked):
        start = pl.multiple_of(kb * bq, bq)
        k = k_ref[0, pl.ds(start, bq), :].astype(BF16)
        v = v_ref[0, pl.ds(start, bq), :].astype(BF16)
        w, carry = weights(_nt_dot(qs, k), carry, masked)
        acc_ref[...] += jnp.dot(w, v, preferred_element_type=F32)
        return carry

    n_pairs = qi // 2

    def pair(t, carry):
        start = pl.multiple_of((2 * (n_pairs - 1 - t)) * bq, bq)
        k = k_ref[0, pl.ds(start, 2 * bq), :].astype(BF16)
        v = v_ref[0, pl.ds(start, 2 * bq), :].astype(BF16)
        z = _nt_dot(qs, k)
        w_late, carry = weights(z[:, bq:], carry, False)
        w_early, carry = weights(z[:, :bq], carry, False)
        acc_ref[...] += jnp.dot(jnp.concatenate([w_early, w_late], axis=1), v, preferred_element_type=F32)
        return carry

    acc_ref[...] = jnp.zeros_like(acc_ref)
    carry = single(qi, jnp.zeros((rows, 1), F32), True)
    carry = lax.cond(qi % 2 == 1, lambda c: single(qi - 1, c, False), lambda c: c, carry)
    lax.fori_loop(0, n_pairs, pair, carry)

    acc = acc_ref[...]
    for g in range(GROUP):
        sl = slice(g * HEAD_DIM, (g + 1) * HEAD_DIM)
        o_ref[0, :, sl] = _rms(acc[g * bq:(g + 1) * bq, :], g_ref[:, sl]).astype(o_ref.dtype)


def _sb_prompt(proj3, gain, *, kv_heads, q_off, k_off, v_off):
    b, t, _ = proj3.shape
    bq = ATTN_BLOCK
    qw = GROUP * HEAD_DIM
    width = kv_heads * qw
    assert q_off % qw == 0 and k_off % HEAD_DIM == 0 and v_off % HEAD_DIM == 0 and t % bq == 0
    kern = functools.partial(_sb_prompt_kernel, bq=bq, scale=HEAD_DIM ** -0.5)
    return pl.pallas_call(
        kern,
        out_shape=jax.ShapeDtypeStruct((b, t, width), BF16),
        grid=(b, kv_heads, t // bq),
        in_specs=[pl.BlockSpec((1, bq, qw), lambda bi, h, qi: (bi, qi, q_off // qw + h)),
                  pl.BlockSpec((1, t, HEAD_DIM), lambda bi, h, qi: (bi, 0, k_off // HEAD_DIM + h)),
                  pl.BlockSpec((1, t, HEAD_DIM), lambda bi, h, qi: (bi, 0, v_off // HEAD_DIM + h)),
                  pl.BlockSpec((1, qw), lambda bi, h, qi: (0, h))],
        out_specs=pl.BlockSpec((1, bq, qw), lambda bi, h, qi: (bi, qi, h)),
        scratch_shapes=[pltpu.VMEM((GROUP * bq, HEAD_DIM), F32)],
        compiler_params=_cparams(3), name="sb_prompt",
    )(proj3, proj3, proj3, gain.reshape(1, width))


def _bias_from_distance(n, table):
    bias = table(0)
    for k in range(1, N_BUCKETS):
        bias = jnp.where(n >= _BUCKET_THR[k - 1], table(k), bias)
    return bias


def _diff_prompt_kernel(rb_ref, lam_ref, q_ref, k_ref, v_ref, sub_ref, o_ref,
                        tiles_ref, s_ref, m_ref, l_ref, acc_ref, *, bq, scale, kv_heads, out_scale):
    bi = pl.program_id(0)
    h = pl.program_id(1)
    qi = pl.program_id(2)
    rows = GROUP * bq
    vw = 2 * HEAD_DIM
    row_i = lax.broadcasted_iota(jnp.int32, (bq, bq), 0)
    col_i = lax.broadcasted_iota(jnp.int32, (bq, bq), 1)

    @pl.when(jnp.logical_and(jnp.logical_and(bi == 0, h == 0), qi == 0))
    def _():
        for kind in range(2):
            n = jnp.maximum(row_i - col_i + kind * bq, 0)
            for hh in range(kv_heads):
                for g in range(GROUP):
                    head = hh * GROUP + g
                    tiles_ref[kind, hh, g * bq:(g + 1) * bq, :] = _bias_from_distance(
                        n, lambda k, head=head: rb_ref[k, head])

    far_bias = jnp.concatenate(
        [jnp.full((bq, 1), rb_ref[N_BUCKETS - 1, h * GROUP + g], F32) for g in range(GROUP)], axis=0)
    causal = jnp.concatenate([col_i <= row_i] * GROUP, axis=0)
    qs = []
    for c in range(2):
        qc = jnp.concatenate([q_ref[0, :, (2 * g + c) * HEAD_DIM:(2 * g + c + 1) * HEAD_DIM]
                              for g in range(GROUP)], axis=0)
        qs.append((qc * scale).astype(BF16))

    def logits(c, start, width):
        k = k_ref[0, pl.ds(pl.multiple_of(start, bq), width), c * HEAD_DIM:(c + 1) * HEAD_DIM].astype(BF16)
        return _nt_dot(qs[c], k)

    n_far = jnp.maximum(qi - 1, 0)
    n_far_pairs = n_far // 2
    m_ref[...] = jnp.full_like(m_ref, NEG_BIG)

    def far_pair(j, carry):
        for c in range(2):
            s = logits(c, j * 2 * bq, 2 * bq)
            m_ref[c] = jnp.maximum(m_ref[c], jnp.maximum(s[:, :bq], s[:, bq:]))
            s_ref[c, 2 * j] = s[:, :bq] + far_bias
            s_ref[c, 2 * j + 1] = s[:, bq:] + far_bias
        return carry

    lax.fori_loop(0, n_far_pairs, far_pair, 0)

    @pl.when(n_far % 2 == 1)
    def _():
        for c in range(2):
            s = logits(c, (n_far - 1) * bq, bq)
            m_ref[c] = jnp.maximum(m_ref[c], s)
            s_ref[c, n_far - 1] = s + far_bias

    for c in range(2):
        m_ref[c] = m_ref[c] + far_bias

    @pl.when(qi >= 1)
    def _():
        for c in range(2):
            s = logits(c, (qi - 1) * bq, bq) + tiles_ref[1, h]
            m_ref[c] = jnp.maximum(m_ref[c], s)
            s_ref[c, qi - 1] = s

    for c in range(2):
        s = jnp.where(causal, logits(c, qi * bq, bq) + tiles_ref[0, h], NEG_BIG)
        s_ref[c, qi] = s
        m_ref[c] = jnp.broadcast_to(jnp.max(jnp.maximum(m_ref[c], s), axis=-1, keepdims=True), (rows, bq))

    @pl.when(qi % 2 == 0)
    def _():
        for c in range(2):
            s_ref[c, qi + 1] = jnp.full((rows, bq), NEG_BIG, F32)

    l_ref[...] = jnp.zeros_like(l_ref)
    acc_ref[...] = jnp.zeros_like(acc_ref)

    def accumulate(j, carry):
        v = v_ref[0, pl.ds(pl.multiple_of(j * 2 * bq, 2 * bq), 2 * bq), :].astype(BF16)
        for c in range(2):
            p0 = jnp.exp(s_ref[c, 2 * j] - m_ref[c])
            p1 = jnp.exp(s_ref[c, 2 * j + 1] - m_ref[c])
            l_ref[c] += p0 + p1
            acc_ref[c] += jnp.dot(jnp.concatenate([p0, p1], axis=1).astype(BF16), v, preferred_element_type=F32)
        return carry

    lax.fori_loop(0, qi // 2 + 1, accumulate, 0)

    lam = lam_ref[0]
    l0 = jnp.sum(l_ref[0], axis=-1, keepdims=True)
    l1 = jnp.sum(l_ref[1], axis=-1, keepdims=True)
    o = acc_ref[0] / l0 - lam * (acc_ref[1] / l1)
    for g in range(GROUP):
        o_ref[0, :, g * vw:(g + 1) * vw] = (_rms(o[g * bq:(g + 1) * bq, :], sub_ref[...]) * out_scale
                                            ).astype(o_ref.dtype)


def _diff_prompt(proj3, rel_bias, lam, subln, *, kv_heads, q_off, k_off, v_off, out_scale):
    b, t, _ = proj3.shape
    bq = ATTN_BLOCK
    qw = GROUP * 2 * HEAD_DIM
    vw = 2 * HEAD_DIM
    width = kv_heads * GROUP * vw
    assert q_off % qw == 0 and k_off % vw == 0 and v_off % vw == 0 and t % (2 * bq) == 0
    assert bq + 1 >= _BUCKET_THR[-1]
    kern = functools.partial(_diff_prompt_kernel, bq=bq, scale=HEAD_DIM ** -0.5, kv_heads=kv_heads,
                             out_scale=out_scale)
    smem = pl.BlockSpec(memory_space=pltpu.SMEM)
    return pl.pallas_call(
        kern,
        out_shape=jax.ShapeDtypeStruct((b, t, width), BF16),
        grid=(b, kv_heads, t // bq),
        in_specs=[smem, smem,
                  pl.BlockSpec((1, bq, qw), lambda bi, h, qi: (bi, qi, q_off // qw + h)),
                  pl.BlockSpec((1, t, vw), lambda bi, h, qi: (bi, 0, k_off // vw + h)),
                  pl.BlockSpec((1, t, vw), lambda bi, h, qi: (bi, 0, v_off // vw + h)),
                  pl.BlockSpec((1, vw), lambda bi, h, qi: (0, 0))],
        out_specs=pl.BlockSpec((1, bq, GROUP * vw), lambda bi, h, qi: (bi, qi, h)),
        scratch_shapes=[pltpu.VMEM((2, kv_heads, GROUP * bq, bq), F32),
                        pltpu.VMEM((2, t // bq, GROUP * bq, bq), F32),
                        pltpu.VMEM((2, GROUP * bq, bq), F32),
                        pltpu.VMEM((2, GROUP * bq, bq), F32),
                        pltpu.VMEM((2, GROUP * bq, vw), F32)],
        compiler_params=_cparams(3), name="diff_prompt",
    )(rel_bias, lam.reshape(1), proj3, proj3, proj3, subln.reshape(1, vw))


def _block_diag(q, n_blocks, block_of_row):
    r = q.shape[0]
    row = lax.broadcasted_iota(jnp.int32, (r, HEAD_DIM), 0)
    blk = block_of_row(row)
    return jnp.concatenate([jnp.where(blk == hb, q, 0.0) for hb in range(n_blocks)], axis=1)


def _split_bf16(x):
    hi = x.astype(BF16)
    lo = (x - hi.astype(F32)).astype(BF16)
    return hi, lo


def _page_lanes(ref, subs):
    n = ref.shape[1] // LANES
    return jnp.concatenate([ref[0, pl.ds(sub, LANES, stride=n), :] for sub in subs], axis=1).astype(BF16)


def _page_specs(n_pages, pages_per_step, width, step_of):
    specs = []
    for n in range(pages_per_step):
        def imap(b, s, pt, n=n):
            return (pt[b * n_pages + step_of(s) * pages_per_step + n], 0, 0)
        specs.append(pl.BlockSpec((1, width, LANES), imap))
    return specs


def _sb_decode_kernel(pt_ref, q_ref, *refs, kv_heads, scale):
    k_refs = refs[:PAGES_PER_STEP]
    v_refs = refs[PAGES_PER_STEP:2 * PAGES_PER_STEP]
    g_ref, o_ref, acc_ref, carry_ref = refs[2 * PAGES_PER_STEP:]
    s = pl.program_id(1)
    heads = kv_heads * GROUP
    width = PAGES_PER_STEP * LANES

    @pl.when(s == 0)
    def _():
        acc_ref[...] = jnp.zeros_like(acc_ref)
        carry_ref[...] = jnp.zeros_like(carry_ref)

    q_bd = _block_diag(q_ref[0], kv_heads, block_of_row=lambda r: r // GROUP).astype(BF16)
    subs = tuple(range(kv_heads))
    z = jnp.concatenate([_nt_dot(q_bd, _page_lanes(k_refs[p], subs)) for p in range(PAGES_PER_STEP)],
                        axis=1) * scale
    sp = _softplus(z)
    log_keep = -sp
    lane = lax.broadcasted_iota(jnp.int32, (heads, width), 1) % LANES
    suffix = log_keep
    sh = 1
    while sh < LANES:
        suffix = suffix + jnp.where(lane + sh < LANES, pltpu.roll(suffix, width - sh, 1), 0.0)
        sh *= 2
    run = carry_ref[...]
    offsets = [None] * PAGES_PER_STEP
    for p in reversed(range(PAGES_PER_STEP)):
        offsets[p] = run
        run = run + jnp.broadcast_to(suffix[:, p * LANES:p * LANES + 1], (heads, LANES))
    carry_ref[...] = run
    between = suffix - log_keep + jnp.concatenate(offsets, axis=1)
    w = jnp.exp(z - sp + between)
    o = acc_ref[...]
    for p in range(PAGES_PER_STEP):
        o = o + jnp.dot(w[:, p * LANES:(p + 1) * LANES].astype(BF16), _page_lanes(v_refs[p], subs),
                        preferred_element_type=F32)
    acc_ref[...] = o

    @pl.when(s == pl.num_programs(1) - 1)
    def _():
        row = lax.broadcasted_iota(jnp.int32, (heads, HEAD_DIM), 0)
        out = jnp.zeros((heads, HEAD_DIM), F32)
        for hb in range(kv_heads):
            out = out + jnp.where(row // GROUP == hb, o[:, hb * HEAD_DIM:(hb + 1) * HEAD_DIM], 0.0)
        o_ref[0] = _rms(out, g_ref[...])


def _sb_decode(q, cache_k, cache_v, page_table, gain, *, kv_heads):
    db, heads, _ = q.shape
    n_pages = page_table.shape[1]
    assert n_pages % PAGES_PER_STEP == 0
    n_steps = n_pages // PAGES_PER_STEP
    width = kv_heads * HEAD_DIM
    kern = functools.partial(_sb_decode_kernel, kv_heads=kv_heads, scale=HEAD_DIM ** -0.5)
    pages = _page_specs(n_pages, PAGES_PER_STEP, width, lambda s: n_steps - 1 - s)
    grid_spec = pltpu.PrefetchScalarGridSpec(
        num_scalar_prefetch=1, grid=(db, n_steps),
        in_specs=[pl.BlockSpec((1, heads, HEAD_DIM), lambda b, s, pt: (b, 0, 0))] + pages + pages
                 + [pl.BlockSpec((heads, HEAD_DIM), lambda b, s, pt: (0, 0))],
        out_specs=pl.BlockSpec((1, heads, HEAD_DIM), lambda b, s, pt: (b, 0, 0)),
        scratch_shapes=[pltpu.VMEM((heads, width), F32), pltpu.VMEM((heads, LANES), F32)])
    return pl.pallas_call(
        kern, out_shape=jax.ShapeDtypeStruct((db, heads, HEAD_DIM), F32), grid_spec=grid_spec,
        compiler_params=_cparams(2), name="sb_decode",
    )(page_table.reshape(-1), q, *([cache_k] * PAGES_PER_STEP), *([cache_v] * PAGES_PER_STEP), gain)


def _diff_decode_kernel(pt_ref, lam_ref, q_ref, kn_ref, vn_ref, rbt_ref, *refs,
                        kv_heads, scale, out_scale, n_steps):
    k_refs = refs[:DIFF_PAGES_PER_STEP]
    v_refs = refs[DIFF_PAGES_PER_STEP:2 * DIFF_PAGES_PER_STEP]
    sub_ref, o_ref, s_ref, m_ref, acc_ref = refs[2 * DIFF_PAGES_PER_STEP:]
    s = pl.program_id(1)
    heads = kv_heads * GROUP
    rows = 2 * heads
    vw = 2 * HEAD_DIM
    lam = lam_ref[0]
    rbt = rbt_ref[...]
    q_bd = _block_diag(q_ref[0], 2 * kv_heads,
                       block_of_row=lambda r: 2 * ((r % heads) // GROUP) + r // heads).astype(BF16)
    k_subs = tuple(range(2 * kv_heads))
    v_subs = tuple(half * kv_heads + hb for hb in range(kv_heads) for half in range(2))

    @pl.when(s == 0)
    def _():
        m_ref[...] = jnp.full_like(m_ref, NEG_BIG)

    @pl.when(s < n_steps)
    def _():
        far = jnp.broadcast_to(rbt[:, N_BUCKETS - 1:N_BUCKETS], (rows, LANES))
        n = LANES - lax.broadcasted_iota(jnp.int32, (rows, LANES), 1)
        near = _bias_from_distance(n, lambda k: jnp.broadcast_to(rbt[:, k:k + 1], (rows, LANES)))
        tiles = []
        for p in range(DIFF_PAGES_PER_STEP):
            bias = jnp.where(s == n_steps - 1, near, far) if p == DIFF_PAGES_PER_STEP - 1 else far
            tiles.append(_nt_dot(q_bd, _page_lanes(k_refs[p], k_subs)) * scale + bias)
        s_ref[s] = jnp.concatenate(tiles, axis=1)
        m_ref[...] = jnp.maximum(m_ref[...], functools.reduce(jnp.maximum, tiles))

    @pl.when(s == n_steps)
    def _():
        k_self = kn_ref[0].astype(BF16).astype(F32)
        s_self = jnp.sum(q_bd.astype(F32) * k_self, axis=-1, keepdims=True) * scale + rbt[:, 0:1]
        m = jnp.maximum(jnp.max(m_ref[...], axis=-1, keepdims=True), s_self)
        e_self = jnp.exp(s_self - m)
        denom = e_self
        for st in range(n_steps):
            e = jnp.exp(s_ref[st] - m)
            s_ref[st] = e
            denom = denom + jnp.sum(e, axis=-1, keepdims=True)
        for st in range(n_steps):
            p = s_ref[st] / denom
            s_ref[st, :heads] = p[:heads] - lam * p[heads:]
        p_self = e_self / denom
        a_self = (p_self[:heads] - lam * p_self[heads:]).astype(BF16).astype(F32)
        acc_ref[...] = a_self * vn_ref[0].astype(BF16).astype(F32)

    @pl.when(s >= n_steps)
    def _():
        attn = s_ref[s - n_steps, :heads].astype(BF16)
        o = acc_ref[...]
        for p in range(DIFF_PAGES_PER_STEP):
            o = o + jnp.dot(attn[:, p * LANES:(p + 1) * LANES], _page_lanes(v_refs[p], v_subs),
                            preferred_element_type=F32)
        acc_ref[...] = o

        @pl.when(s == 2 * n_steps - 1)
        def _():
            row = lax.broadcasted_iota(jnp.int32, (heads, vw), 0)
            out = jnp.zeros((heads, vw), F32)
            for hb in range(kv_heads):
                out = out + jnp.where(row // GROUP == hb, o[:, hb * vw:(hb + 1) * vw], 0.0)
            o_ref[0] = _rms(out, sub_ref[...]) * out_scale


def _diff_decode(q, k_new, v_new, cache_k, cache_v, page_table, rbt, lam, subln, *, kv_heads, out_scale):
    db, rows, _ = q.shape
    heads = rows // 2
    n_pages = page_table.shape[1]
    pps = DIFF_PAGES_PER_STEP
    assert n_pages % pps == 0 and LANES >= MAX_DISTANCE
    n_steps = n_pages // pps
    width = kv_heads * 2 * HEAD_DIM
    vw = 2 * HEAD_DIM
    kern = functools.partial(_diff_decode_kernel, kv_heads=kv_heads, scale=HEAD_DIM ** -0.5, out_scale=out_scale,
                             n_steps=n_steps)
    k_pages = _page_specs(n_pages, pps, width, lambda s: jnp.minimum(s, n_steps - 1))
    v_pages = _page_specs(n_pages, pps, width, lambda s: jnp.maximum(s - n_steps, 0))
    grid_spec = pltpu.PrefetchScalarGridSpec(
        num_scalar_prefetch=1, grid=(db, 2 * n_steps),
        in_specs=[pl.BlockSpec(memory_space=pltpu.SMEM),
                  pl.BlockSpec((1, rows, HEAD_DIM), lambda b, s, pt: (b, 0, 0)),
                  pl.BlockSpec((1, 1, width), lambda b, s, pt: (b, 0, 0)),
                  pl.BlockSpec((1, 1, width), lambda b, s, pt: (b, 0, 0)),
                  pl.BlockSpec((rows, N_BUCKETS), lambda b, s, pt: (0, 0))] + k_pages + v_pages
                 + [pl.BlockSpec((1, vw), lambda b, s, pt: (0, 0))],
        out_specs=pl.BlockSpec((1, heads, vw), lambda b, s, pt: (b, 0, 0)),
        scratch_shapes=[pltpu.VMEM((n_steps, rows, pps * LANES), F32), pltpu.VMEM((rows, LANES), F32),
                        pltpu.VMEM((heads, width), F32)])
    return pl.pallas_call(
        kern, out_shape=jax.ShapeDtypeStruct((db, heads, vw), F32), grid_spec=grid_spec,
        compiler_params=_cparams(2), name="diff_decode",
    )(page_table.reshape(-1), lam.reshape(1), q, k_new, v_new, rbt,
      *([cache_k] * pps), *([cache_v] * pps), subln.reshape(1, vw))


def _router_kernel(x_ref, g_ref, wr_ref, br_ref, *refs, append):
    h_ref, idx_ref, gate_ref = refs[-3:]
    h = _rms(x_ref[...], g_ref[...])
    if append:
        tm = h.shape[0]
        h_ref[:tm] = h
        h_ref[tm:] = jnp.zeros_like(h)
    else:
        h_ref[...] = h
    logits = _nt_dot(wr_ref[...].astype(BF16), h.astype(BF16)) + br_ref[...]
    n_exp = logits.shape[0]
    expert = lax.broadcasted_iota(jnp.int32, logits.shape, 0)
    vals, idxs = [], []
    for _ in range(TOP_K):
        top = jnp.max(logits, axis=0, keepdims=True)
        idx = jnp.min(jnp.where(logits == top, expert, n_exp), axis=0, keepdims=True)
        vals.append(top)
        idxs.append(idx)
        logits = jnp.where(expert == idx, -jnp.inf, logits)
    top_val = jnp.concatenate(vals, axis=0)
    e = jnp.exp(top_val - top_val[0:1])
    gate_ref[...] = e / jnp.sum(e, axis=0, keepdims=True)
    idx_ref[...] = jnp.concatenate(idxs, axis=0)


def _router(x, gain, w_router, b_router, tm, table_rows=None, table=None, row_offset=0):
    t, d = x.shape
    n_exp = w_router.shape[1]
    append = table is not None
    operands = [x, gain.reshape(1, d), w_router.T, b_router.reshape(n_exp, 1)]
    in_specs = [pl.BlockSpec((tm, d), lambda i: (i, 0)), pl.BlockSpec((1, d), lambda i: (0, 0)),
                pl.BlockSpec((n_exp, d), lambda i: (0, 0)), pl.BlockSpec((n_exp, 1), lambda i: (0, 0))]
    if append:
        assert t == tm and row_offset % (2 * t) == 0
        block = row_offset // (2 * t)
        operands.append(table)
        in_specs.append(pl.BlockSpec(memory_space=pl.ANY))
        h_shape, h_spec = table.shape, pl.BlockSpec((2 * t, d), lambda i: (block, 0))
    else:
        h_shape, h_spec = (table_rows or t, d), pl.BlockSpec((tm, d), lambda i: (i, 0))
    return pl.pallas_call(
        functools.partial(_router_kernel, append=append),
        out_shape=(jax.ShapeDtypeStruct(h_shape, F32), jax.ShapeDtypeStruct((TOP_K, t), jnp.int32),
                   jax.ShapeDtypeStruct((TOP_K, t), F32)),
        grid=(t // tm,),
        in_specs=in_specs,
        out_specs=(h_spec, pl.BlockSpec((TOP_K, tm), lambda i: (0, i)),
                   pl.BlockSpec((TOP_K, tm), lambda i: (0, i))),
        input_output_aliases={len(operands) - 1: 0} if append else {},
        compiler_params=_cparams(1), name="router",
    )(*operands)


def _moe_plan(top_idx, n_exp, zero_token):
    n = top_idx.size
    n_chunks = n_exp + n // MOE_CHUNK + 1
    flat_e = top_idx.reshape(-1).astype(jnp.int32)
    counts = jnp.bincount(flat_e, length=n_exp).astype(jnp.int32)
    order = jnp.argsort(flat_e, stable=True).astype(jnp.int32)
    position = jnp.argsort(order).astype(jnp.int32)
    group_start = jnp.cumsum(counts) - counts
    chunks_e = (counts + MOE_CHUNK - 1) // MOE_CHUNK
    chunk_end_e = jnp.cumsum(chunks_e)
    chunk_start_e = chunk_end_e - chunks_e
    dest = chunk_start_e[flat_e] * MOE_CHUNK + position - group_start[flat_e]
    n_used = chunk_end_e[-1]
    cidx = jnp.arange(n_chunks, dtype=jnp.int32)
    c_exp = jnp.clip(jnp.searchsorted(chunk_end_e, cidx, side="right"), 0, n_exp - 1).astype(jnp.int32)
    c_cnt = jnp.clip(counts[c_exp] - (cidx - chunk_start_e[c_exp]) * MOE_CHUNK, 0, MOE_CHUNK)
    used = cidx < n_used
    c_exp = jnp.where(used, c_exp, c_exp[n_used - 1])
    c_cnt = jnp.where(used, c_cnt, 0).astype(jnp.int32)
    c_blk = jnp.where(used, cidx, n_used - 1).astype(jnp.int32)
    c_first = (group_start[c_exp] + (cidx - chunk_start_e[c_exp]) * MOE_CHUNK).astype(jnp.int32)
    return dict(n_chunks=n_chunks, dest=dest.astype(jnp.int32), c_exp=c_exp, c_cnt=c_cnt, c_blk=c_blk,
                n_used=n_used.reshape(1).astype(jnp.int32), c_first=c_first, order=order,
                zero_token=zero_token)


def _row_copy(src_hbm, row, dst_vmem, slot, sem):
    return pltpu.make_async_copy(src_hbm.at[pl.ds(row, 1), :], dst_vmem.at[pl.ds(slot, 1), :], sem)


def _moe_matmul_kernel(ce_ref, cnt_ref, blk_ref, nu_ref, *refs, gated, n_chunks, nj, zero_token):
    c = pl.program_id(0)
    j = pl.program_id(1)
    cnt = cnt_ref[c]
    if gated:
        (first_ref, order_ref, h_hbm, wg_ref, bg_ref, wu_ref, bu_ref, o_ref,
         wgb_ref, wub_ref, xbuf, stage, sem) = refs
        slot = c % 2
        piece = stage.shape[0]
        n_elem = order_ref.shape[0]

        def read_rows(rows):
            return xbuf[slot, rows, :]

        def piece_rows(chunk_cnt, lo):
            live = (chunk_cnt + MOE_TAIL - 1) // MOE_TAIL * MOE_TAIL
            return jnp.clip(live - lo, 0, piece)

        def start_rows(chunk, chunk_cnt, lo, n):
            first = first_ref[chunk]

            def issue(g, carry):
                for u in range(BF16_ROWS):
                    r = g * BF16_ROWS + u
                    elem = order_ref[jnp.minimum(first + lo + r, n_elem - 1)]
                    token = jnp.where(lo + r < chunk_cnt, elem // TOP_K, zero_token)
                    _row_copy(h_hbm, token, stage, r, sem).start()
                return carry

            lax.fori_loop(0, n // BF16_ROWS, issue, 0)

        def finish_rows(dst_slot, lo, n):
            def drain(g, carry):
                for u in range(BF16_ROWS):
                    _row_copy(h_hbm, 0, stage, g * BF16_ROWS + u, sem).wait()
                return carry

            def cast(g, carry):
                src = pl.ds(pl.multiple_of(g * BF16_ROWS, BF16_ROWS), BF16_ROWS)
                dst = pl.ds(pl.multiple_of(lo + g * BF16_ROWS, BF16_ROWS), BF16_ROWS)
                xbuf[dst_slot, dst, :] = stage[src, :].astype(BF16)
                return carry

            lax.fori_loop(0, n // BF16_ROWS, drain, 0)
            lax.fori_loop(0, n // BF16_ROWS, cast, 0)

        @pl.when(jnp.logical_and(c == 0, j == 0))
        def _():
            def first_chunk(p, carry):
                n = piece_rows(cnt_ref[0], p * piece)
                start_rows(0, cnt_ref[0], p * piece, n)
                finish_rows(0, p * piece, n)
                return carry

            lax.fori_loop(0, nj, first_chunk, 0)

        nxt = jnp.minimum(c + 1, n_chunks - 1)
        nxt_cnt = jnp.where(c + 1 < n_chunks, cnt_ref[nxt], 0)
        ahead = piece_rows(nxt_cnt, j * piece)
        start_rows(nxt, nxt_cnt, j * piece, ahead)
    else:
        x_ref, wg_ref, bg_ref, o_ref, wgb_ref = refs

        def read_rows(rows):
            return x_ref[rows, :]

    def cast_weights():
        wgb_ref[...] = wg_ref[0].astype(BF16)
        if gated:
            wub_ref[...] = wu_ref[0].astype(BF16)

    def rows_block(start, size):
        rows = pl.ds(pl.multiple_of(start, MOE_TAIL), size)
        xs = read_rows(rows)
        y = jnp.dot(xs, wgb_ref[...], preferred_element_type=F32) + bg_ref[0]
        if gated:
            up = jnp.dot(xs, wub_ref[...], preferred_element_type=F32) + bu_ref[0]
            gate = jnp.minimum(y, SWIGLU_LIMIT)
            up = jnp.clip(up, -SWIGLU_LIMIT, SWIGLU_LIMIT)
            y = (up + 1.0) * (gate * jax.nn.sigmoid(SWIGLU_ALPHA * gate))
        o_ref[rows, :] = y.astype(o_ref.dtype)

    units = (cnt + MOE_TAIL - 1) // MOE_TAIL
    top = MOE_BLOCKS[0]
    assert MOE_CHUNK < 2 * top
    has_top = units >= top // MOE_TAIL

    @pl.when(has_top)
    def _():
        cast_weights()
        rows_block(0, top)

    @pl.when(jnp.logical_and(jnp.logical_not(has_top), cnt > 0))
    def _():
        cast_weights()

    start = jnp.where(has_top, top, 0)
    units = units - start // MOE_TAIL
    for size in MOE_BLOCKS[1:]:
        take = units >= size // MOE_TAIL

        @pl.when(take)
        def _(start=start, size=size):
            rows_block(start, size)

        start = start + jnp.where(take, size, 0)
        units = units - jnp.where(take, size // MOE_TAIL, 0)

    if gated:
        finish_rows(1 - slot, j * piece, ahead)


def _moe_matmul(x, plan, weights, biases, out_dtype):
    gated = len(weights) == 2
    n_exp, k, n = weights[0].shape
    n_chunks = plan["n_chunks"]
    nj = n // MOE_TN
    n_prefetch = 6 if gated else 4

    def col(c, j, nu):
        return jnp.where(c < nu[0], j, nj - 1)

    def w_map(c, j, ce, cnt, blk, nu, *_):
        return (ce[c], 0, col(c, j, nu))

    def o_map(c, j, ce, cnt, blk, nu, *_):
        return (blk[c], col(c, j, nu))

    w_spec = pl.BlockSpec((1, k, MOE_TN), w_map)
    b_spec = pl.BlockSpec((1, 1, MOE_TN), w_map)
    scratch = [pltpu.VMEM((k, MOE_TN), BF16)] * len(weights)
    if gated:
        assert MOE_CHUNK % nj == 0 and (MOE_CHUNK // nj) % BF16_ROWS == 0
        prefetch = [plan["c_first"], plan["order"]]
        operands, in_specs = [x], [pl.BlockSpec(memory_space=pl.ANY)]
        scratch += [pltpu.VMEM((2, MOE_CHUNK, k), BF16), pltpu.VMEM((MOE_CHUNK // nj, k), F32),
                    pltpu.SemaphoreType.DMA]
    else:
        prefetch = []
        operands = [x]
        in_specs = [pl.BlockSpec((MOE_CHUNK, k), lambda c, j, ce, cnt, blk, nu: (blk[c], 0))]
    for w, b in zip(weights, biases):
        operands += [w, b.reshape(n_exp, 1, n)]
        in_specs += [w_spec, b_spec]
    grid_spec = pltpu.PrefetchScalarGridSpec(
        num_scalar_prefetch=n_prefetch, grid=(n_chunks, nj), in_specs=in_specs,
        out_specs=pl.BlockSpec((MOE_CHUNK, MOE_TN), o_map), scratch_shapes=scratch)
    return pl.pallas_call(
        functools.partial(_moe_matmul_kernel, gated=gated, n_chunks=n_chunks, nj=nj,
                          zero_token=plan["zero_token"]),
        out_shape=jax.ShapeDtypeStruct((n_chunks * MOE_CHUNK, n), out_dtype), grid_spec=grid_spec,
        compiler_params=_cparams(2), name="moe_up" if gated else "moe_down",
    )(plan["c_exp"], plan["c_cnt"], plan["c_blk"], plan["n_used"], *prefetch, *operands)


def _moe_combine_kernel(dest_ref, y_hbm, x_ref, g_ref, o_ref, buf, sem, *, tb):
    base = pl.program_id(0) * tb * TOP_K

    def issue(r, carry):
        for k in range(TOP_K):
            _row_copy(y_hbm, dest_ref[base + r * TOP_K + k], buf.at[k], r, sem).start(priority=k % DMA_THREADS)
        return carry

    def drain(r, carry):
        for k in range(TOP_K):
            _row_copy(y_hbm, 0, buf.at[k], r, sem).wait()
        return carry

    lax.fori_loop(0, tb, issue, 0, unroll=4)
    lax.fori_loop(0, tb, drain, 0, unroll=4)
    gates = g_ref[...]
    moe = gates[:, 0:1] * buf[0]
    for k in range(1, TOP_K):
        moe = moe + gates[:, k:k + 1] * buf[k]
    o_ref[...] = x_ref[...] + moe


def _moe_combine(y_rows, dest, x, gates, tb):
    t, d = x.shape
    grid_spec = pltpu.PrefetchScalarGridSpec(
        num_scalar_prefetch=1, grid=(t // tb,),
        in_specs=[pl.BlockSpec(memory_space=pl.ANY),
                  pl.BlockSpec((tb, d), lambda i, dest: (i, 0)),
                  pl.BlockSpec((tb, TOP_K), lambda i, dest: (i, 0))],
        out_specs=pl.BlockSpec((tb, d), lambda i, dest: (i, 0)),
        scratch_shapes=[pltpu.VMEM((TOP_K, tb, d), F32), pltpu.SemaphoreType.DMA])
    return pl.pallas_call(
        functools.partial(_moe_combine_kernel, tb=tb), out_shape=jax.ShapeDtypeStruct((t, d), F32),
        grid_spec=grid_spec, compiler_params=_cparams(1), name="moe_combine",
    )(dest, y_rows, x, gates)


def _pick_tile(n, candidates):
    for c in candidates:
        if n % c == 0:
            return c
    return n


def kernel(x_prompt, x_sample, cache_sb_k, cache_sb_v, cache_diff_k, cache_diff_v, page_table, attn_norm, w_in, diff_q_norm, diff_k_norm, diff_lambda_q1, diff_lambda_k1, diff_lambda_q2, diff_lambda_k2, rel_bias, sb_out_norm, diff_subln, w_out, ffn_norm, w_router, b_router, w_gate, b_gate, w_up, b_up, w_down, b_down):
    depth = attn_norm.shape[0]
    assert depth == 1, "single-layer trunk"
    bsz, seq, d = x_prompt.shape
    db, dseq, _ = x_sample.shape
    assert dseq == 1
    n_exp = w_router.shape[2]
    half = d // 2
    sb_heads = half // HEAD_DIM
    sb_kv = sb_heads // GROUP
    diff_heads = half // (2 * HEAD_DIM)
    diff_kv = diff_heads // GROUP
    sbq, sbk = sb_heads * HEAD_DIM, sb_kv * HEAD_DIM
    dq, dk = diff_heads * 2 * HEAD_DIM, diff_kv * 2 * HEAD_DIM
    off_sbk, off_sbv = sbq, sbq + sbk
    off_dq = sbq + 2 * sbk
    off_dk = off_dq + dq
    off_dv = off_dk + dk
    in_cols = off_dv + dk
    layer = 0
    lambda_init = 0.8 - 0.6 * math.exp(-0.3 * layer)
    lam = (jnp.exp(jnp.sum(diff_lambda_q1[layer] * diff_lambda_k1[layer]))
           - jnp.exp(jnp.sum(diff_lambda_q2[layer] * diff_lambda_k2[layer])) + lambda_init).astype(F32)

    tn_in = 512
    assert off_dq % tn_in == 0 and off_dv % tn_in == 0
    qk_gain = jnp.concatenate([jnp.ones((off_dq,), F32), jnp.tile(diff_q_norm[layer], dq // HEAD_DIM),
                               jnp.tile(diff_k_norm[layer], dk // HEAD_DIM), jnp.ones((dk,), F32)]).reshape(1, in_cols)
    norm_tiles = dict(norm_lo=off_dq // tn_in, norm_hi=off_dv // tn_in)
    n_tok = bsz * seq

    xp = x_prompt.reshape(n_tok, d)
    h_p = _rmsnorm(xp, attn_norm[layer], BF16, _pick_tile(n_tok, (256, 128, 8)))
    tm_p = _pick_tile(n_tok, (1024, 512, 256, 128, 8))
    proj_p = _matmul((h_p,), w_in[layer], qk_gain, mode="qknorm", tm=tm_p, tn=tn_in, **norm_tiles)
    proj3 = proj_p.reshape(bsz, seq, in_cols)
    mix_sb = _sb_prompt(proj3, sb_out_norm[layer].reshape(-1), kv_heads=sb_kv,
                        q_off=0, k_off=off_sbk, v_off=off_sbv)
    mix_d = _diff_prompt(proj3, rel_bias, lam, diff_subln[layer], kv_heads=diff_kv,
                         q_off=off_dq, k_off=off_dk, v_off=off_dv, out_scale=1.0 - lambda_init)
    x2_p = _matmul((mix_sb.reshape(n_tok, sbq), mix_d.reshape(n_tok, dq)), w_out[layer], xp,
                   mode="residual", tm=tm_p, tn=512)

    xs = x_sample.reshape(db, d)
    h_s = _rmsnorm(xs, attn_norm[layer], BF16, db)
    proj_s = _matmul((h_s,), w_in[layer], qk_gain, mode="qknorm", tm=db, tn=tn_in, **norm_tiles)
    pool = cache_sb_k.shape[1]
    page = cache_sb_k.shape[2]
    assert page == LANES
    rows4 = (pool, page * (sbk // HEAD_DIM), HEAD_DIM)
    assert sbk == dk == 4 * HEAD_DIM
    sb_o = _sb_decode(proj_s[:, :sbq].reshape(db, sb_heads, HEAD_DIM),
                      cache_sb_k.reshape(rows4), cache_sb_v.reshape(rows4),
                      page_table, sb_out_norm[layer], kv_heads=sb_kv)
    diff_v_rows = cache_diff_v.reshape(pool, page, diff_kv, 2, HEAD_DIM).transpose(0, 1, 3, 2, 4).reshape(rows4)
    q_d = proj_s[:, off_dq:off_dk].reshape(db, diff_heads, 2, HEAD_DIM).transpose(0, 2, 1, 3)
    rbt = jnp.concatenate([rel_bias.T, rel_bias.T], axis=0).astype(F32)
    d_o = _diff_decode(q_d.reshape(db, 2 * diff_heads, HEAD_DIM),
                       proj_s[:, off_dk:off_dv].reshape(db, 1, dk), proj_s[:, off_dv:].reshape(db, 1, dk),
                       cache_diff_k.reshape(rows4), diff_v_rows,
                       page_table, rbt, lam, diff_subln[layer], kv_heads=diff_kv, out_scale=1.0 - lambda_init)
    x2_s = _matmul((sb_o.reshape(db, sbq).astype(BF16), d_o.reshape(db, dq).astype(BF16)), w_out[layer], xs,
                   mode="residual", tm=db, tn=512)

    h_table, idx_p, gate_p = _router(x2_p, ffn_norm[layer], w_router[layer], b_router[layer],
                                     _pick_tile(n_tok, (256, 128)), table_rows=n_tok + 2 * db)
    h_all, idx_s, gate_s = _router(x2_s, ffn_norm[layer], w_router[layer], b_router[layer], db,
                                   table=h_table, row_offset=n_tok)
    top_idx = jnp.concatenate([idx_p, idx_s], axis=1).T
    plan = _moe_plan(top_idx, n_exp, zero_token=n_tok + db)
    act = _moe_matmul(h_all, plan, (w_gate[layer], w_up[layer]), (b_gate[layer], b_up[layer]), BF16)
    y_rows = _moe_matmul(act, plan, (w_down[layer],), (b_down[layer],), F32)
    dest = plan["dest"]
    y_p = _moe_combine(y_rows, dest[:n_tok * TOP_K], x2_p, gate_p.T, _pick_tile(n_tok, (COMBINE_TOKENS, 8)))
    y_s = _moe_combine(y_rows, dest[n_tok * TOP_K:], x2_s, gate_s.T, db)

    def rows(p, lead, lo, hi, shape):
        return p[:, lo:hi].reshape((depth,) + lead + shape)

    lead_p, lead_s = (bsz, seq), (db, dseq)
    return (y_p.reshape(bsz, seq, d), y_s.reshape(db, dseq, d),
            rows(proj_p, lead_p, off_sbk, off_sbv, (sb_kv, HEAD_DIM)),
            rows(proj_p, lead_p, off_sbv, off_dq, (sb_kv, HEAD_DIM)),
            rows(proj_p, lead_p, off_dk, off_dv, (diff_kv, 2, HEAD_DIM)),
            rows(proj_p, lead_p, off_dv, in_cols, (diff_kv, 2 * HEAD_DIM)),
            rows(proj_s, lead_s, off_sbk, off_sbv, (sb_kv, HEAD_DIM)),
            rows(proj_s, lead_s, off_sbv, off_dq, (sb_kv, HEAD_DIM)),
            rows(proj_s, lead_s, off_dk, off_dv, (diff_kv, 2, HEAD_DIM)),
            rows(proj_s, lead_s, off_dv, in_cols, (diff_kv, 2 * HEAD_DIM)))
```

```python
import functools
import math

import numpy as np
import jax
import jax.numpy as jnp
from jax import lax
from jax.experimental import pallas as pl
from jax.experimental.pallas import tpu as pltpu

F32 = jnp.float32
BF16 = jnp.bfloat16

HEAD_DIM = 128
GROUP = 4
N_BUCKETS = 32
MAX_EXACT = N_BUCKETS // 2
MAX_DISTANCE = 128
TOP_K = 4
SWIGLU_LIMIT = 7.0
SWIGLU_ALPHA = 1.702
EPS = 1e-5
NEG_BIG = -1e30

V7X_VMEM_LIMIT_BYTES = 56 * 1024 * 1024
LANES = 128
ATTN_BLOCK = 128
PAGES_PER_STEP = 16
DIFF_PAGES_PER_STEP = 16
MOE_TAIL = 128
MOE_BLOCKS = (1024, 512, 256, 128)
MOE_CHUNK = 1280
BF16_ROWS = 16
MOE_TN = 256
MOE_DOWN_TN = 512
COMBINE_TOKENS = 64
DMA_THREADS = 2


def _cparams(n_axes):
    return pltpu.CompilerParams(dimension_semantics=("arbitrary",) * n_axes,
                                vmem_limit_bytes=V7X_VMEM_LIMIT_BYTES)


def _nt_dot(a, b, precision=None):
    return lax.dot_general(a, b, (((1,), (1,)), ((), ())), precision=precision,
                           preferred_element_type=F32)


def _rms(x, gain):
    ms = jnp.mean(x * x, axis=-1, keepdims=True)
    return x * lax.rsqrt(ms + EPS) * gain


def _bucket_thresholds():
    n = np.arange(MAX_DISTANCE + 1)
    nf = np.maximum(n, MAX_EXACT).astype(np.float32)
    large = MAX_EXACT + (np.log(nf / np.float32(MAX_EXACT)) / np.float32(math.log(MAX_DISTANCE / MAX_EXACT))
                         * np.float32(N_BUCKETS - MAX_EXACT)).astype(np.int32)
    large = np.minimum(large, N_BUCKETS - 1)
    bucket = np.where(n < MAX_EXACT, n, large)
    return [int(np.argmax(bucket >= k)) for k in range(1, N_BUCKETS)]


_BUCKET_THR = _bucket_thresholds()


def _rmsnorm_kernel(x_ref, g_ref, o_ref):
    o_ref[...] = _rms(x_ref[...], g_ref[...]).astype(o_ref.dtype)


def _rmsnorm(x, gain, out_dtype, tm):
    t, d = x.shape
    return pl.pallas_call(
        _rmsnorm_kernel,
        out_shape=jax.ShapeDtypeStruct((t, d), out_dtype),
        grid=(t // tm,),
        in_specs=[pl.BlockSpec((tm, d), lambda i: (i, 0)), pl.BlockSpec((1, d), lambda i: (0, 0))],
        out_specs=pl.BlockSpec((tm, d), lambda i: (i, 0)),
        compiler_params=_cparams(1), name="rmsnorm",
    )(x, gain.reshape(1, d))


def _matmul_kernel(*refs, n_parts, mode, norm_lo, norm_hi, tn):
    a_refs = refs[:n_parts]
    w_ref, e_ref, o_ref, wb_ref = refs[n_parts:]
    j = pl.program_id(0)
    i = pl.program_id(1)

    @pl.when(i == 0)
    def _():
        wb_ref[...] = w_ref[...].astype(BF16)

    acc, k_lo = None, 0
    for a_ref in a_refs:
        k_hi = k_lo + a_ref.shape[1]
        part = jnp.dot(a_ref[...], wb_ref[k_lo:k_hi, :], preferred_element_type=F32)
        acc = part if acc is None else acc + part
        k_lo = k_hi
    if mode == "residual":
        o_ref[...] = e_ref[...] + acc
    else:
        in_range = jnp.logical_and(j >= norm_lo, j < norm_hi)

        @pl.when(in_range)
        def _():
            for c in range(tn // LANES):
                sl = slice(c * LANES, (c + 1) * LANES)
                o_ref[:, sl] = _rms(acc[:, sl], e_ref[:, sl])

        @pl.when(jnp.logical_not(in_range))
        def _():
            o_ref[...] = acc


def _matmul(a_parts, w, extra, *, mode, tm, tn, norm_lo=0, norm_hi=0):
    m = a_parts[0].shape[0]
    k, n = w.shape
    assert sum(a.shape[1] for a in a_parts) == k
    if mode == "residual":
        e_spec = pl.BlockSpec((tm, tn), lambda j, i: (i, j))
    else:
        e_spec = pl.BlockSpec((1, tn), lambda j, i: (0, j))
    kern = functools.partial(_matmul_kernel, n_parts=len(a_parts), mode=mode, norm_lo=norm_lo, norm_hi=norm_hi,
                             tn=tn)
    return pl.pallas_call(
        kern,
        out_shape=jax.ShapeDtypeStruct((m, n), F32),
        grid=(n // tn, m // tm),
        in_specs=[pl.BlockSpec((tm, a.shape[1]), lambda j, i: (i, 0)) for a in a_parts]
                 + [pl.BlockSpec((k, tn), lambda j, i: (0, j)), e_spec],
        out_specs=pl.BlockSpec((tm, tn), lambda j, i: (i, j)),
        scratch_shapes=[pltpu.VMEM((k, tn), BF16)],
        compiler_params=_cparams(2), name="matmul_" + mode,
    )(*a_parts, w, extra)


def _softplus(z):
    return jnp.maximum(z, 0.0) + jnp.log(1.0 + jnp.exp(-jnp.abs(z)))


def _sb_prompt_kernel(q_ref, k_ref, v_ref, g_ref, o_ref, acc_ref, *, bq, scale):
    qi = pl.program_id(2)
    rows = GROUP * bq
    q = jnp.concatenate([q_ref[0, :, g * HEAD_DIM:(g + 1) * HEAD_DIM] for g in range(GROUP)], axis=0)
    qs = (q * scale).astype(BF16)
    row_i = lax.broadcasted_iota(jnp.int32, (bq, bq), 0)
    col_i = lax.broadcasted_iota(jnp.int32, (bq, bq), 1)
    later = jnp.where(row_i > col_i, 1.0, 0.0).astype(BF16)
    visible = jnp.concatenate([col_i < row_i] * GROUP, axis=0)

    later2 = jnp.concatenate([later, later], axis=0)

    def weights(z, carry, masked):
        sp = _softplus(z)
        log_keep = jnp.where(visible, -sp, 0.0) if masked else -sp
        hi, lo = _split_bf16(log_keep)
        between = jnp.dot(jnp.concatenate([hi, lo], axis=1), later2, preferred_element_type=F32) + carry
        w = jnp.exp(z - sp + between)
        if masked:
            w = jnp.where(visible, w, 0.0)
        return w.astype(BF16), carry + jnp.sum(log_keep, axis=-1, keepdims=True)

    def single(kb, carry, masked):
        start = pl.multiple_of(kb * bq, bq)
        k = k_ref[0, pl.ds(start, bq), :].astype(BF16)
        v = v_ref[0, pl.ds(start, bq), :].astype(BF16)
        w, carry = weights(_nt_dot(qs, k), carry, masked)
        acc_ref[...] += jnp.dot(w, v, preferred_element_type=F32)
        return carry

    n_pairs = qi // 2

    def pair(t, carry):
        start = pl.multiple_of((2 * (n_pairs - 1 - t)) * bq, bq)
        k = k_ref[0, pl.ds(start, 2 * bq), :].astype(BF16)
        v = v_ref[0, pl.ds(start, 2 * bq), :].astype(BF16)
        z = _nt_dot(qs, k)
        w_late, carry = weights(z[:, bq:], carry, False)
        w_early, carry = weights(z[:, :bq], carry, False)
        acc_ref[...] += jnp.dot(jnp.concatenate([w_early, w_late], axis=1), v, preferred_element_type=F32)
        return carry

    acc_ref[...] = jnp.zeros_like(acc_ref)
    carry = single(qi, jnp.zeros((rows, 1), F32), True)
    carry = lax.cond(qi % 2 == 1, lambda c: single(qi - 1, c, False), lambda c: c, carry)
    lax.fori_loop(0, n_pairs, pair, carry)

    acc = acc_ref[...]
    for g in range(GROUP):
        sl = slice(g * HEAD_DIM, (g + 1) * HEAD_DIM)
        o_ref[0, :, sl] = _rms(acc[g * bq:(g + 1) * bq, :], g_ref[:, sl]).astype(o_ref.dtype)


def _sb_prompt(proj3, gain, *, kv_heads, q_off, k_off, v_off):
    b, t, _ = proj3.shape
    bq = ATTN_BLOCK
    qw = GROUP * HEAD_DIM
    width = kv_heads * qw
    assert q_off % qw == 0 and k_off % HEAD_DIM == 0 and v_off % HEAD_DIM == 0 and t % bq == 0
    kern = functools.partial(_sb_prompt_kernel, bq=bq, scale=HEAD_DIM ** -0.5)
    return pl.pallas_call(
        kern,
        out_shape=jax.ShapeDtypeStruct((b, t, width), BF16),
        grid=(b, kv_heads, t // bq),
        in_specs=[pl.BlockSpec((1, bq, qw), lambda bi, h, qi: (bi, qi, q_off // qw + h)),
                  pl.BlockSpec((1, t, HEAD_DIM), lambda bi, h, qi: (bi, 0, k_off // HEAD_DIM + h)),
                  pl.BlockSpec((1, t, HEAD_DIM), lambda bi, h, qi: (bi, 0, v_off // HEAD_DIM + h)),
                  pl.BlockSpec((1, qw), lambda bi, h, qi: (0, h))],
        out_specs=pl.BlockSpec((1, bq, qw), lambda bi, h, qi: (bi, qi, h)),
        scratch_shapes=[pltpu.VMEM((GROUP * bq, HEAD_DIM), F32)],
        compiler_params=_cparams(3), name="sb_prompt",
    )(proj3, proj3, proj3, gain.reshape(1, width))


def _bias_from_distance(n, table):
    bias = table(0)
    for k in range(1, N_BUCKETS):
        bias = jnp.where(n >= _BUCKET_THR[k - 1], table(k), bias)
    return bias


def _diff_prompt_kernel(rb_ref, lam_ref, q_ref, k_ref, v_ref, sub_ref, o_ref,
                        tiles_ref, s_ref, m_ref, l_ref, acc_ref, *, bq, scale, kv_heads, out_scale):
    bi = pl.program_id(0)
    h = pl.program_id(1)
    qi = pl.program_id(2)
    rows = GROUP * bq
    vw = 2 * HEAD_DIM
    row_i = lax.broadcasted_iota(jnp.int32, (bq, bq), 0)
    col_i = lax.broadcasted_iota(jnp.int32, (bq, bq), 1)

    @pl.when(jnp.logical_and(jnp.logical_and(bi == 0, h == 0), qi == 0))
    def _():
        for kind in range(2):
            n = jnp.maximum(row_i - col_i + kind * bq, 0)
            for hh in range(kv_heads):
                for g in range(GROUP):
                    head = hh * GROUP + g
                    tiles_ref[kind, hh, g * bq:(g + 1) * bq, :] = _bias_from_distance(
                        n, lambda k, head=head: rb_ref[k, head])

    far_bias = jnp.concatenate(
        [jnp.full((bq, 1), rb_ref[N_BUCKETS - 1, h * GROUP + g], F32) for g in range(GROUP)], axis=0)
    causal = jnp.concatenate([col_i <= row_i] * GROUP, axis=0)
    qs = []
    for c in range(2):
        qc = jnp.concatenate([q_ref[0, :, (2 * g + c) * HEAD_DIM:(2 * g + c + 1) * HEAD_DIM]
                              for g in range(GROUP)], axis=0)
        qs.append((qc * scale).astype(BF16))

    def logits(c, start, width):
        k = k_ref[0, pl.ds(pl.multiple_of(start, bq), width), c * HEAD_DIM:(c + 1) * HEAD_DIM].astype(BF16)
        return _nt_dot(qs[c], k)

    n_far = jnp.maximum(qi - 1, 0)
    n_far_pairs = n_far // 2
    m_ref[...] = jnp.full_like(m_ref, NEG_BIG)

    def far_pair(j, carry):
        for c in range(2):
            s = logits(c, j * 2 * bq, 2 * bq)
            m_ref[c] = jnp.maximum(m_ref[c], jnp.maximum(s[:, :bq], s[:, bq:]))
            s_ref[c, 2 * j] = s[:, :bq] + far_bias
            s_ref[c, 2 * j + 1] = s[:, bq:] + far_bias
        return carry

    lax.fori_loop(0, n_far_pairs, far_pair, 0)

    @pl.when(n_far % 2 == 1)
    def _():
        for c in range(2):
            s = logits(c, (n_far - 1) * bq, bq)
            m_ref[c] = jnp.maximum(m_ref[c], s)
            s_ref[c, n_far - 1] = s + far_bias

    for c in range(2):
        m_ref[c] = m_ref[c] + far_bias

    @pl.when(qi >= 1)
    def _():
        for c in range(2):
            s = logits(c, (qi - 1) * bq, bq) + tiles_ref[1, h]
            m_ref[c] = jnp.maximum(m_ref[c], s)
            s_ref[c, qi - 1] = s

    for c in range(2):
        s = jnp.where(causal, logits(c, qi * bq, bq) + tiles_ref[0, h], NEG_BIG)
        s_ref[c, qi] = s
        m_ref[c] = jnp.broadcast_to(jnp.max(jnp.maximum(m_ref[c], s), axis=-1, keepdims=True), (rows, bq))

    @pl.when(qi % 2 == 0)
    def _():
        for c in range(2):
            s_ref[c, qi + 1] = jnp.full((rows, bq), NEG_BIG, F32)

    l_ref[...] = jnp.zeros_like(l_ref)
    acc_ref[...] = jnp.zeros_like(acc_ref)

    def accumulate(j, carry):
        v = v_ref[0, pl.ds(pl.multiple_of(j * 2 * bq, 2 * bq), 2 * bq), :].astype(BF16)
        for c in range(2):
            p0 = jnp.exp(s_ref[c, 2 * j] - m_ref[c])
            p1 = jnp.exp(s_ref[c, 2 * j + 1] - m_ref[c])
            l_ref[c] += p0 + p1
            acc_ref[c] += jnp.dot(jnp.concatenate([p0, p1], axis=1).astype(BF16), v, preferred_element_type=F32)
        return carry

    lax.fori_loop(0, qi // 2 + 1, accumulate, 0)

    lam = lam_ref[0]
    l0 = jnp.sum(l_ref[0], axis=-1, keepdims=True)
    l1 = jnp.sum(l_ref[1], axis=-1, keepdims=True)
    o = acc_ref[0] / l0 - lam * (acc_ref[1] / l1)
    for g in range(GROUP):
        o_ref[0, :, g * vw:(g + 1) * vw] = (_rms(o[g * bq:(g + 1) * bq, :], sub_ref[...]) * out_scale
                                            ).astype(o_ref.dtype)


def _diff_prompt(proj3, rel_bias, lam, subln, *, kv_heads, q_off, k_off, v_off, out_scale):
    b, t, _ = proj3.shape
    bq = ATTN_BLOCK
    qw = GROUP * 2 * HEAD_DIM
    vw = 2 * HEAD_DIM
    width = kv_heads * GROUP * vw
    assert q_off % qw == 0 and k_off % vw == 0 and v_off % vw == 0 and t % (2 * bq) == 0
    assert bq + 1 >= _BUCKET_THR[-1]
    kern = functools.partial(_diff_prompt_kernel, bq=bq, scale=HEAD_DIM ** -0.5, kv_heads=kv_heads,
                             out_scale=out_scale)
    smem = pl.BlockSpec(memory_space=pltpu.SMEM)
    return pl.pallas_call(
        kern,
        out_shape=jax.ShapeDtypeStruct((b, t, width), BF16),
        grid=(b, kv_heads, t // bq),
        in_specs=[smem, smem,
                  pl.BlockSpec((1, bq, qw), lambda bi, h, qi: (bi, qi, q_off // qw + h)),
                  pl.BlockSpec((1, t, vw), lambda bi, h, qi: (bi, 0, k_off // vw + h)),
                  pl.BlockSpec((1, t, vw), lambda bi, h, qi: (bi, 0, v_off // vw + h)),
                  pl.BlockSpec((1, vw), lambda bi, h, qi: (0, 0))],
        out_specs=pl.BlockSpec((1, bq, GROUP * vw), lambda bi, h, qi: (bi, qi, h)),
        scratch_shapes=[pltpu.VMEM((2, kv_heads, GROUP * bq, bq), F32),
                        pltpu.VMEM((2, t // bq, GROUP * bq, bq), F32),
                        pltpu.VMEM((2, GROUP * bq, bq), F32),
                        pltpu.VMEM((2, GROUP * bq, bq), F32),
                        pltpu.VMEM((2, GROUP * bq, vw), F32)],
        compiler_params=_cparams(3), name="diff_prompt",
    )(rel_bias, lam.reshape(1), proj3, proj3, proj3, subln.reshape(1, vw))


def _block_diag(q, n_blocks, block_of_row):
    r = q.shape[0]
    row = lax.broadcasted_iota(jnp.int32, (r, HEAD_DIM), 0)
    blk = block_of_row(row)
    return jnp.concatenate([jnp.where(blk == hb, q, 0.0) for hb in range(n_blocks)], axis=1)


def _split_bf16(x):
    hi = x.astype(BF16)
    lo = (x - hi.astype(F32)).astype(BF16)
    return hi, lo


def _page_lanes(ref, subs):
    n = ref.shape[1] // LANES
    return jnp.concatenate([ref[0, pl.ds(sub, LANES, stride=n), :] for sub in subs], axis=1).astype(BF16)


def _page_specs(n_pages, pages_per_step, width, step_of):
    specs = []
    for n in range(pages_per_step):
        def imap(b, s, pt, n=n):
            return (pt[b * n_pages + step_of(s) * pages_per_step + n], 0, 0)
        specs.append(pl.BlockSpec((1, width, LANES), imap))
    return specs


def _sb_decode_kernel(pt_ref, q_ref, *refs, kv_heads, scale):
    k_refs = refs[:PAGES_PER_STEP]
    v_refs = refs[PAGES_PER_STEP:2 * PAGES_PER_STEP]
    g_ref, o_ref, acc_ref, carry_ref = refs[2 * PAGES_PER_STEP:]
    s = pl.program_id(1)
    heads = kv_heads * GROUP
    width = PAGES_PER_STEP * LANES

    @pl.when(s == 0)
    def _():
        acc_ref[...] = jnp.zeros_like(acc_ref)
        carry_ref[...] = jnp.zeros_like(carry_ref)

    q_bd = _block_diag(q_ref[0], kv_heads, block_of_row=lambda r: r // GROUP).astype(BF16)
    subs = tuple(range(kv_heads))
    z = jnp.concatenate([_nt_dot(q_bd, _page_lanes(k_refs[p], subs)) for p in range(PAGES_PER_STEP)],
                        axis=1) * scale
    sp = _softplus(z)
    log_keep = -sp
    lane = lax.broadcasted_iota(jnp.int32, (heads, width), 1) % LANES
    suffix = log_keep
    sh = 1
    while sh < LANES:
        suffix = suffix + jnp.where(lane + sh < LANES, pltpu.roll(suffix, width - sh, 1), 0.0)
        sh *= 2
    run = carry_ref[...]
    offsets = [None] * PAGES_PER_STEP
    for p in reversed(range(PAGES_PER_STEP)):
        offsets[p] = run
        run = run + jnp.broadcast_to(suffix[:, p * LANES:p * LANES + 1], (heads, LANES))
    carry_ref[...] = run
    between = suffix - log_keep + jnp.concatenate(offsets, axis=1)
    w = jnp.exp(z - sp + between)
    o = acc_ref[...]
    for p in range(PAGES_PER_STEP):
        o = o + jnp.dot(w[:, p * LANES:(p + 1) * LANES].astype(BF16), _page_lanes(v_refs[p], subs),
                        preferred_element_type=F32)
    acc_ref[...] = o

    @pl.when(s == pl.num_programs(1) - 1)
    def _():
        row = lax.broadcasted_iota(jnp.int32, (heads, HEAD_DIM), 0)
        out = jnp.zeros((heads, HEAD_DIM), F32)
        for hb in range(kv_heads):
            out = out + jnp.where(row // GROUP == hb, o[:, hb * HEAD_DIM:(hb + 1) * HEAD_DIM], 0.0)
        o_ref[0] = _rms(out, g_ref[...])


def _sb_decode(q, cache_k, cache_v, page_table, gain, *, kv_heads):
    db, heads, _ = q.shape
    n_pages = page_table.shape[1]
    assert n_pages % PAGES_PER_STEP == 0
    n_steps = n_pages // PAGES_PER_STEP
    width = kv_heads * HEAD_DIM
    kern = functools.partial(_sb_decode_kernel, kv_heads=kv_heads, scale=HEAD_DIM ** -0.5)
    pages = _page_specs(n_pages, PAGES_PER_STEP, width, lambda s: n_steps - 1 - s)
    grid_spec = pltpu.PrefetchScalarGridSpec(
        num_scalar_prefetch=1, grid=(db, n_steps),
        in_specs=[pl.BlockSpec((1, heads, HEAD_DIM), lambda b, s, pt: (b, 0, 0))] + pages + pages
                 + [pl.BlockSpec((heads, HEAD_DIM), lambda b, s, pt: (0, 0))],
        out_specs=pl.BlockSpec((1, heads, HEAD_DIM), lambda b, s, pt: (b, 0, 0)),
        scratch_shapes=[pltpu.VMEM((heads, width), F32), pltpu.VMEM((heads, LANES), F32)])
    return pl.pallas_call(
        kern, out_shape=jax.ShapeDtypeStruct((db, heads, HEAD_DIM), F32), grid_spec=grid_spec,
        compiler_params=_cparams(2), name="sb_decode",
    )(page_table.reshape(-1), q, *([cache_k] * PAGES_PER_STEP), *([cache_v] * PAGES_PER_STEP), gain)


def _diff_decode_kernel(pt_ref, lam_ref, q_ref, kn_ref, vn_ref, rbt_ref, *refs,
                        kv_heads, scale, out_scale, n_steps):
    k_refs = refs[:DIFF_PAGES_PER_STEP]
    v_refs = refs[DIFF_PAGES_PER_STEP:2 * DIFF_PAGES_PER_STEP]
    sub_ref, o_ref, s_ref, m_ref, acc_ref = refs[2 * DIFF_PAGES_PER_STEP:]
    s = pl.program_id(1)
    heads = kv_heads * GROUP
    rows = 2 * heads
    vw = 2 * HEAD_DIM
    lam = lam_ref[0]
    rbt = rbt_ref[...]
    q_bd = _block_diag(q_ref[0], 2 * kv_heads,
                       block_of_row=lambda r: 2 * ((r % heads) // GROUP) + r // heads).astype(BF16)
    k_subs = tuple(range(2 * kv_heads))
    v_subs = tuple(half * kv_heads + hb for hb in range(kv_heads) for half in range(2))

    @pl.when(s == 0)
    def _():
        m_ref[...] = jnp.full_like(m_ref, NEG_BIG)

    @pl.when(s < n_steps)
    def _():
        far = jnp.broadcast_to(rbt[:, N_BUCKETS - 1:N_BUCKETS], (rows, LANES))
        n = LANES - lax.broadcasted_iota(jnp.int32, (rows, LANES), 1)
        near = _bias_from_distance(n, lambda k: jnp.broadcast_to(rbt[:, k:k + 1], (rows, LANES)))
        tiles = []
        for p in range(DIFF_PAGES_PER_STEP):
            bias = jnp.where(s == n_steps - 1, near, far) if p == DIFF_PAGES_PER_STEP - 1 else far
            tiles.append(_nt_dot(q_bd, _page_lanes(k_refs[p], k_subs)) * scale + bias)
        s_ref[s] = jnp.concatenate(tiles, axis=1)
        m_ref[...] = jnp.maximum(m_ref[...], functools.reduce(jnp.maximum, tiles))

    @pl.when(s == n_steps)
    def _():
        k_self = kn_ref[0].astype(BF16).astype(F32)
        s_self = jnp.sum(q_bd.astype(F32) * k_self, axis=-1, keepdims=True) * scale + rbt[:, 0:1]
        m = jnp.maximum(jnp.max(m_ref[...], axis=-1, keepdims=True), s_self)
        e_self = jnp.exp(s_self - m)
        denom = e_self
        for st in range(n_steps):
            e = jnp.exp(s_ref[st] - m)
            s_ref[st] = e
            denom = denom + jnp.sum(e, axis=-1, keepdims=True)
        for st in range(n_steps):
            p = s_ref[st] / denom
            s_ref[st, :heads] = p[:heads] - lam * p[heads:]
        p_self = e_self / denom
        a_self = (p_self[:heads] - lam * p_self[heads:]).astype(BF16).astype(F32)
        acc_ref[...] = a_self * vn_ref[0].astype(BF16).astype(F32)

    @pl.when(s >= n_steps)
    def _():
        attn = s_ref[s - n_steps, :heads].astype(BF16)
        o = acc_ref[...]
        for p in range(DIFF_PAGES_PER_STEP):
            o = o + jnp.dot(attn[:, p * LANES:(p + 1) * LANES], _page_lanes(v_refs[p], v_subs),
                            preferred_element_type=F32)
        acc_ref[...] = o

        @pl.when(s == 2 * n_steps - 1)
        def _():
            row = lax.broadcasted_iota(jnp.int32, (heads, vw), 0)
            out = jnp.zeros((heads, vw), F32)
            for hb in range(kv_heads):
                out = out + jnp.where(row // GROUP == hb, o[:, hb * vw:(hb + 1) * vw], 0.0)
            o_ref[0] = _rms(out, sub_ref[...]) * out_scale


def _diff_decode(q, k_new, v_new, cache_k, cache_v, page_table, rbt, lam, subln, *, kv_heads, out_scale):
    db, rows, _ = q.shape
    heads = rows // 2
    n_pages = page_table.shape[1]
    pps = DIFF_PAGES_PER_STEP
    assert n_pages % pps == 0 and LANES >= MAX_DISTANCE
    n_steps = n_pages // pps
    width = kv_heads * 2 * HEAD_DIM
    vw = 2 * HEAD_DIM
    kern = functools.partial(_diff_decode_kernel, kv_heads=kv_heads, scale=HEAD_DIM ** -0.5, out_scale=out_scale,
                             n_steps=n_steps)
    k_pages = _page_specs(n_pages, pps, width, lambda s: jnp.minimum(s, n_steps - 1))
    v_pages = _page_specs(n_pages, pps, width, lambda s: jnp.maximum(s - n_steps, 0))
    grid_spec = pltpu.PrefetchScalarGridSpec(
        num_scalar_prefetch=1, grid=(db, 2 * n_steps),
        in_specs=[pl.BlockSpec(memory_space=pltpu.SMEM),
                  pl.BlockSpec((1, rows, HEAD_DIM), lambda b, s, pt: (b, 0, 0)),
                  pl.BlockSpec((1, 1, width), lambda b, s, pt: (b, 0, 0)),
                  pl.BlockSpec((1, 1, width), lambda b, s, pt: (b, 0, 0)),
                  pl.BlockSpec((rows, N_BUCKETS), lambda b, s, pt: (0, 0))] + k_pages + v_pages
                 + [pl.BlockSpec((1, vw), lambda b, s, pt: (0, 0))],
        out_specs=pl.BlockSpec((1, heads, vw), lambda b, s, pt: (b, 0, 0)),
        scratch_shapes=[pltpu.VMEM((n_steps, rows, pps * LANES), F32), pltpu.VMEM((rows, LANES), F32),
                        pltpu.VMEM((heads, width), F32)])
    return pl.pallas_call(
        kern, out_shape=jax.ShapeDtypeStruct((db, heads, vw), F32), grid_spec=grid_spec,
        compiler_params=_cparams(2), name="diff_decode",
    )(page_table.reshape(-1), lam.reshape(1), q, k_new, v_new, rbt,
      *([cache_k] * pps), *([cache_v] * pps), subln.reshape(1, vw))


def _router_kernel(x_ref, g_ref, wr_ref, br_ref, *refs, append):
    h_ref, idx_ref, gate_ref = refs[-3:]
    h = _rms(x_ref[...], g_ref[...])
    if append:
        tm = h.shape[0]
        h_ref[:tm] = h
        h_ref[tm:] = jnp.zeros_like(h)
    else:
        h_ref[...] = h
    logits = _nt_dot(wr_ref[...].astype(BF16), h.astype(BF16)) + br_ref[...]
    n_exp = logits.shape[0]
    expert = lax.broadcasted_iota(jnp.int32, logits.shape, 0)
    vals, idxs = [], []
    for _ in range(TOP_K):
        top = jnp.max(logits, axis=0, keepdims=True)
        idx = jnp.min(jnp.where(logits == top, expert, n_exp), axis=0, keepdims=True)
        vals.append(top)
        idxs.append(idx)
        logits = jnp.where(expert == idx, -jnp.inf, logits)
    top_val = jnp.concatenate(vals, axis=0)
    e = jnp.exp(top_val - top_val[0:1])
    gate_ref[...] = e / jnp.sum(e, axis=0, keepdims=True)
    idx_ref[...] = jnp.concatenate(idxs, axis=0)


def _router(x, gain, w_router, b_router, tm, table_rows=None, table=None, row_offset=0):
    t, d = x.shape
    n_exp = w_router.shape[1]
    append = table is not None
    operands = [x, gain.reshape(1, d), w_router.T, b_router.reshape(n_exp, 1)]
    in_specs = [pl.BlockSpec((tm, d), lambda i: (i, 0)), pl.BlockSpec((1, d), lambda i: (0, 0)),
                pl.BlockSpec((n_exp, d), lambda i: (0, 0)), pl.BlockSpec((n_exp, 1), lambda i: (0, 0))]
    if append:
        assert t == tm and row_offset % (2 * t) == 0
        block = row_offset // (2 * t)
        operands.append(table)
        in_specs.append(pl.BlockSpec(memory_space=pl.ANY))
        h_shape, h_spec = table.shape, pl.BlockSpec((2 * t, d), lambda i: (block, 0))
    else:
        h_shape, h_spec = (table_rows or t, d), pl.BlockSpec((tm, d), lambda i: (i, 0))
    return pl.pallas_call(
        functools.partial(_router_kernel, append=append),
        out_shape=(jax.ShapeDtypeStruct(h_shape, F32), jax.ShapeDtypeStruct((TOP_K, t), jnp.int32),
                   jax.ShapeDtypeStruct((TOP_K, t), F32)),
        grid=(t // tm,),
        in_specs=in_specs,
        out_specs=(h_spec, pl.BlockSpec((TOP_K, tm), lambda i: (0, i)),
                   pl.BlockSpec((TOP_K, tm), lambda i: (0, i))),
        input_output_aliases={len(operands) - 1: 0} if append else {},
        compiler_params=_cparams(1), name="router",
    )(*operands)


def _moe_plan(top_idx, n_exp, zero_token):
    n = top_idx.size
    n_chunks = n_exp + n // MOE_CHUNK + 1
    flat_e = top_idx.reshape(-1).astype(jnp.int32)
    counts = jnp.bincount(flat_e, length=n_exp).astype(jnp.int32)
    order = jnp.argsort(flat_e, stable=True).astype(jnp.int32)
    position = jnp.argsort(order).astype(jnp.int32)
    group_start = jnp.cumsum(counts) - counts
    chunks_e = (counts + MOE_CHUNK - 1) // MOE_CHUNK
    chunk_end_e = jnp.cumsum(chunks_e)
    chunk_start_e = chunk_end_e - chunks_e
    dest = chunk_start_e[flat_e] * MOE_CHUNK + position - group_start[flat_e]
    n_used = chunk_end_e[-1]
    cidx = jnp.arange(n_chunks, dtype=jnp.int32)
    c_exp = jnp.clip(jnp.searchsorted(chunk_end_e, cidx, side="right"), 0, n_exp - 1).astype(jnp.int32)
    c_cnt = jnp.clip(counts[c_exp] - (cidx - chunk_start_e[c_exp]) * MOE_CHUNK, 0, MOE_CHUNK)
    used = cidx < n_used
    c_exp = jnp.where(used, c_exp, c_exp[n_used - 1])
    c_cnt = jnp.where(used, c_cnt, 0).astype(jnp.int32)
    c_blk = jnp.where(used, cidx, n_used - 1).astype(jnp.int32)
    c_first = (group_start[c_exp] + (cidx - chunk_start_e[c_exp]) * MOE_CHUNK).astype(jnp.int32)
    return dict(n_chunks=n_chunks, dest=dest.astype(jnp.int32), c_exp=c_exp, c_cnt=c_cnt, c_blk=c_blk,
                n_used=n_used.reshape(1).astype(jnp.int32), c_first=c_first, order=order,
                zero_token=zero_token)


def _row_copy(src_hbm, row, dst_vmem, slot, sem):
    return pltpu.make_async_copy(src_hbm.at[pl.ds(row, 1), :], dst_vmem.at[pl.ds(slot, 1), :], sem)


def _moe_matmul_kernel(ce_ref, cnt_ref, blk_ref, nu_ref, *refs, gated, n_chunks, nj, zero_token):
    c = pl.program_id(0)
    j = pl.program_id(1)
    cnt = cnt_ref[c]
    if gated:
        (first_ref, order_ref, h_hbm, wg_ref, bg_ref, wu_ref, bu_ref, o_ref,
         wgb_ref, wub_ref, xbuf, stage, sem) = refs
        slot = c % 2
        piece = stage.shape[0]
        n_elem = order_ref.shape[0]

        def read_rows(rows):
            return xbuf[slot, rows, :]

        def piece_rows(chunk_cnt, lo):
            live = (chunk_cnt + MOE_TAIL - 1) // MOE_TAIL * MOE_TAIL
            return jnp.clip(live - lo, 0, piece)

        def start_rows(chunk, chunk_cnt, lo, n):
            first = first_ref[chunk]

            def issue(g, carry):
                for u in range(BF16_ROWS):
                    r = g * BF16_ROWS + u
                    elem = order_ref[jnp.minimum(first + lo + r, n_elem - 1)]
                    token = jnp.where(lo + r < chunk_cnt, elem // TOP_K, zero_token)
                    _row_copy(h_hbm, token, stage, r, sem).start()
                return carry

            lax.fori_loop(0, n // BF16_ROWS, issue, 0)

        def finish_rows(dst_slot, lo, n):
            def drain(g, carry):
                for u in range(BF16_ROWS):
                    _row_copy(h_hbm, 0, stage, g * BF16_ROWS + u, sem).wait()
                return carry

            def cast(g, carry):
                src = pl.ds(pl.multiple_of(g * BF16_ROWS, BF16_ROWS), BF16_ROWS)
                dst = pl.ds(pl.multiple_of(lo + g * BF16_ROWS, BF16_ROWS), BF16_ROWS)
                xbuf[dst_slot, dst, :] = stage[src, :].astype(BF16)
                return carry

            lax.fori_loop(0, n // BF16_ROWS, drain, 0)
            lax.fori_loop(0, n // BF16_ROWS, cast, 0)

        @pl.when(jnp.logical_and(c == 0, j == 0))
        def _():
            def first_chunk(p, carry):
                n = piece_rows(cnt_ref[0], p * piece)
                start_rows(0, cnt_ref[0], p * piece, n)
                finish_rows(0, p * piece, n)
                return carry

            lax.fori_loop(0, nj, first_chunk, 0)

        nxt = jnp.minimum(c + 1, n_chunks - 1)
        nxt_cnt = jnp.where(c + 1 < n_chunks, cnt_ref[nxt], 0)
        ahead = piece_rows(nxt_cnt, j * piece)
        start_rows(nxt, nxt_cnt, j * piece, ahead)
    else:
        x_ref, wg_ref, bg_ref, o_ref, wgb_ref = refs

        def read_rows(rows):
            return x_ref[rows, :]

    def cast_weights():
        wgb_ref[...] = wg_ref[0].astype(BF16)
        if gated:
            wub_ref[...] = wu_ref[0].astype(BF16)

    def rows_block(start, size):
        rows = pl.ds(pl.multiple_of(start, MOE_TAIL), size)
        xs = read_rows(rows)
        y = jnp.dot(xs, wgb_ref[...], preferred_element_type=F32) + bg_ref[0]
        if gated:
            up = jnp.dot(xs, wub_ref[...], preferred_element_type=F32) + bu_ref[0]
            gate = jnp.minimum(y, SWIGLU_LIMIT)
            up = jnp.clip(up, -SWIGLU_LIMIT, SWIGLU_LIMIT)
            y = (up + 1.0) * (gate * jax.nn.sigmoid(SWIGLU_ALPHA * gate))
        o_ref[rows, :] = y.astype(o_ref.dtype)

    units = (cnt + MOE_TAIL - 1) // MOE_TAIL
    top = MOE_BLOCKS[0]
    assert MOE_CHUNK < 2 * top
    has_top = units >= top // MOE_TAIL

    @pl.when(has_top)
    def _():
        cast_weights()
        rows_block(0, top)

    @pl.when(jnp.logical_and(jnp.logical_not(has_top), cnt > 0))
    def _():
        cast_weights()

    start = jnp.where(has_top, top, 0)
    units = units - start // MOE_TAIL
    for size in MOE_BLOCKS[1:]:
        take = units >= size // MOE_TAIL

        @pl.when(take)
        def _(start=start, size=size):
            rows_block(start, size)

        start = start + jnp.where(take, size, 0)
        units = units - jnp.where(take, size // MOE_TAIL, 0)

    if gated:
        finish_rows(1 - slot, j * piece, ahead)


def _moe_matmul(x, plan, weights, biases, out_dtype):
    gated = len(weights) == 2
    n_exp, k, n = weights[0].shape
    n_chunks = plan["n_chunks"]
    tn = MOE_TN if gated else MOE_DOWN_TN
    nj = n // tn
    n_prefetch = 6 if gated else 4

    def col(c, j, nu):
        return jnp.where(c < nu[0], j, nj - 1)

    def w_map(c, j, ce, cnt, blk, nu, *_):
        return (ce[c], 0, col(c, j, nu))

    def o_map(c, j, ce, cnt, blk, nu, *_):
        return (blk[c], col(c, j, nu))

    w_spec = pl.BlockSpec((1, k, tn), w_map)
    b_spec = pl.BlockSpec((1, 1, tn), w_map)
    scratch = [pltpu.VMEM((k, tn), BF16)] * len(weights)
    if gated:
        assert MOE_CHUNK % nj == 0 and (MOE_CHUNK // nj) % BF16_ROWS == 0
        prefetch = [plan["c_first"], plan["order"]]
        operands, in_specs = [x], [pl.BlockSpec(memory_space=pl.ANY)]
        scratch += [pltpu.VMEM((2, MOE_CHUNK, k), BF16), pltpu.VMEM((MOE_CHUNK // nj, k), F32),
                    pltpu.SemaphoreType.DMA]
    else:
        prefetch = []
        operands = [x]
        in_specs = [pl.BlockSpec((MOE_CHUNK, k), lambda c, j, ce, cnt, blk, nu: (blk[c], 0))]
    for w, b in zip(weights, biases):
        operands += [w, b.reshape(n_exp, 1, n)]
        in_specs += [w_spec, b_spec]
    grid_spec = pltpu.PrefetchScalarGridSpec(
        num_scalar_prefetch=n_prefetch, grid=(n_chunks, nj), in_specs=in_specs,
        out_specs=pl.BlockSpec((MOE_CHUNK, tn), o_map), scratch_shapes=scratch)
    return pl.pallas_call(
        functools.partial(_moe_matmul_kernel, gated=gated, n_chunks=n_chunks, nj=nj,
                          zero_token=plan["zero_token"]),
        out_shape=jax.ShapeDtypeStruct((n_chunks * MOE_CHUNK, n), out_dtype), grid_spec=grid_spec,
        compiler_params=_cparams(2), name="moe_up" if gated else "moe_down",
    )(plan["c_exp"], plan["c_cnt"], plan["c_blk"], plan["n_used"], *prefetch, *operands)


def _moe_combine_kernel(dest_ref, y_hbm, x_ref, g_ref, o_ref, buf, sem, *, tb):
    base = pl.program_id(0) * tb * TOP_K

    def issue(r, carry):
        for k in range(TOP_K):
            _row_copy(y_hbm, dest_ref[base + r * TOP_K + k], buf.at[k], r, sem).start(priority=k % DMA_THREADS)
        return carry

    def drain(r, carry):
        for k in range(TOP_K):
            _row_copy(y_hbm, 0, buf.at[k], r, sem).wait()
        return carry

    lax.fori_loop(0, tb, issue, 0, unroll=4)
    lax.fori_loop(0, tb, drain, 0, unroll=4)
    gates = g_ref[...]
    moe = gates[:, 0:1] * buf[0]
    for k in range(1, TOP_K):
        moe = moe + gates[:, k:k + 1] * buf[k]
    o_ref[...] = x_ref[...] + moe


def _moe_combine(y_rows, dest, x, gates, tb):
    t, d = x.shape
    grid_spec = pltpu.PrefetchScalarGridSpec(
        num_scalar_prefetch=1, grid=(t // tb,),
        in_specs=[pl.BlockSpec(memory_space=pl.ANY),
                  pl.BlockSpec((tb, d), lambda i, dest: (i, 0)),
                  pl.BlockSpec((tb, TOP_K), lambda i, dest: (i, 0))],
        out_specs=pl.BlockSpec((tb, d), lambda i, dest: (i, 0)),
        scratch_shapes=[pltpu.VMEM((TOP_K, tb, d), F32), pltpu.SemaphoreType.DMA])
    return pl.pallas_call(
        functools.partial(_moe_combine_kernel, tb=tb), out_shape=jax.ShapeDtypeStruct((t, d), F32),
        grid_spec=grid_spec, compiler_params=_cparams(1), name="moe_combine",
    )(dest, y_rows, x, gates)


def _pick_tile(n, candidates):
    for c in candidates:
        if n % c == 0:
            return c
    return n


def kernel(x_prompt, x_sample, cache_sb_k, cache_sb_v, cache_diff_k, cache_diff_v, page_table, attn_norm, w_in, diff_q_norm, diff_k_norm, diff_lambda_q1, diff_lambda_k1, diff_lambda_q2, diff_lambda_k2, rel_bias, sb_out_norm, diff_subln, w_out, ffn_norm, w_router, b_router, w_gate, b_gate, w_up, b_up, w_down, b_down):
    depth = attn_norm.shape[0]
    assert depth == 1, "single-layer trunk"
    bsz, seq, d = x_prompt.shape
    db, dseq, _ = x_sample.shape
    assert dseq == 1
    n_exp = w_router.shape[2]
    half = d // 2
    sb_heads = half // HEAD_DIM
    sb_kv = sb_heads // GROUP
    diff_heads = half // (2 * HEAD_DIM)
    diff_kv = diff_heads // GROUP
    sbq, sbk = sb_heads * HEAD_DIM, sb_kv * HEAD_DIM
    dq, dk = diff_heads * 2 * HEAD_DIM, diff_kv * 2 * HEAD_DIM
    off_sbk, off_sbv = sbq, sbq + sbk
    off_dq = sbq + 2 * sbk
    off_dk = off_dq + dq
    off_dv = off_dk + dk
    in_cols = off_dv + dk
    layer = 0
    lambda_init = 0.8 - 0.6 * math.exp(-0.3 * layer)
    lam = (jnp.exp(jnp.sum(diff_lambda_q1[layer] * diff_lambda_k1[layer]))
           - jnp.exp(jnp.sum(diff_lambda_q2[layer] * diff_lambda_k2[layer])) + lambda_init).astype(F32)

    tn_in = 512
    assert off_dq % tn_in == 0 and off_dv % tn_in == 0
    qk_gain = jnp.concatenate([jnp.ones((off_dq,), F32), jnp.tile(diff_q_norm[layer], dq // HEAD_DIM),
                               jnp.tile(diff_k_norm[layer], dk // HEAD_DIM), jnp.ones((dk,), F32)]).reshape(1, in_cols)
    norm_tiles = dict(norm_lo=off_dq // tn_in, norm_hi=off_dv // tn_in)
    n_tok = bsz * seq

    xp = x_prompt.reshape(n_tok, d)
    h_p = _rmsnorm(xp, attn_norm[layer], BF16, _pick_tile(n_tok, (256, 128, 8)))
    tm_p = _pick_tile(n_tok, (1024, 512, 256, 128, 8))
    proj_p = _matmul((h_p,), w_in[layer], qk_gain, mode="qknorm", tm=tm_p, tn=tn_in, **norm_tiles)
    proj3 = proj_p.reshape(bsz, seq, in_cols)
    mix_sb = _sb_prompt(proj3, sb_out_norm[layer].reshape(-1), kv_heads=sb_kv,
                        q_off=0, k_off=off_sbk, v_off=off_sbv)
    mix_d = _diff_prompt(proj3, rel_bias, lam, diff_subln[layer], kv_heads=diff_kv,
                         q_off=off_dq, k_off=off_dk, v_off=off_dv, out_scale=1.0 - lambda_init)
    x2_p = _matmul((mix_sb.reshape(n_tok, sbq), mix_d.reshape(n_tok, dq)), w_out[layer], xp,
                   mode="residual", tm=tm_p, tn=512)

    xs = x_sample.reshape(db, d)
    h_s = _rmsnorm(xs, attn_norm[layer], BF16, db)
    proj_s = _matmul((h_s,), w_in[layer], qk_gain, mode="qknorm", tm=db, tn=tn_in, **norm_tiles)
    pool = cache_sb_k.shape[1]
    page = cache_sb_k.shape[2]
    assert page == LANES
    rows4 = (pool, page * (sbk // HEAD_DIM), HEAD_DIM)
    assert sbk == dk == 4 * HEAD_DIM
    sb_o = _sb_decode(proj_s[:, :sbq].reshape(db, sb_heads, HEAD_DIM),
                      cache_sb_k.reshape(rows4), cache_sb_v.reshape(rows4),
                      page_table, sb_out_norm[layer], kv_heads=sb_kv)
    diff_v_rows = cache_diff_v.reshape(pool, page, diff_kv, 2, HEAD_DIM).transpose(0, 1, 3, 2, 4).reshape(rows4)
    q_d = proj_s[:, off_dq:off_dk].reshape(db, diff_heads, 2, HEAD_DIM).transpose(0, 2, 1, 3)
    rbt = jnp.concatenate([rel_bias.T, rel_bias.T], axis=0).astype(F32)
    d_o = _diff_decode(q_d.reshape(db, 2 * diff_heads, HEAD_DIM),
                       proj_s[:, off_dk:off_dv].reshape(db, 1, dk), proj_s[:, off_dv:].reshape(db, 1, dk),
                       cache_diff_k.reshape(rows4), diff_v_rows,
                       page_table, rbt, lam, diff_subln[layer], kv_heads=diff_kv, out_scale=1.0 - lambda_init)
    x2_s = _matmul((sb_o.reshape(db, sbq).astype(BF16), d_o.reshape(db, dq).astype(BF16)), w_out[layer], xs,
                   mode="residual", tm=db, tn=512)

    h_table, idx_p, gate_p = _router(x2_p, ffn_norm[layer], w_router[layer], b_router[layer],
                                     _pick_tile(n_tok, (256, 128)), table_rows=n_tok + 2 * db)
    h_all, idx_s, gate_s = _router(x2_s, ffn_norm[layer], w_router[layer], b_router[layer], db,
                                   table=h_table, row_offset=n_tok)
    top_idx = jnp.concatenate([idx_p, idx_s], axis=1).T
    plan = _moe_plan(top_idx, n_exp, zero_token=n_tok + db)
    act = _moe_matmul(h_all, plan, (w_gate[layer], w_up[layer]), (b_gate[layer], b_up[layer]), BF16)
    y_rows = _moe_matmul(act, plan, (w_down[layer],), (b_down[layer],), F32)
    dest = plan["dest"]
    y_p = _moe_combine(y_rows, dest[:n_tok * TOP_K], x2_p, gate_p.T, _pick_tile(n_tok, (COMBINE_TOKENS, 8)))
    y_s = _moe_combine(y_rows, dest[n_tok * TOP_K:], x2_s, gate_s.T, db)

    def rows(p, lead, lo, hi, shape):
        return p[:, lo:hi].reshape((depth,) + lead + shape)

    lead_p, lead_s = (bsz, seq), (db, dseq)
    return (y_p.reshape(bsz, seq, d), y_s.reshape(db, dseq, d),
            rows(proj_p, lead_p, off_sbk, off_sbv, (sb_kv, HEAD_DIM)),
            rows(proj_p, lead_p, off_sbv, off_dq, (sb_kv, HEAD_DIM)),
            rows(proj_p, lead_p, off_dk, off_dv, (diff_kv, 2, HEAD_DIM)),
            rows(proj_p, lead_p, off_dv, in_cols, (diff_kv, 2 * HEAD_DIM)),
            rows(proj_s, lead_s, off_sbk, off_sbv, (sb_kv, HEAD_DIM)),
            rows(proj_s, lead_s, off_sbv, off_dq, (sb_kv, HEAD_DIM)),
            rows(proj_s, lead_s, off_dk, off_dv, (diff_kv, 2, HEAD_DIM)),
            rows(proj_s, lead_s, off_dv, in_cols, (diff_kv, 2 * HEAD_DIM)))
```

```python
import functools
import math

import numpy as np
import jax
import jax.numpy as jnp
from jax import lax
from jax.experimental import pallas as pl
from jax.experimental.pallas import tpu as pltpu

F32 = jnp.float32
BF16 = jnp.bfloat16

HEAD_DIM = 128
GROUP = 4
N_BUCKETS = 32
MAX_EXACT = N_BUCKETS // 2
MAX_DISTANCE = 128
TOP_K = 4
SWIGLU_LIMIT = 7.0
SWIGLU_ALPHA = 1.702
EPS = 1e-5
NEG_BIG = -1e30

V7X_VMEM_LIMIT_BYTES = 56 * 1024 * 1024
LANES = 128
ATTN_BLOCK = 128
PAGES_PER_STEP = 16
DIFF_PAGES_PER_STEP = 32
MOE_TAIL = 128
MOE_BLOCKS = (1024, 512, 256, 128)
MOE_CHUNK = 1280
BF16_ROWS = 16
MOE_TN = 256
MOE_DOWN_TN = 512
COMBINE_TOKENS = 128
DMA_THREADS = 2


def _cparams(n_axes):
    return pltpu.CompilerParams(dimension_semantics=("arbitrary",) * n_axes,
                                vmem_limit_bytes=V7X_VMEM_LIMIT_BYTES)


def _nt_dot(a, b, precision=None):
    return lax.dot_general(a, b, (((1,), (1,)), ((), ())), precision=precision,
                           preferred_element_type=F32)


def _rms(x, gain):
    ms = jnp.mean(x * x, axis=-1, keepdims=True)
    return x * lax.rsqrt(ms + EPS) * gain


def _bucket_thresholds():
    n = np.arange(MAX_DISTANCE + 1)
    nf = np.maximum(n, MAX_EXACT).astype(np.float32)
    large = MAX_EXACT + (np.log(nf / np.float32(MAX_EXACT)) / np.float32(math.log(MAX_DISTANCE / MAX_EXACT))
                         * np.float32(N_BUCKETS - MAX_EXACT)).astype(np.int32)
    large = np.minimum(large, N_BUCKETS - 1)
    bucket = np.where(n < MAX_EXACT, n, large)
    return [int(np.argmax(bucket >= k)) for k in range(1, N_BUCKETS)]


_BUCKET_THR = _bucket_thresholds()


def _rmsnorm_kernel(x_ref, g_ref, o_ref):
    o_ref[...] = _rms(x_ref[...], g_ref[...]).astype(o_ref.dtype)


def _rmsnorm(x, gain, out_dtype, tm):
    t, d = x.shape
    return pl.pallas_call(
        _rmsnorm_kernel,
        out_shape=jax.ShapeDtypeStruct((t, d), out_dtype),
        grid=(t // tm,),
        in_specs=[pl.BlockSpec((tm, d), lambda i: (i, 0)), pl.BlockSpec((1, d), lambda i: (0, 0))],
        out_specs=pl.BlockSpec((tm, d), lambda i: (i, 0)),
        compiler_params=_cparams(1), name="rmsnorm",
    )(x, gain.reshape(1, d))


def _matmul_kernel(*refs, n_parts, mode, norm_lo, norm_hi, tn):
    a_refs = refs[:n_parts]
    w_ref, e_ref, o_ref, wb_ref = refs[n_parts:]
    j = pl.program_id(0)
    i = pl.program_id(1)

    @pl.when(i == 0)
    def _():
        wb_ref[...] = w_ref[...].astype(BF16)

    acc, k_lo = None, 0
    for a_ref in a_refs:
        k_hi = k_lo + a_ref.shape[1]
        part = jnp.dot(a_ref[...], wb_ref[k_lo:k_hi, :], preferred_element_type=F32)
        acc = part if acc is None else acc + part
        k_lo = k_hi
    if mode == "residual":
        o_ref[...] = e_ref[...] + acc
    else:
        in_range = jnp.logical_and(j >= norm_lo, j < norm_hi)

        @pl.when(in_range)
        def _():
            for c in range(tn // LANES):
                sl = slice(c * LANES, (c + 1) * LANES)
                o_ref[:, sl] = _rms(acc[:, sl], e_ref[:, sl])

        @pl.when(jnp.logical_not(in_range))
        def _():
            o_ref[...] = acc


def _matmul(a_parts, w, extra, *, mode, tm, tn, norm_lo=0, norm_hi=0):
    m = a_parts[0].shape[0]
    k, n = w.shape
    assert sum(a.shape[1] for a in a_parts) == k
    if mode == "residual":
        e_spec = pl.BlockSpec((tm, tn), lambda j, i: (i, j))
    else:
        e_spec = pl.BlockSpec((1, tn), lambda j, i: (0, j))
    kern = functools.partial(_matmul_kernel, n_parts=len(a_parts), mode=mode, norm_lo=norm_lo, norm_hi=norm_hi,
                             tn=tn)
    return pl.pallas_call(
        kern,
        out_shape=jax.ShapeDtypeStruct((m, n), F32),
        grid=(n // tn, m // tm),
        in_specs=[pl.BlockSpec((tm, a.shape[1]), lambda j, i: (i, 0)) for a in a_parts]
                 + [pl.BlockSpec((k, tn), lambda j, i: (0, j)), e_spec],
        out_specs=pl.BlockSpec((tm, tn), lambda j, i: (i, j)),
        scratch_shapes=[pltpu.VMEM((k, tn), BF16)],
        compiler_params=_cparams(2), name="matmul_" + mode,
    )(*a_parts, w, extra)


def _softplus(z):
    return jnp.maximum(z, 0.0) + jnp.log(1.0 + jnp.exp(-jnp.abs(z)))


def _sb_prompt_kernel(q_ref, k_ref, v_ref, g_ref, o_ref, acc_ref, *, bq, scale):
    qi = pl.program_id(2)
    rows = GROUP * bq
    q = jnp.concatenate([q_ref[0, :, g * HEAD_DIM:(g + 1) * HEAD_DIM] for g in range(GROUP)], axis=0)
    qs = (q * scale).astype(BF16)
    row_i = lax.broadcasted_iota(jnp.int32, (bq, bq), 0)
    col_i = lax.broadcasted_iota(jnp.int32, (bq, bq), 1)
    later = jnp.where(row_i > col_i, 1.0, 0.0).astype(BF16)
    visible = jnp.concatenate([col_i < row_i] * GROUP, axis=0)

    later2 = jnp.concatenate([later, later], axis=0)

    def weights(z, carry, masked):
        sp = _softplus(z)
        log_keep = jnp.where(visible, -sp, 0.0) if masked else -sp
        hi, lo = _split_bf16(log_keep)
        between = jnp.dot(jnp.concatenate([hi, lo], axis=1), later2, preferred_element_type=F32) + carry
        w = jnp.exp(z - sp + between)
        if masked:
            w = jnp.where(visible, w, 0.0)
        return w.astype(BF16), carry + jnp.sum(log_keep, axis=-1, keepdims=True)

    def single(kb, carry, masked):
        start = pl.multiple_of(kb * bq, bq)
        k = k_ref[0, pl.ds(start, bq), :].astype(BF16)
        v = v_ref[0, pl.ds(start, bq), :].astype(BF16)
        w, carry = weights(_nt_dot(qs, k), carry, masked)
        acc_ref[...] += jnp.dot(w, v, preferred_element_type=F32)
        return carry

    n_pairs = qi // 2

    def pair(t, carry):
        start = pl.multiple_of((2 * (n_pairs - 1 - t)) * bq, bq)
        k = k_ref[0, pl.ds(start, 2 * bq), :].astype(BF16)
        v = v_ref[0, pl.ds(start, 2 * bq), :].astype(BF16)
        z = _nt_dot(qs, k)
        w_late, carry = weights(z[:, bq:], carry, False)
        w_early, carry = weights(z[:, :bq], carry, False)
        acc_ref[...] += jnp.dot(jnp.concatenate([w_early, w_late], axis=1), v, preferred_element_type=F32)
        return carry

    acc_ref[...] = jnp.zeros_like(acc_ref)
    carry = single(qi, jnp.zeros((rows, 1), F32), True)
    carry = lax.cond(qi % 2 == 1, lambda c: single(qi - 1, c, False), lambda c: c, carry)
    lax.fori_loop(0, n_pairs, pair, carry)

    acc = acc_ref[...]
    for g in range(GROUP):
        sl = slice(g * HEAD_DIM, (g + 1) * HEAD_DIM)
        o_ref[0, :, sl] = _rms(acc[g * bq:(g + 1) * bq, :], g_ref[:, sl]).astype(o_ref.dtype)


def _sb_prompt(proj3, gain, *, kv_heads, q_off, k_off, v_off):
    b, t, _ = proj3.shape
    bq = ATTN_BLOCK
    qw = GROUP * HEAD_DIM
    width = kv_heads * qw
    assert q_off % qw == 0 and k_off % HEAD_DIM == 0 and v_off % HEAD_DIM == 0 and t % bq == 0
    kern = functools.partial(_sb_prompt_kernel, bq=bq, scale=HEAD_DIM ** -0.5)
    return pl.pallas_call(
        kern,
        out_shape=jax.ShapeDtypeStruct((b, t, width), BF16),
        grid=(b, kv_heads, t // bq),
        in_specs=[pl.BlockSpec((1, bq, qw), lambda bi, h, qi: (bi, qi, q_off // qw + h)),
                  pl.BlockSpec((1, t, HEAD_DIM), lambda bi, h, qi: (bi, 0, k_off // HEAD_DIM + h)),
                  pl.BlockSpec((1, t, HEAD_DIM), lambda bi, h, qi: (bi, 0, v_off // HEAD_DIM + h)),
                  pl.BlockSpec((1, qw), lambda bi, h, qi: (0, h))],
        out_specs=pl.BlockSpec((1, bq, qw), lambda bi, h, qi: (bi, qi, h)),
        scratch_shapes=[pltpu.VMEM((GROUP * bq, HEAD_DIM), F32)],
        compiler_params=_cparams(3), name="sb_prompt",
    )(proj3, proj3, proj3, gain.reshape(1, width))


def _bias_from_distance(n, table):
    bias = table(0)
    for k in range(1, N_BUCKETS):
        bias = jnp.where(n >= _BUCKET_THR[k - 1], table(k), bias)
    return bias


def _diff_prompt_kernel(rb_ref, lam_ref, q_ref, k_ref, v_ref, sub_ref, o_ref,
                        tiles_ref, s_ref, m_ref, l_ref, acc_ref, *, bq, scale, kv_heads, out_scale):
    bi = pl.program_id(0)
    h = pl.program_id(1)
    qi = pl.program_id(2)
    rows = GROUP * bq
    vw = 2 * HEAD_DIM
    row_i = lax.broadcasted_iota(jnp.int32, (bq, bq), 0)
    col_i = lax.broadcasted_iota(jnp.int32, (bq, bq), 1)

    @pl.when(jnp.logical_and(jnp.logical_and(bi == 0, h == 0), qi == 0))
    def _():
        for kind in range(2):
            n = jnp.maximum(row_i - col_i + kind * bq, 0)
            for hh in range(kv_heads):
                for g in range(GROUP):
                    head = hh * GROUP + g
                    tiles_ref[kind, hh, g * bq:(g + 1) * bq, :] = _bias_from_distance(
                        n, lambda k, head=head: rb_ref[k, head])

    far_bias = jnp.concatenate(
        [jnp.full((bq, 1), rb_ref[N_BUCKETS - 1, h * GROUP + g], F32) for g in range(GROUP)], axis=0)
    causal = jnp.concatenate([col_i <= row_i] * GROUP, axis=0)
    qs = []
    for c in range(2):
        qc = jnp.concatenate([q_ref[0, :, (2 * g + c) * HEAD_DIM:(2 * g + c + 1) * HEAD_DIM]
                              for g in range(GROUP)], axis=0)
        qs.append((qc * scale).astype(BF16))

    def logits(c, start, width):
        k = k_ref[0, pl.ds(pl.multiple_of(start, bq), width), c * HEAD_DIM:(c + 1) * HEAD_DIM].astype(BF16)
        return _nt_dot(qs[c], k)

    n_far = jnp.maximum(qi - 1, 0)
    n_far_pairs = n_far // 2
    m_ref[...] = jnp.full_like(m_ref, NEG_BIG)

    def far_pair(j, carry):
        for c in range(2):
            s = logits(c, j * 2 * bq, 2 * bq)
            m_ref[c] = jnp.maximum(m_ref[c], jnp.maximum(s[:, :bq], s[:, bq:]))
            s_ref[c, 2 * j] = s[:, :bq] + far_bias
            s_ref[c, 2 * j + 1] = s[:, bq:] + far_bias
        return carry

    lax.fori_loop(0, n_far_pairs, far_pair, 0)

    @pl.when(n_far % 2 == 1)
    def _():
        for c in range(2):
            s = logits(c, (n_far - 1) * bq, bq)
            m_ref[c] = jnp.maximum(m_ref[c], s)
            s_ref[c, n_far - 1] = s + far_bias

    for c in range(2):
        m_ref[c] = m_ref[c] + far_bias

    @pl.when(qi >= 1)
    def _():
        for c in range(2):
            s = logits(c, (qi - 1) * bq, bq) + tiles_ref[1, h]
            m_ref[c] = jnp.maximum(m_ref[c], s)
            s_ref[c, qi - 1] = s

    for c in range(2):
        s = jnp.where(causal, logits(c, qi * bq, bq) + tiles_ref[0, h], NEG_BIG)
        s_ref[c, qi] = s
        m_ref[c] = jnp.broadcast_to(jnp.max(jnp.maximum(m_ref[c], s), axis=-1, keepdims=True), (rows, bq))

    @pl.when(qi % 2 == 0)
    def _():
        for c in range(2):
            s_ref[c, qi + 1] = jnp.full((rows, bq), NEG_BIG, F32)

    l_ref[...] = jnp.zeros_like(l_ref)
    acc_ref[...] = jnp.zeros_like(acc_ref)

    def accumulate(j, carry):
        v = v_ref[0, pl.ds(pl.multiple_of(j * 2 * bq, 2 * bq), 2 * bq), :].astype(BF16)
        for c in range(2):
            p0 = jnp.exp(s_ref[c, 2 * j] - m_ref[c])
            p1 = jnp.exp(s_ref[c, 2 * j + 1] - m_ref[c])
            l_ref[c] += p0 + p1
            acc_ref[c] += jnp.dot(jnp.concatenate([p0, p1], axis=1).astype(BF16), v, preferred_element_type=F32)
        return carry

    lax.fori_loop(0, qi // 2 + 1, accumulate, 0)

    lam = lam_ref[0]
    l0 = jnp.sum(l_ref[0], axis=-1, keepdims=True)
    l1 = jnp.sum(l_ref[1], axis=-1, keepdims=True)
    o = acc_ref[0] / l0 - lam * (acc_ref[1] / l1)
    for g in range(GROUP):
        o_ref[0, :, g * vw:(g + 1) * vw] = (_rms(o[g * bq:(g + 1) * bq, :], sub_ref[...]) * out_scale
                                            ).astype(o_ref.dtype)


def _diff_prompt(proj3, rel_bias, lam, subln, *, kv_heads, q_off, k_off, v_off, out_scale):
    b, t, _ = proj3.shape
    bq = ATTN_BLOCK
    qw = GROUP * 2 * HEAD_DIM
    vw = 2 * HEAD_DIM
    width = kv_heads * GROUP * vw
    assert q_off % qw == 0 and k_off % vw == 0 and v_off % vw == 0 and t % (2 * bq) == 0
    assert bq + 1 >= _BUCKET_THR[-1]
    kern = functools.partial(_diff_prompt_kernel, bq=bq, scale=HEAD_DIM ** -0.5, kv_heads=kv_heads,
                             out_scale=out_scale)
    smem = pl.BlockSpec(memory_space=pltpu.SMEM)
    return pl.pallas_call(
        kern,
        out_shape=jax.ShapeDtypeStruct((b, t, width), BF16),
        grid=(b, kv_heads, t // bq),
        in_specs=[smem, smem,
                  pl.BlockSpec((1, bq, qw), lambda bi, h, qi: (bi, qi, q_off // qw + h)),
                  pl.BlockSpec((1, t, vw), lambda bi, h, qi: (bi, 0, k_off // vw + h)),
                  pl.BlockSpec((1, t, vw), lambda bi, h, qi: (bi, 0, v_off // vw + h)),
                  pl.BlockSpec((1, vw), lambda bi, h, qi: (0, 0))],
        out_specs=pl.BlockSpec((1, bq, GROUP * vw), lambda bi, h, qi: (bi, qi, h)),
        scratch_shapes=[pltpu.VMEM((2, kv_heads, GROUP * bq, bq), F32),
                        pltpu.VMEM((2, t // bq, GROUP * bq, bq), F32),
                        pltpu.VMEM((2, GROUP * bq, bq), F32),
                        pltpu.VMEM((2, GROUP * bq, bq), F32),
                        pltpu.VMEM((2, GROUP * bq, vw), F32)],
        compiler_params=_cparams(3), name="diff_prompt",
    )(rel_bias, lam.reshape(1), proj3, proj3, proj3, subln.reshape(1, vw))


def _block_diag(q, n_blocks, block_of_row):
    r = q.shape[0]
    row = lax.broadcasted_iota(jnp.int32, (r, HEAD_DIM), 0)
    blk = block_of_row(row)
    return jnp.concatenate([jnp.where(blk == hb, q, 0.0) for hb in range(n_blocks)], axis=1)


def _split_bf16(x):
    hi = x.astype(BF16)
    lo = (x - hi.astype(F32)).astype(BF16)
    return hi, lo


def _page_lanes(ref, subs):
    n = ref.shape[1] // LANES
    return jnp.concatenate([ref[0, pl.ds(sub, LANES, stride=n), :] for sub in subs], axis=1).astype(BF16)


def _page_specs(n_pages, pages_per_step, width, step_of):
    specs = []
    for n in range(pages_per_step):
        def imap(b, s, pt, n=n):
            return (pt[b * n_pages + step_of(s) * pages_per_step + n], 0, 0)
        specs.append(pl.BlockSpec((1, width, LANES), imap))
    return specs


def _sb_decode_kernel(pt_ref, q_ref, *refs, kv_heads, scale):
    k_refs = refs[:PAGES_PER_STEP]
    v_refs = refs[PAGES_PER_STEP:2 * PAGES_PER_STEP]
    g_ref, o_ref, acc_ref, carry_ref = refs[2 * PAGES_PER_STEP:]
    s = pl.program_id(1)
    heads = kv_heads * GROUP
    width = PAGES_PER_STEP * LANES

    @pl.when(s == 0)
    def _():
        acc_ref[...] = jnp.zeros_like(acc_ref)
        carry_ref[...] = jnp.zeros_like(carry_ref)

    q_bd = _block_diag(q_ref[0], kv_heads, block_of_row=lambda r: r // GROUP).astype(BF16)
    subs = tuple(range(kv_heads))
    z = jnp.concatenate([_nt_dot(q_bd, _page_lanes(k_refs[p], subs)) for p in range(PAGES_PER_STEP)],
                        axis=1) * scale
    sp = _softplus(z)
    log_keep = -sp
    lane = lax.broadcasted_iota(jnp.int32, (heads, width), 1) % LANES
    suffix = log_keep
    sh = 1
    while sh < LANES:
        suffix = suffix + jnp.where(lane + sh < LANES, pltpu.roll(suffix, width - sh, 1), 0.0)
        sh *= 2
    run = carry_ref[...]
    offsets = [None] * PAGES_PER_STEP
    for p in reversed(range(PAGES_PER_STEP)):
        offsets[p] = run
        run = run + jnp.broadcast_to(suffix[:, p * LANES:p * LANES + 1], (heads, LANES))
    carry_ref[...] = run
    between = suffix - log_keep + jnp.concatenate(offsets, axis=1)
    w = jnp.exp(z - sp + between)
    o = acc_ref[...]
    for p in range(PAGES_PER_STEP):
        o = o + jnp.dot(w[:, p * LANES:(p + 1) * LANES].astype(BF16), _page_lanes(v_refs[p], subs),
                        preferred_element_type=F32)
    acc_ref[...] = o

    @pl.when(s == pl.num_programs(1) - 1)
    def _():
        row = lax.broadcasted_iota(jnp.int32, (heads, HEAD_DIM), 0)
        out = jnp.zeros((heads, HEAD_DIM), F32)
        for hb in range(kv_heads):
            out = out + jnp.where(row // GROUP == hb, o[:, hb * HEAD_DIM:(hb + 1) * HEAD_DIM], 0.0)
        o_ref[0] = _rms(out, g_ref[...])


def _sb_decode(q, cache_k, cache_v, page_table, gain, *, kv_heads):
    db, heads, _ = q.shape
    n_pages = page_table.shape[1]
    assert n_pages % PAGES_PER_STEP == 0
    n_steps = n_pages // PAGES_PER_STEP
    width = kv_heads * HEAD_DIM
    kern = functools.partial(_sb_decode_kernel, kv_heads=kv_heads, scale=HEAD_DIM ** -0.5)
    pages = _page_specs(n_pages, PAGES_PER_STEP, width, lambda s: n_steps - 1 - s)
    grid_spec = pltpu.PrefetchScalarGridSpec(
        num_scalar_prefetch=1, grid=(db, n_steps),
        in_specs=[pl.BlockSpec((1, heads, HEAD_DIM), lambda b, s, pt: (b, 0, 0))] + pages + pages
                 + [pl.BlockSpec((heads, HEAD_DIM), lambda b, s, pt: (0, 0))],
        out_specs=pl.BlockSpec((1, heads, HEAD_DIM), lambda b, s, pt: (b, 0, 0)),
        scratch_shapes=[pltpu.VMEM((heads, width), F32), pltpu.VMEM((heads, LANES), F32)])
    return pl.pallas_call(
        kern, out_shape=jax.ShapeDtypeStruct((db, heads, HEAD_DIM), F32), grid_spec=grid_spec,
        compiler_params=_cparams(2), name="sb_decode",
    )(page_table.reshape(-1), q, *([cache_k] * PAGES_PER_STEP), *([cache_v] * PAGES_PER_STEP), gain)


def _diff_decode_kernel(pt_ref, lam_ref, q_ref, kn_ref, vn_ref, rbt_ref, *refs,
                        kv_heads, scale, out_scale, n_steps):
    k_refs = refs[:DIFF_PAGES_PER_STEP]
    v_refs = refs[DIFF_PAGES_PER_STEP:2 * DIFF_PAGES_PER_STEP]
    sub_ref, o_ref, s_ref, m_ref, acc_ref = refs[2 * DIFF_PAGES_PER_STEP:]
    s = pl.program_id(1)
    heads = kv_heads * GROUP
    rows = 2 * heads
    vw = 2 * HEAD_DIM
    lam = lam_ref[0]
    rbt = rbt_ref[...]
    q_bd = _block_diag(q_ref[0], 2 * kv_heads,
                       block_of_row=lambda r: 2 * ((r % heads) // GROUP) + r // heads).astype(BF16)
    k_subs = tuple(range(2 * kv_heads))
    v_subs = tuple(half * kv_heads + hb for hb in range(kv_heads) for half in range(2))

    @pl.when(s == 0)
    def _():
        m_ref[...] = jnp.full_like(m_ref, NEG_BIG)

    @pl.when(s < n_steps)
    def _():
        far = jnp.broadcast_to(rbt[:, N_BUCKETS - 1:N_BUCKETS], (rows, LANES))
        n = LANES - lax.broadcasted_iota(jnp.int32, (rows, LANES), 1)
        near = _bias_from_distance(n, lambda k: jnp.broadcast_to(rbt[:, k:k + 1], (rows, LANES)))
        tiles = []
        for p in range(DIFF_PAGES_PER_STEP):
            bias = jnp.where(s == n_steps - 1, near, far) if p == DIFF_PAGES_PER_STEP - 1 else far
            tiles.append(_nt_dot(q_bd, _page_lanes(k_refs[p], k_subs)) * scale + bias)
        s_ref[s] = jnp.concatenate(tiles, axis=1)
        m_ref[...] = jnp.maximum(m_ref[...], functools.reduce(jnp.maximum, tiles))

    @pl.when(s == n_steps)
    def _():
        k_self = kn_ref[0].astype(BF16).astype(F32)
        s_self = jnp.sum(q_bd.astype(F32) * k_self, axis=-1, keepdims=True) * scale + rbt[:, 0:1]
        m = jnp.maximum(jnp.max(m_ref[...], axis=-1, keepdims=True), s_self)
        e_self = jnp.exp(s_self - m)
        denom = e_self
        for st in range(n_steps):
            e = jnp.exp(s_ref[st] - m)
            s_ref[st] = e
            denom = denom + jnp.sum(e, axis=-1, keepdims=True)
        for st in range(n_steps):
            p = s_ref[st] / denom
            s_ref[st, :heads] = p[:heads] - lam * p[heads:]
        p_self = e_self / denom
        a_self = (p_self[:heads] - lam * p_self[heads:]).astype(BF16).astype(F32)
        acc_ref[...] = a_self * vn_ref[0].astype(BF16).astype(F32)

    @pl.when(s >= n_steps)
    def _():
        attn = s_ref[s - n_steps, :heads].astype(BF16)
        o = acc_ref[...]
        for p in range(DIFF_PAGES_PER_STEP):
            o = o + jnp.dot(attn[:, p * LANES:(p + 1) * LANES], _page_lanes(v_refs[p], v_subs),
                            preferred_element_type=F32)
        acc_ref[...] = o

        @pl.when(s == 2 * n_steps - 1)
        def _():
            row = lax.broadcasted_iota(jnp.int32, (heads, vw), 0)
            out = jnp.zeros((heads, vw), F32)
            for hb in range(kv_heads):
                out = out + jnp.where(row // GROUP == hb, o[:, hb * vw:(hb + 1) * vw], 0.0)
            o_ref[0] = _rms(out, sub_ref[...]) * out_scale


def _diff_decode(q, k_new, v_new, cache_k, cache_v, page_table, rbt, lam, subln, *, kv_heads, out_scale):
    db, rows, _ = q.shape
    heads = rows // 2
    n_pages = page_table.shape[1]
    pps = DIFF_PAGES_PER_STEP
    assert n_pages % pps == 0 and LANES >= MAX_DISTANCE
    n_steps = n_pages // pps
    width = kv_heads * 2 * HEAD_DIM
    vw = 2 * HEAD_DIM
    kern = functools.partial(_diff_decode_kernel, kv_heads=kv_heads, scale=HEAD_DIM ** -0.5, out_scale=out_scale,
                             n_steps=n_steps)
    k_pages = _page_specs(n_pages, pps, width, lambda s: jnp.minimum(s, n_steps - 1))
    v_pages = _page_specs(n_pages, pps, width, lambda s: jnp.maximum(s - n_steps, 0))
    grid_spec = pltpu.PrefetchScalarGridSpec(
        num_scalar_prefetch=1, grid=(db, 2 * n_steps),
        in_specs=[pl.BlockSpec(memory_space=pltpu.SMEM),
                  pl.BlockSpec((1, rows, HEAD_DIM), lambda b, s, pt: (b, 0, 0)),
                  pl.BlockSpec((1, 1, width), lambda b, s, pt: (b, 0, 0)),
                  pl.BlockSpec((1, 1, width), lambda b, s, pt: (b, 0, 0)),
                  pl.BlockSpec((rows, N_BUCKETS), lambda b, s, pt: (0, 0))] + k_pages + v_pages
                 + [pl.BlockSpec((1, vw), lambda b, s, pt: (0, 0))],
        out_specs=pl.BlockSpec((1, heads, vw), lambda b, s, pt: (b, 0, 0)),
        scratch_shapes=[pltpu.VMEM((n_steps, rows, pps * LANES), F32), pltpu.VMEM((rows, LANES), F32),
                        pltpu.VMEM((heads, width), F32)])
    return pl.pallas_call(
        kern, out_shape=jax.ShapeDtypeStruct((db, heads, vw), F32), grid_spec=grid_spec,
        compiler_params=_cparams(2), name="diff_decode",
    )(page_table.reshape(-1), lam.reshape(1), q, k_new, v_new, rbt,
      *([cache_k] * pps), *([cache_v] * pps), subln.reshape(1, vw))


def _router_kernel(x_ref, g_ref, wr_ref, br_ref, *refs, append):
    h_ref, idx_ref, gate_ref = refs[-3:]
    h = _rms(x_ref[...], g_ref[...])
    if append:
        tm = h.shape[0]
        h_ref[:tm] = h
        h_ref[tm:] = jnp.zeros_like(h)
    else:
        h_ref[...] = h
    logits = _nt_dot(wr_ref[...].astype(BF16), h.astype(BF16)) + br_ref[...]
    n_exp = logits.shape[0]
    expert = lax.broadcasted_iota(jnp.int32, logits.shape, 0)
    vals, idxs = [], []
    for _ in range(TOP_K):
        top = jnp.max(logits, axis=0, keepdims=True)
        idx = jnp.min(jnp.where(logits == top, expert, n_exp), axis=0, keepdims=True)
        vals.append(top)
        idxs.append(idx)
        logits = jnp.where(expert == idx, -jnp.inf, logits)
    top_val = jnp.concatenate(vals, axis=0)
    e = jnp.exp(top_val - top_val[0:1])
    gate_ref[...] = e / jnp.sum(e, axis=0, keepdims=True)
    idx_ref[...] = jnp.concatenate(idxs, axis=0)


def _router(x, gain, w_router, b_router, tm, table_rows=None, table=None, row_offset=0):
    t, d = x.shape
    n_exp = w_router.shape[1]
    append = table is not None
    operands = [x, gain.reshape(1, d), w_router.T, b_router.reshape(n_exp, 1)]
    in_specs = [pl.BlockSpec((tm, d), lambda i: (i, 0)), pl.BlockSpec((1, d), lambda i: (0, 0)),
                pl.BlockSpec((n_exp, d), lambda i: (0, 0)), pl.BlockSpec((n_exp, 1), lambda i: (0, 0))]
    if append:
        assert t == tm and row_offset % (2 * t) == 0
        block = row_offset // (2 * t)
        operands.append(table)
        in_specs.append(pl.BlockSpec(memory_space=pl.ANY))
        h_shape, h_spec = table.shape, pl.BlockSpec((2 * t, d), lambda i: (block, 0))
    else:
        h_shape, h_spec = (table_rows or t, d), pl.BlockSpec((tm, d), lambda i: (i, 0))
    return pl.pallas_call(
        functools.partial(_router_kernel, append=append),
        out_shape=(jax.ShapeDtypeStruct(h_shape, F32), jax.ShapeDtypeStruct((TOP_K, t), jnp.int32),
                   jax.ShapeDtypeStruct((TOP_K, t), F32)),
        grid=(t // tm,),
        in_specs=in_specs,
        out_specs=(h_spec, pl.BlockSpec((TOP_K, tm), lambda i: (0, i)),
                   pl.BlockSpec((TOP_K, tm), lambda i: (0, i))),
        input_output_aliases={len(operands) - 1: 0} if append else {},
        compiler_params=_cparams(1), name="router",
    )(*operands)


def _moe_plan(top_idx, n_exp, zero_token):
    n = top_idx.size
    n_chunks = n_exp + n // MOE_CHUNK + 1
    flat_e = top_idx.reshape(-1).astype(jnp.int32)
    counts = jnp.bincount(flat_e, length=n_exp).astype(jnp.int32)
    order = jnp.argsort(flat_e, stable=True).astype(jnp.int32)
    position = jnp.argsort(order).astype(jnp.int32)
    group_start = jnp.cumsum(counts) - counts
    chunks_e = (counts + MOE_CHUNK - 1) // MOE_CHUNK
    chunk_end_e = jnp.cumsum(chunks_e)
    chunk_start_e = chunk_end_e - chunks_e
    dest = chunk_start_e[flat_e] * MOE_CHUNK + position - group_start[flat_e]
    n_used = chunk_end_e[-1]
    cidx = jnp.arange(n_chunks, dtype=jnp.int32)
    c_exp = jnp.clip(jnp.searchsorted(chunk_end_e, cidx, side="right"), 0, n_exp - 1).astype(jnp.int32)
    c_cnt = jnp.clip(counts[c_exp] - (cidx - chunk_start_e[c_exp]) * MOE_CHUNK, 0, MOE_CHUNK)
    used = cidx < n_used
    c_exp = jnp.where(used, c_exp, c_exp[n_used - 1])
    c_cnt = jnp.where(used, c_cnt, 0).astype(jnp.int32)
    c_blk = jnp.where(used, cidx, n_used - 1).astype(jnp.int32)
    c_first = (group_start[c_exp] + (cidx - chunk_start_e[c_exp]) * MOE_CHUNK).astype(jnp.int32)
    return dict(n_chunks=n_chunks, dest=dest.astype(jnp.int32), c_exp=c_exp, c_cnt=c_cnt, c_blk=c_blk,
                n_used=n_used.reshape(1).astype(jnp.int32), c_first=c_first, order=order,
                zero_token=zero_token)


def _row_copy(src_hbm, row, dst_vmem, slot, sem):
    return pltpu.make_async_copy(src_hbm.at[pl.ds(row, 1), :], dst_vmem.at[pl.ds(slot, 1), :], sem)


def _moe_matmul_kernel(ce_ref, cnt_ref, blk_ref, nu_ref, *refs, gated, n_chunks, nj, zero_token):
    c = pl.program_id(0)
    j = pl.program_id(1)
    cnt = cnt_ref[c]
    if gated:
        (first_ref, order_ref, h_hbm, wg_ref, bg_ref, wu_ref, bu_ref, o_ref,
         wgb_ref, wub_ref, xbuf, stage, sem) = refs
        slot = c % 2
        piece = stage.shape[0]
        n_elem = order_ref.shape[0]

        def read_rows(rows):
            return xbuf[slot, rows, :]

        def piece_rows(chunk_cnt, lo):
            live = (chunk_cnt + MOE_TAIL - 1) // MOE_TAIL * MOE_TAIL
            return jnp.clip(live - lo, 0, piece)

        def start_rows(chunk, chunk_cnt, lo, n):
            first = first_ref[chunk]

            def issue(g, carry):
                for u in range(BF16_ROWS):
                    r = g * BF16_ROWS + u
                    elem = order_ref[jnp.minimum(first + lo + r, n_elem - 1)]
                    token = jnp.where(lo + r < chunk_cnt, elem // TOP_K, zero_token)
                    _row_copy(h_hbm, token, stage, r, sem).start()
                return carry

            lax.fori_loop(0, n // BF16_ROWS, issue, 0)

        def finish_rows(dst_slot, lo, n):
            def drain(g, carry):
                for u in range(BF16_ROWS):
                    _row_copy(h_hbm, 0, stage, g * BF16_ROWS + u, sem).wait()
                return carry

            def cast(g, carry):
                src = pl.ds(pl.multiple_of(g * BF16_ROWS, BF16_ROWS), BF16_ROWS)
                dst = pl.ds(pl.multiple_of(lo + g * BF16_ROWS, BF16_ROWS), BF16_ROWS)
                xbuf[dst_slot, dst, :] = stage[src, :].astype(BF16)
                return carry

            lax.fori_loop(0, n // BF16_ROWS, drain, 0)
            lax.fori_loop(0, n // BF16_ROWS, cast, 0)

        @pl.when(jnp.logical_and(c == 0, j == 0))
        def _():
            def first_chunk(p, carry):
                n = piece_rows(cnt_ref[0], p * piece)
                start_rows(0, cnt_ref[0], p * piece, n)
                finish_rows(0, p * piece, n)
                return carry

            lax.fori_loop(0, nj, first_chunk, 0)

        nxt = jnp.minimum(c + 1, n_chunks - 1)
        nxt_cnt = jnp.where(c + 1 < n_chunks, cnt_ref[nxt], 0)
        ahead = piece_rows(nxt_cnt, j * piece)
        start_rows(nxt, nxt_cnt, j * piece, ahead)
    else:
        x_ref, wg_ref, bg_ref, o_ref, wgb_ref = refs

        def read_rows(rows):
            return x_ref[rows, :]

    def cast_weights():
        wgb_ref[...] = wg_ref[0].astype(BF16)
        if gated:
            wub_ref[...] = wu_ref[0].astype(BF16)

    def rows_block(start, size):
        rows = pl.ds(pl.multiple_of(start, MOE_TAIL), size)
        xs = read_rows(rows)
        y = jnp.dot(xs, wgb_ref[...], preferred_element_type=F32) + bg_ref[0]
        if gated:
            up = jnp.dot(xs, wub_ref[...], preferred_element_type=F32) + bu_ref[0]
            gate = jnp.minimum(y, SWIGLU_LIMIT)
            up = jnp.clip(up, -SWIGLU_LIMIT, SWIGLU_LIMIT)
            y = (up + 1.0) * (gate * jax.nn.sigmoid(SWIGLU_ALPHA * gate))
        o_ref[rows, :] = y.astype(o_ref.dtype)

    units = (cnt + MOE_TAIL - 1) // MOE_TAIL
    top = MOE_BLOCKS[0]
    assert MOE_CHUNK < 2 * top
    has_top = units >= top // MOE_TAIL

    @pl.when(has_top)
    def _():
        cast_weights()
        rows_block(0, top)

    @pl.when(jnp.logical_and(jnp.logical_not(has_top), cnt > 0))
    def _():
        cast_weights()

    start = jnp.where(has_top, top, 0)
    units = units - start // MOE_TAIL
    for size in MOE_BLOCKS[1:]:
        take = units >= size // MOE_TAIL

        @pl.when(take)
        def _(start=start, size=size):
            rows_block(start, size)

        start = start + jnp.where(take, size, 0)
        units = units - jnp.where(take, size // MOE_TAIL, 0)

    if gated:
        finish_rows(1 - slot, j * piece, ahead)


def _moe_matmul(x, plan, weights, biases, out_dtype):
    gated = len(weights) == 2
    n_exp, k, n = weights[0].shape
    n_chunks = plan["n_chunks"]
    tn = MOE_TN if gated else MOE_DOWN_TN
    nj = n // tn
    n_prefetch = 6 if gated else 4

    def col(c, j, nu):
        return jnp.where(c < nu[0], j, nj - 1)

    def w_map(c, j, ce, cnt, blk, nu, *_):
        return (ce[c], 0, col(c, j, nu))

    def o_map(c, j, ce, cnt, blk, nu, *_):
        return (blk[c], col(c, j, nu))

    w_spec = pl.BlockSpec((1, k, tn), w_map)
    b_spec = pl.BlockSpec((1, 1, tn), w_map)
    scratch = [pltpu.VMEM((k, tn), BF16)] * len(weights)
    if gated:
        assert MOE_CHUNK % nj == 0 and (MOE_CHUNK // nj) % BF16_ROWS == 0
        prefetch = [plan["c_first"], plan["order"]]
        operands, in_specs = [x], [pl.BlockSpec(memory_space=pl.ANY)]
        scratch += [pltpu.VMEM((2, MOE_CHUNK, k), BF16), pltpu.VMEM((MOE_CHUNK // nj, k), F32),
                    pltpu.SemaphoreType.DMA]
    else:
        prefetch = []
        operands = [x]
        in_specs = [pl.BlockSpec((MOE_CHUNK, k), lambda c, j, ce, cnt, blk, nu: (blk[c], 0))]
    for w, b in zip(weights, biases):
        operands += [w, b.reshape(n_exp, 1, n)]
        in_specs += [w_spec, b_spec]
    grid_spec = pltpu.PrefetchScalarGridSpec(
        num_scalar_prefetch=n_prefetch, grid=(n_chunks, nj), in_specs=in_specs,
        out_specs=pl.BlockSpec((MOE_CHUNK, tn), o_map), scratch_shapes=scratch)
    return pl.pallas_call(
        functools.partial(_moe_matmul_kernel, gated=gated, n_chunks=n_chunks, nj=nj,
                          zero_token=plan["zero_token"]),
        out_shape=jax.ShapeDtypeStruct((n_chunks * MOE_CHUNK, n), out_dtype), grid_spec=grid_spec,
        compiler_params=_cparams(2), name="moe_up" if gated else "moe_down",
    )(plan["c_exp"], plan["c_cnt"], plan["c_blk"], plan["n_used"], *prefetch, *operands)


def _moe_combine_kernel(dest_ref, y_hbm, x_ref, g_ref, o_ref, buf, sem, *, tb):
    base = pl.program_id(0) * tb * TOP_K

    def issue(r, carry):
        for k in range(TOP_K):
            _row_copy(y_hbm, dest_ref[base + r * TOP_K + k], buf.at[k], r, sem).start(priority=k % DMA_THREADS)
        return carry

    def drain(r, carry):
        for k in range(TOP_K):
            _row_copy(y_hbm, 0, buf.at[k], r, sem).wait()
        return carry

    lax.fori_loop(0, tb, issue, 0, unroll=4)
    lax.fori_loop(0, tb, drain, 0, unroll=4)
    gates = g_ref[...]
    moe = gates[:, 0:1] * buf[0]
    for k in range(1, TOP_K):
        moe = moe + gates[:, k:k + 1] * buf[k]
    o_ref[...] = x_ref[...] + moe


def _moe_combine(y_rows, dest, x, gates, tb):
    t, d = x.shape
    grid_spec = pltpu.PrefetchScalarGridSpec(
        num_scalar_prefetch=1, grid=(t // tb,),
        in_specs=[pl.BlockSpec(memory_space=pl.ANY),
                  pl.BlockSpec((tb, d), lambda i, dest: (i, 0)),
                  pl.BlockSpec((tb, TOP_K), lambda i, dest: (i, 0))],
        out_specs=pl.BlockSpec((tb, d), lambda i, dest: (i, 0)),
        scratch_shapes=[pltpu.VMEM((TOP_K, tb, d), F32), pltpu.SemaphoreType.DMA])
    return pl.pallas_call(
        functools.partial(_moe_combine_kernel, tb=tb), out_shape=jax.ShapeDtypeStruct((t, d), F32),
        grid_spec=grid_spec, compiler_params=_cparams(1), name="moe_combine",
    )(dest, y_rows, x, gates)


def _pick_tile(n, candidates):
    for c in candidates:
        if n % c == 0:
            return c
    return n


def kernel(x_prompt, x_sample, cache_sb_k, cache_sb_v, cache_diff_k, cache_diff_v, page_table, attn_norm, w_in, diff_q_norm, diff_k_norm, diff_lambda_q1, diff_lambda_k1, diff_lambda_q2, diff_lambda_k2, rel_bias, sb_out_norm, diff_subln, w_out, ffn_norm, w_router, b_router, w_gate, b_gate, w_up, b_up, w_down, b_down):
    depth = attn_norm.shape[0]
    assert depth == 1, "single-layer trunk"
    bsz, seq, d = x_prompt.shape
    db, dseq, _ = x_sample.shape
    assert dseq == 1
    n_exp = w_router.shape[2]
    half = d // 2
    sb_heads = half // HEAD_DIM
    sb_kv = sb_heads // GROUP
    diff_heads = half // (2 * HEAD_DIM)
    diff_kv = diff_heads // GROUP
    sbq, sbk = sb_heads * HEAD_DIM, sb_kv * HEAD_DIM
    dq, dk = diff_heads * 2 * HEAD_DIM, diff_kv * 2 * HEAD_DIM
    off_sbk, off_sbv = sbq, sbq + sbk
    off_dq = sbq + 2 * sbk
    off_dk = off_dq + dq
    off_dv = off_dk + dk
    in_cols = off_dv + dk
    layer = 0
    lambda_init = 0.8 - 0.6 * math.exp(-0.3 * layer)
    lam = (jnp.exp(jnp.sum(diff_lambda_q1[layer] * diff_lambda_k1[layer]))
           - jnp.exp(jnp.sum(diff_lambda_q2[layer] * diff_lambda_k2[layer])) + lambda_init).astype(F32)

    tn_in = 512
    assert off_dq % tn_in == 0 and off_dv % tn_in == 0
    qk_gain = jnp.concatenate([jnp.ones((off_dq,), F32), jnp.tile(diff_q_norm[layer], dq // HEAD_DIM),
                               jnp.tile(diff_k_norm[layer], dk // HEAD_DIM), jnp.ones((dk,), F32)]).reshape(1, in_cols)
    norm_tiles = dict(norm_lo=off_dq // tn_in, norm_hi=off_dv // tn_in)
    n_tok = bsz * seq

    xp = x_prompt.reshape(n_tok, d)
    h_p = _rmsnorm(xp, attn_norm[layer], BF16, _pick_tile(n_tok, (256, 128, 8)))
    tm_p = _pick_tile(n_tok, (1024, 512, 256, 128, 8))
    proj_p = _matmul((h_p,), w_in[layer], qk_gain, mode="qknorm", tm=tm_p, tn=tn_in, **norm_tiles)
    proj3 = proj_p.reshape(bsz, seq, in_cols)
    mix_sb = _sb_prompt(proj3, sb_out_norm[layer].reshape(-1), kv_heads=sb_kv,
                        q_off=0, k_off=off_sbk, v_off=off_sbv)
    mix_d = _diff_prompt(proj3, rel_bias, lam, diff_subln[layer], kv_heads=diff_kv,
                         q_off=off_dq, k_off=off_dk, v_off=off_dv, out_scale=1.0 - lambda_init)
    x2_p = _matmul((mix_sb.reshape(n_tok, sbq), mix_d.reshape(n_tok, dq)), w_out[layer], xp,
                   mode="residual", tm=tm_p, tn=512)

    xs = x_sample.reshape(db, d)
    h_s = _rmsnorm(xs, attn_norm[layer], BF16, db)
    proj_s = _matmul((h_s,), w_in[layer], qk_gain, mode="qknorm", tm=db, tn=tn_in, **norm_tiles)
    pool = cache_sb_k.shape[1]
    page = cache_sb_k.shape[2]
    assert page == LANES
    rows4 = (pool, page * (sbk // HEAD_DIM), HEAD_DIM)
    assert sbk == dk == 4 * HEAD_DIM
    sb_o = _sb_decode(proj_s[:, :sbq].reshape(db, sb_heads, HEAD_DIM),
                      cache_sb_k.reshape(rows4), cache_sb_v.reshape(rows4),
                      page_table, sb_out_norm[layer], kv_heads=sb_kv)
    diff_v_rows = cache_diff_v.reshape(pool, page, diff_kv, 2, HEAD_DIM).transpose(0, 1, 3, 2, 4).reshape(rows4)
    q_d = proj_s[:, off_dq:off_dk].reshape(db, diff_heads, 2, HEAD_DIM).transpose(0, 2, 1, 3)
    rbt = jnp.concatenate([rel_bias.T, rel_bias.T], axis=0).astype(F32)
    d_o = _diff_decode(q_d.reshape(db, 2 * diff_heads, HEAD_DIM),
                       proj_s[:, off_dk:off_dv].reshape(db, 1, dk), proj_s[:, off_dv:].reshape(db, 1, dk),
                       cache_diff_k.reshape(rows4), diff_v_rows,
                       page_table, rbt, lam, diff_subln[layer], kv_heads=diff_kv, out_scale=1.0 - lambda_init)
    x2_s = _matmul((sb_o.reshape(db, sbq).astype(BF16), d_o.reshape(db, dq).astype(BF16)), w_out[layer], xs,
                   mode="residual", tm=db, tn=512)

    h_table, idx_p, gate_p = _router(x2_p, ffn_norm[layer], w_router[layer], b_router[layer],
                                     _pick_tile(n_tok, (256, 128)), table_rows=n_tok + 2 * db)
    h_all, idx_s, gate_s = _router(x2_s, ffn_norm[layer], w_router[layer], b_router[layer], db,
                                   table=h_table, row_offset=n_tok)
    top_idx = jnp.concatenate([idx_p, idx_s], axis=1).T
    plan = _moe_plan(top_idx, n_exp, zero_token=n_tok + db)
    act = _moe_matmul(h_all, plan, (w_gate[layer], w_up[layer]), (b_gate[layer], b_up[layer]), BF16)
    y_rows = _moe_matmul(act, plan, (w_down[layer],), (b_down[layer],), F32)
    dest = plan["dest"]
    y_p = _moe_combine(y_rows, dest[:n_tok * TOP_K], x2_p, gate_p.T, _pick_tile(n_tok, (COMBINE_TOKENS, 8)))
    y_s = _moe_combine(y_rows, dest[n_tok * TOP_K:], x2_s, gate_s.T, db)

    def rows(p, lead, lo, hi, shape):
        return p[:, lo:hi].reshape((depth,) + lead + shape)

    lead_p, lead_s = (bsz, seq), (db, dseq)
    return (y_p.reshape(bsz, seq, d), y_s.reshape(db, dseq, d),
            rows(proj_p, lead_p, off_sbk, off_sbv, (sb_kv, HEAD_DIM)),
            rows(proj_p, lead_p, off_sbv, off_dq, (sb_kv, HEAD_DIM)),
            rows(proj_p, lead_p, off_dk, off_dv, (diff_kv, 2, HEAD_DIM)),
            rows(proj_p, lead_p, off_dv, in_cols, (diff_kv, 2 * HEAD_DIM)),
            rows(proj_s, lead_s, off_sbk, off_sbv, (sb_kv, HEAD_DIM)),
            rows(proj_s, lead_s, off_sbv, off_dq, (sb_kv, HEAD_DIM)),
            rows(proj_s, lead_s, off_dk, off_dv, (diff_kv, 2, HEAD_DIM)),
            rows(proj_s, lead_s, off_dv, in_cols, (diff_kv, 2 * HEAD_DIM)))
```
